```python
import math
import jax
import jax.numpy as jnp
from jax import lax
import numpy as np

D_MODEL = 2048
BATCH = 4
SEQ = 4096
DEPTH = 2

N_HEADS = 8
HEAD_DIM = 128
KV_HEADS = 2
HEADS_PER_GROUP = N_HEADS // KV_HEADS
NSA_WIDTH = N_HEADS * HEAD_DIM
KV_WIDTH = KV_HEADS * HEAD_DIM
N_NSA_BRANCHES = 3
CMP_BLOCK = 32
CMP_STRIDE = 16
CMP_HIDDEN = 512
SEL_BLOCK = 64
SEL_TOPN = 16
SEL_QBLOCK = 64
WINDOW = 512
WIN_QBLOCK = 128
CONF_WIDTH = 1024
CONF_CONV_WIDTH = 31
SC_WIDTH = 1024
SC_CONV_WIDTH = 3
N_BRANCHES = 3
MIX_WIDTH = NSA_WIDTH + CONF_WIDTH + SC_WIDTH
IN_SPLITS = (NSA_WIDTH, KV_WIDTH, KV_WIDTH, KV_WIDTH, KV_WIDTH, KV_WIDTH, KV_WIDTH,
             N_HEADS * N_NSA_BRANCHES, 2 * CONF_WIDTH, 3 * SC_WIDTH, N_BRANCHES * D_MODEL)
IN_WIDTH = NSA_WIDTH + 6 * KV_WIDTH + N_HEADS * N_NSA_BRANCHES + 2 * CONF_WIDTH + 3 * SC_WIDTH + N_BRANCHES * D_MODEL
REL_BUCKETS = 32
REL_MAX_DIST = 128
N_GROUPS = 4
EXPERTS_PER_GROUP = 8
N_EXPERTS = N_GROUPS * EXPERTS_PER_GROUP
TOPK_IN_GROUP = 2
D_EXPERT = 512
PLE_DIM = 256

EPS = 1e-6
NEG_INF = -1e30
FORCE_SCORE = 1e9

kernel_name = "hybrid_nsa_conformer_shortconv_hiermoe"


def rms_norm(x, g):
    xf = x.astype(jnp.float32)
    var = jnp.mean(xf * xf, axis=-1, keepdims=True)
    return (xf * lax.rsqrt(var + EPS) * g.astype(jnp.float32)).astype(x.dtype)


def layer_norm(x, g, b):
    xf = x.astype(jnp.float32)
    mu = jnp.mean(xf, axis=-1, keepdims=True)
    var = jnp.mean(jnp.square(xf - mu), axis=-1, keepdims=True)
    y = (xf - mu) * lax.rsqrt(var + EPS) * g.astype(jnp.float32) + b.astype(jnp.float32)
    return y.astype(x.dtype)


def split_cols(z, sizes):
    cuts = [int(c) for c in np.cumsum(sizes)[:-1]]
    return jnp.split(z, cuts, axis=-1)


def causal_depthwise_conv(x, w):
    width, ch = w.shape
    return lax.conv_general_dilated(
        x, w[:, None, :].astype(x.dtype), window_strides=(1,), padding=[(width - 1, 0)],
        dimension_numbers=("NWC", "WIO", "NWC"), feature_group_count=ch)


def t5_bucket(dist):
    n = jnp.maximum(dist, 0)
    max_exact = REL_BUCKETS // 2
    nf = jnp.maximum(n, 1).astype(jnp.float32)
    large = max_exact + (jnp.log(nf / max_exact) / math.log(REL_MAX_DIST / max_exact)
                         * (REL_BUCKETS - max_exact)).astype(jnp.int32)
    large = jnp.minimum(large, REL_BUCKETS - 1)
    return jnp.where(n < max_exact, n, large)


def to_q_heads(q):
    b, s, _ = q.shape
    return q.reshape(b, s, KV_HEADS, HEADS_PER_GROUP, HEAD_DIM).transpose(0, 2, 3, 1, 4)


def to_kv_heads(k):
    b, s, _ = k.shape
    return k.reshape(b, s, KV_HEADS, HEAD_DIM).transpose(0, 2, 1, 3)


def compress_blocks(blocks, pe, w1, w2):
    b, g, nc, l, hd = blocks.shape
    flat = (blocks + pe.astype(blocks.dtype)).reshape(b, g, nc, l * hd)
    return jax.nn.gelu(flat @ w1) @ w2


def compressed_branch(q, k_cmp, v_cmp, rel_tab, scale):
    s = q.shape[3]
    n_cmp = k_cmp.shape[2]
    t = jnp.arange(s)
    blk_end = jnp.arange(n_cmp) * CMP_STRIDE + CMP_BLOCK - 1
    dist = t[:, None] - blk_end[None, :]
    valid = dist >= 0
    bias = jnp.transpose(rel_tab[t5_bucket(dist)], (2, 3, 0, 1))
    logits = jnp.einsum("bghsd,bgcd->bghsc", q, k_cmp,
                        preferred_element_type=jnp.float32) * scale + bias
    logits = jnp.where(valid, logits, NEG_INF)
    p_cmp = jnp.where(valid, jax.nn.softmax(logits, axis=-1), 0.0)
    o = jnp.einsum("bghsc,bgcd->bghsd", p_cmp.astype(v_cmp.dtype), v_cmp)
    return o, p_cmp


def select_blocks(p_cmp, s):
    n_cmp = p_cmp.shape[-1]
    n_sel_blocks = s // SEL_BLOCK
    c_start = jnp.arange(n_cmp)[:, None] * CMP_STRIDE
    blk = jnp.arange(n_sel_blocks)[None, :]
    overlap = ((c_start < (blk + 1) * SEL_BLOCK) &
               (c_start + CMP_BLOCK > blk * SEL_BLOCK)).astype(jnp.float32)
    imp = jnp.einsum("bghsc,cn->bgsn", p_cmp, overlap)
    t = jnp.arange(s)[:, None]
    cur = t // SEL_BLOCK
    forced = (blk == 0) | (blk == cur) | (blk == cur - 1)
    causal = blk * SEL_BLOCK <= t
    score = jnp.where(forced, FORCE_SCORE, jnp.where(causal, imp, NEG_INF))
    n_top = min(SEL_TOPN, n_sel_blocks)
    _, idx = lax.top_k(score, n_top)
    return idx


def selected_branch(q, k_s, v_s, sel_idx, rel_tab, scale):
    b, g, hg, s, hd = q.shape
    n_top = sel_idx.shape[-1]
    n_chunks = s // SEL_QBLOCK
    n_keys = n_top * SEL_BLOCK
    kb = k_s.reshape(b, g, s // SEL_BLOCK, SEL_BLOCK, hd)
    vb = v_s.reshape(b, g, s // SEL_BLOCK, SEL_BLOCK, hd)
    bi = jnp.arange(b)[:, None, None, None]
    gi = jnp.arange(g)[None, :, None, None]
    offs = jnp.arange(SEL_BLOCK)
    q_c = jnp.moveaxis(q.reshape(b, g, hg, n_chunks, SEL_QBLOCK, hd), 3, 0)
    idx_c = jnp.moveaxis(sel_idx.reshape(b, g, n_chunks, SEL_QBLOCK, n_top), 2, 0)
    t_c = jnp.arange(s).reshape(n_chunks, SEL_QBLOCK)

    def one_step(args):
        qc, ic, tc = args
        kg = kb[bi, gi, ic].reshape(b, g, SEL_QBLOCK, n_keys, hd)
        vg = vb[bi, gi, ic].reshape(b, g, SEL_QBLOCK, n_keys, hd)
        kpos = (ic[..., None] * SEL_BLOCK + offs).reshape(b, g, SEL_QBLOCK, n_keys)
        dist = tc[:, None] - kpos
        bias = jnp.moveaxis(rel_tab[t5_bucket(dist), gi], -1, 2)
        logits = jnp.einsum("bghqd,bgqkd->bghqk", qc, kg,
                            preferred_element_type=jnp.float32) * scale + bias
        logits = jnp.where((dist >= 0)[:, :, None], logits, NEG_INF)
        p = jax.nn.softmax(logits, axis=-1)
        return jnp.einsum("bghqk,bgqkd->bghqd", p.astype(vg.dtype), vg)

    o = lax.map(one_step, (q_c, idx_c, t_c))
    return jnp.moveaxis(o, 0, 3).reshape(b, g, hg, s, hd)


def window_branch(q, k_w, v_w, rel_tab, scale):
    b, g, hg, s, hd = q.shape
    nb = s // WIN_QBLOCK
    n_prev = WINDOW // WIN_QBLOCK
    n_keys = (n_prev + 1) * WIN_QBLOCK

    def banded(z):
        zb = z.reshape(b, g, nb, WIN_QBLOCK, hd)
        zp = jnp.pad(zb, ((0, 0), (0, 0), (n_prev, 0), (0, 0), (0, 0)))
        return jnp.concatenate([zp[:, :, j:j + nb] for j in range(n_prev + 1)], axis=3)

    kw, vw = banded(k_w), banded(v_w)
    qw = q.reshape(b, g, hg, nb, WIN_QBLOCK, hd)
    dist = jnp.arange(WIN_QBLOCK)[:, None] + n_prev * WIN_QBLOCK - jnp.arange(n_keys)[None, :]
    kpos = jnp.arange(nb)[:, None] * WIN_QBLOCK - n_prev * WIN_QBLOCK + jnp.arange(n_keys)[None, :]
    mask = (dist >= 0) & (dist < WINDOW) & (kpos >= 0)[:, None, :]
    bias = jnp.transpose(rel_tab[t5_bucket(dist)], (2, 3, 0, 1))
    logits = jnp.einsum("bghnqd,bgnkd->bghnqk", qw, kw,
                        preferred_element_type=jnp.float32) * scale + bias[:, :, None]
    logits = jnp.where(mask, logits, NEG_INF)
    p = jax.nn.softmax(logits, axis=-1)
    o = jnp.einsum("bghnqk,bgnkd->bghnqd", p.astype(vw.dtype), vw)
    return o.reshape(b, g, hg, s, hd)


def native_sparse_attention(q, kc, vc, ks, vs, kw, vw, gate_logits, rel_tab, cmp_pe, cmp_w1, cmp_w2):
    b, s, _ = q.shape
    scale = HEAD_DIM ** -0.5
    qh = to_q_heads(q)
    kc_h, vc_h = to_kv_heads(kc), to_kv_heads(vc)
    n_cmp = (s - CMP_BLOCK) // CMP_STRIDE + 1
    blk_idx = jnp.arange(n_cmp)[:, None] * CMP_STRIDE + jnp.arange(CMP_BLOCK)[None, :]
    k_cmp = compress_blocks(kc_h[:, :, blk_idx], cmp_pe[0], cmp_w1[0], cmp_w2[0])
    v_cmp = compress_blocks(vc_h[:, :, blk_idx], cmp_pe[1], cmp_w1[1], cmp_w2[1])
    o_cmp, p_cmp = compressed_branch(qh, k_cmp, v_cmp, rel_tab, scale)
    sel_idx = select_blocks(p_cmp, s)
    o_slc = selected_branch(qh, to_kv_heads(ks), to_kv_heads(vs), sel_idx, rel_tab, scale)
    o_win = window_branch(qh, to_kv_heads(kw), to_kv_heads(vw), rel_tab, scale)
    gts = jax.nn.sigmoid(gate_logits.astype(jnp.float32)).reshape(
        b, s, KV_HEADS, HEADS_PER_GROUP, N_NSA_BRANCHES).transpose(0, 2, 3, 1, 4)
    o = gts[..., 0:1] * o_cmp + gts[..., 1:2] * o_slc + gts[..., 2:3] * o_win
    return o.astype(q.dtype).transpose(0, 3, 1, 2, 4).reshape(b, s, NSA_WIDTH)


def conformer_conv(glu_in, conv_w, conv_b, ln_g, ln_b):
    a, gate = jnp.split(glu_in, 2, axis=-1)
    u = a * jax.nn.sigmoid(gate)
    u = causal_depthwise_conv(u, conv_w) + conv_b.astype(u.dtype)
    return jax.nn.silu(layer_norm(u, ln_g, ln_b))


def short_gated_conv(sc_in, conv_w):
    bg, cg, xs = jnp.split(sc_in, 3, axis=-1)
    return bg * causal_depthwise_conv(cg * xs, conv_w)


def hierarchical_moe(h, rg_w, rg_b, re_w, re_b, we_g, we_u, we_d):
    t = h.shape[0]
    g_logits = (h @ rg_w).astype(jnp.float32) + rg_b.astype(jnp.float32)
    g_prob = jax.nn.softmax(g_logits, axis=-1)
    grp = jnp.argmax(g_logits, axis=-1)
    g_w = jnp.take_along_axis(g_prob, grp[:, None], axis=-1)
    e_logits = ((h @ re_w).astype(jnp.float32) + re_b.astype(jnp.float32)).reshape(
        t, N_GROUPS, EXPERTS_PER_GROUP)
    e_logits = jnp.take_along_axis(e_logits, grp[:, None, None], axis=1)[:, 0]
    e_prob = jax.nn.softmax(e_logits, axis=-1)
    top_p, top_i = lax.top_k(e_prob, TOPK_IN_GROUP)
    top_p = top_p / jnp.sum(top_p, axis=-1, keepdims=True)
    weights = g_w * top_p
    eid = grp[:, None] * EXPERTS_PER_GROUP + top_i
    combine = jnp.sum(jax.nn.one_hot(eid, N_EXPERTS, dtype=jnp.float32) * weights[..., None], axis=1)
    y = jnp.zeros(h.shape, jnp.float32)
    for e in range(N_EXPERTS):
        he = jax.nn.silu(h @ we_g[e]) * (h @ we_u[e])
        y = y + combine[:, e:e + 1] * (he @ we_d[e])
    return y.astype(h.dtype)


def hybrid_layer(x, p_i, rel_tab, attn_norm, w_in, cmp_pe, cmp_w1, cmp_w2,
                 conf_conv_w, conf_conv_b, conf_ln_g, conf_ln_b, sc_conv_w, w_branch, w_out,
                 ffn_norm, rg_w, rg_b, re_w, re_b, we_g, we_u, we_d,
                 ple_norm, ple_gate_w, ple_proj_w):
    b, s, d = x.shape
    h = rms_norm(x, attn_norm)
    z = h @ w_in
    q, kc, vc, ks, vs, kw, vw, nsa_g, conf_in, sc_in, merge_g = split_cols(z, IN_SPLITS)
    o_nsa = native_sparse_attention(q, kc, vc, ks, vs, kw, vw, nsa_g, rel_tab, cmp_pe, cmp_w1, cmp_w2)
    o_conf = conformer_conv(conf_in, conf_conv_w, conf_conv_b, conf_ln_g, conf_ln_b)
    o_sc = short_gated_conv(sc_in, sc_conv_w)
    gates = jax.nn.sigmoid(merge_g).reshape(b, s, N_BRANCHES, d)
    w_nsa, w_conf, w_sc = jnp.split(w_branch, [NSA_WIDTH, NSA_WIDTH + CONF_WIDTH], axis=0)
    merged = (gates[:, :, 0] * (o_nsa @ w_nsa) + gates[:, :, 1] * (o_conf @ w_conf)
              + gates[:, :, 2] * (o_sc @ w_sc))
    x = x + merged @ w_out
    h2 = rms_norm(x, ffn_norm).reshape(b * s, d)
    x = x + hierarchical_moe(h2, rg_w, rg_b, re_w, re_b, we_g, we_u, we_d).reshape(b, s, d)
    gate = jax.nn.sigmoid(rms_norm(x, ple_norm) @ ple_gate_w)
    return x + gate * (p_i @ ple_proj_w)


def setup_inputs(seed: int = 0) -> dict:
    key = jax.random.key(seed)
    ks = jax.random.split(key, 27)
    f32 = jnp.float32

    def nrm(k, shape, scale):
        return jax.random.normal(k, shape, f32) * scale

    def gain(k, shape):
        return 1.0 + 0.01 * jax.random.normal(k, shape, f32)

    L = DEPTH
    return {
        "x": nrm(ks[0], (BATCH, SEQ, D_MODEL), 1.0),
        "p": nrm(ks[1], (DEPTH, BATCH, SEQ, PLE_DIM), 1.0),
        "rel_bias": nrm(ks[2], (REL_BUCKETS, N_HEADS), 0.2),
        "attn_norm": gain(ks[3], (L, D_MODEL)),
        "w_in": nrm(ks[4], (L, D_MODEL, IN_WIDTH), D_MODEL ** -0.5),
        "cmp_pe": nrm(ks[5], (L, 2, CMP_BLOCK, HEAD_DIM), 0.1),
        "cmp_w1": nrm(ks[6], (L, 2, CMP_BLOCK * HEAD_DIM, CMP_HIDDEN), (CMP_BLOCK * HEAD_DIM) ** -0.5),
        "cmp_w2": nrm(ks[7], (L, 2, CMP_HIDDEN, HEAD_DIM), CMP_HIDDEN ** -0.5),
        "conf_conv_w": nrm(ks[8], (L, CONF_CONV_WIDTH, CONF_WIDTH), CONF_CONV_WIDTH ** -0.5),
        "conf_conv_b": nrm(ks[9], (L, CONF_WIDTH), 0.01),
        "conf_ln_g": gain(ks[10], (L, CONF_WIDTH)),
        "conf_ln_b": nrm(ks[11], (L, CONF_WIDTH), 0.01),
        "sc_conv_w": nrm(ks[12], (L, SC_CONV_WIDTH, SC_WIDTH), SC_CONV_WIDTH ** -0.5),
        "w_branch": nrm(ks[13], (L, MIX_WIDTH, D_MODEL), NSA_WIDTH ** -0.5),
        "w_out": nrm(ks[14], (L, D_MODEL, D_MODEL), D_MODEL ** -0.5),
        "ffn_norm": gain(ks[15], (L, D_MODEL)),
        "router_group_w": nrm(ks[16], (L, D_MODEL, N_GROUPS), D_MODEL ** -0.5),
        "router_group_b": nrm(ks[17], (L, N_GROUPS), 0.01),
        "router_expert_w": nrm(ks[18], (L, D_MODEL, N_EXPERTS), D_MODEL ** -0.5),
        "router_expert_b": nrm(ks[19], (L, N_EXPERTS), 0.01),
        "expert_w_gate": nrm(ks[20], (L, N_EXPERTS, D_MODEL, D_EXPERT), D_MODEL ** -0.5),
        "expert_w_up": nrm(ks[21], (L, N_EXPERTS, D_MODEL, D_EXPERT), D_MODEL ** -0.5),
        "expert_w_down": nrm(ks[22], (L, N_EXPERTS, D_EXPERT, D_MODEL), D_EXPERT ** -0.5),
        "ple_norm": gain(ks[23], (L, D_MODEL)),
        "ple_gate_w": nrm(ks[24], (L, D_MODEL, D_MODEL), D_MODEL ** -0.5),
        "ple_proj_w": nrm(ks[25], (L, PLE_DIM, D_MODEL), PLE_DIM ** -0.5),
        "final_norm": gain(ks[26], (D_MODEL,)),
    }


def reference(x, p, rel_bias, attn_norm, w_in, cmp_pe, cmp_w1, cmp_w2,
              conf_conv_w, conf_conv_b, conf_ln_g, conf_ln_b, sc_conv_w, w_branch, w_out,
              ffn_norm, router_group_w, router_group_b, router_expert_w, router_expert_b,
              expert_w_gate, expert_w_up, expert_w_down, ple_norm, ple_gate_w, ple_proj_w,
              final_norm):
    rel_tab = rel_bias.reshape(REL_BUCKETS, KV_HEADS, HEADS_PER_GROUP)
    for i in range(DEPTH):
        x = hybrid_layer(
            x, p[i], rel_tab, attn_norm[i], w_in[i], cmp_pe[i], cmp_w1[i], cmp_w2[i],
            conf_conv_w[i], conf_conv_b[i], conf_ln_g[i], conf_ln_b[i], sc_conv_w[i],
            w_branch[i], w_out[i], ffn_norm[i], router_group_w[i], router_group_b[i],
            router_expert_w[i], router_expert_b[i], expert_w_gate[i], expert_w_up[i],
            expert_w_down[i], ple_norm[i], ple_gate_w[i], ple_proj_w[i])
    return rms_norm(x, final_norm)
```

```python
import functools
import math

import jax
import jax.numpy as jnp
from jax import lax
from jax.experimental import pallas as pl
from jax.experimental.pallas import tpu as pltpu

D_MODEL = 2048
N_HEADS = 8
HEAD_DIM = 128
KV_HEADS = 2
HEADS_PER_GROUP = N_HEADS // KV_HEADS
NSA_WIDTH = N_HEADS * HEAD_DIM
KV_WIDTH = KV_HEADS * HEAD_DIM
N_NSA_BRANCHES = 3
CMP_BLOCK = 32
CMP_STRIDE = 16
CMP_HIDDEN = 512
SEL_BLOCK = 64
SEL_TOPN = 16
WINDOW = 512
WIN_QBLOCK = 128
CONF_WIDTH = 1024
CONF_CONV_WIDTH = 31
SC_WIDTH = 1024
SC_CONV_WIDTH = 3
REL_BUCKETS = 32
REL_MAX_DIST = 128
N_GROUPS = 4
EXPERTS_PER_GROUP = 8
N_EXPERTS = N_GROUPS * EXPERTS_PER_GROUP
D_EXPERT = 512
PLE_DIM = 256
EPS = 1e-6
NEG_INF = -1e30
FORCE_SCORE = 1e9
SCALE = HEAD_DIM ** -0.5

LANE = 128
VMEM_LIMIT = 56 * 1024 * 1024

OFF_MERGE = 0
OFF_SC = OFF_MERGE + 3 * D_MODEL
OFF_CONF = OFF_SC + 3 * SC_WIDTH
OFF_Q = OFF_CONF + 2 * CONF_WIDTH
OFF_KC = OFF_Q + NSA_WIDTH
OFF_VC = OFF_KC + KV_WIDTH
OFF_KS = OFF_VC + KV_WIDTH
OFF_VS = OFF_KS + KV_WIDTH
OFF_KW = OFF_VS + KV_WIDTH
OFF_VW = OFF_KW + KV_WIDTH
OFF_NSAG = OFF_VW + KV_WIDTH
Z_WIDTH = OFF_NSAG + 512

QW = HEADS_PER_GROUP * HEAD_DIM
TQ = 128
SEL_TK = 256
N_PREV = WINDOW // WIN_QBLOCK
ROUTE_LANES = LANE
TMX = 256


def _cparams(sem, vmem=VMEM_LIMIT):
    return pltpu.CompilerParams(dimension_semantics=sem, vmem_limit_bytes=vmem)


def _t5_bucket(dist):
    n = jnp.maximum(dist, 0)
    max_exact = REL_BUCKETS // 2
    nf = jnp.maximum(n, 1).astype(jnp.float32)
    large = max_exact + (jnp.log(nf / max_exact) / math.log(REL_MAX_DIST / max_exact)
                         * (REL_BUCKETS - max_exact)).astype(jnp.int32)
    large = jnp.minimum(large, REL_BUCKETS - 1)
    return jnp.where(n < max_exact, n, large)


def _bias_table_kernel(rel_ref, o_ref, *, col_mult, d0_base, d0_step, lo, hi, rows, cols):
    h = pl.program_id(0)
    k = pl.program_id(1)
    rb = pl.program_id(2)
    r = lax.broadcasted_iota(jnp.int32, (rows, cols), 0) + rb * rows
    c = lax.broadcasted_iota(jnp.int32, (rows, cols), 1)
    dist = r - col_mult * c + d0_base + d0_step * k
    bucket = _t5_bucket(dist)
    val = jnp.zeros((rows, cols), jnp.float32)
    for b in range(REL_BUCKETS):
        val = jnp.where(bucket == b, rel_ref[b, h], val)
    ok = jnp.where(dist >= lo, jnp.where(dist < hi, 1.0, 0.0), 0.0)
    o_ref[0, 0] = jnp.where(ok > 0.5, val, NEG_INF)


def _bias_tables(rel_bias, n_k, n_rows, rows, cols, **kw):
    kern = functools.partial(_bias_table_kernel, rows=rows, cols=cols, **kw)
    return pl.pallas_call(
        kern,
        grid=(N_HEADS, n_k, n_rows // rows),
        in_specs=[pl.BlockSpec(memory_space=pltpu.SMEM)],
        out_specs=pl.BlockSpec((1, 1, rows, cols), lambda h, k, r: (h, k, r, 0)),
        out_shape=jax.ShapeDtypeStruct((N_HEADS, n_k, n_rows, cols), jnp.float32),
        compiler_params=_cparams(("parallel", "parallel", "parallel")),
        name="bias_tables",
    )(rel_bias)


def _inproj_kernel(x_ref, g_ref, w_ref, o_ref, h_scr):
    @pl.when(pl.program_id(1) == 0)
    def _():
        x = x_ref[...]
        var = jnp.mean(x * x, axis=-1, keepdims=True)
        h_scr[...] = (x * lax.rsqrt(var + EPS) * g_ref[...]).astype(jnp.bfloat16)

    o_ref[...] = jnp.dot(h_scr[...], w_ref[...], preferred_element_type=jnp.float32).astype(o_ref.dtype)


def _in_proj(x, g, w, tm, tn):
    t = x.shape[0]
    return pl.pallas_call(
        _inproj_kernel,
        grid=(t // tm, Z_WIDTH // tn),
        in_specs=[pl.BlockSpec((tm, D_MODEL), lambda i, j: (i, 0)),
                  pl.BlockSpec((1, D_MODEL), lambda i, j: (0, 0)),
                  pl.BlockSpec((D_MODEL, tn), lambda i, j: (0, j))],
        out_specs=pl.BlockSpec((tm, tn), lambda i, j: (i, j)),
        out_shape=jax.ShapeDtypeStruct((t, Z_WIDTH), jnp.bfloat16),
        scratch_shapes=[pltpu.VMEM((tm, D_MODEL), jnp.bfloat16)],
        compiler_params=_cparams(("parallel", "arbitrary")),
        name="in_proj",
    )(x, g, w)


def _compress_kernel(a_ref, pe_ref, w1_ref, w2_ref, o_ref, *, ncp):
    lo = jnp.zeros((ncp, CMP_HIDDEN), jnp.float32)
    hi = jnp.zeros((ncp, CMP_HIDDEN), jnp.float32)
    for l in range(CMP_STRIDE):
        a = a_ref[0, 0, l].astype(jnp.float32)
        a_lo = (a + pe_ref[0, l:l + 1, :]).astype(jnp.bfloat16)
        a_hi = (a + pe_ref[0, CMP_STRIDE + l:CMP_STRIDE + l + 1, :]).astype(jnp.bfloat16)
        lo = lo + jnp.dot(a_lo, w1_ref[0, l * HEAD_DIM:(l + 1) * HEAD_DIM, :],
                          preferred_element_type=jnp.float32)
        hi = hi + jnp.dot(a_hi, w1_ref[0, (CMP_STRIDE + l) * HEAD_DIM:(CMP_STRIDE + l + 1) * HEAD_DIM, :],
                          preferred_element_type=jnp.float32)
    hidden = lo + pltpu.roll(hi, ncp - 1, 0)
    act = jax.nn.gelu(hidden).astype(jnp.bfloat16)
    out = jnp.dot(act, w2_ref[0], preferred_element_type=jnp.float32)
    row = lax.broadcasted_iota(jnp.int32, (ncp, HEAD_DIM), 0)
    o_ref[0, 0] = jnp.where(row < ncp - 1, out, 0.0).astype(o_ref.dtype)


def _compress(a, pe, w1, w2):
    b, _, _, ncp, _ = a.shape
    return pl.pallas_call(
        functools.partial(_compress_kernel, ncp=ncp),
        grid=(b, 2 * KV_HEADS),
        in_specs=[pl.BlockSpec((1, 1, CMP_STRIDE, ncp, HEAD_DIM), lambda i, j: (i, j, 0, 0, 0)),
                  pl.BlockSpec((1, CMP_BLOCK, HEAD_DIM), lambda i, j: (j // KV_HEADS, 0, 0)),
                  pl.BlockSpec((1, CMP_BLOCK * HEAD_DIM, CMP_HIDDEN), lambda i, j: (j // KV_HEADS, 0, 0)),
                  pl.BlockSpec((1, CMP_HIDDEN, HEAD_DIM), lambda i, j: (j // KV_HEADS, 0, 0))],
        out_specs=pl.BlockSpec((1, 1, ncp, HEAD_DIM), lambda i, j: (i, j, 0, 0)),
        out_shape=jax.ShapeDtypeStruct((b, 2 * KV_HEADS, ncp, HEAD_DIM), jnp.bfloat16),
        compiler_params=_cparams(("parallel", "parallel")),
        name="compress",
    )(a, pe, w1, w2)


def _cmp_attn_kernel(q_ref, kc_ref, vc_ref, tab_ref, o_ref, neg_ref, *, tq, ncp, nsel):
    i = pl.program_id(2)
    kc = kc_ref[0, 0]
    vc = vc_ref[0, 0]
    psum = jnp.zeros((tq, ncp), jnp.float32)
    for h in range(HEADS_PER_GROUP):
        sl = slice(h * HEAD_DIM, (h + 1) * HEAD_DIM)
        tab = tab_ref[h, 0]
        s = lax.dot_general(q_ref[:, sl], kc, (((1,), (1,)), ((), ())),
                            preferred_element_type=jnp.float32) * SCALE + tab
        valid = tab > 0.5 * NEG_INF
        m = jnp.max(s, axis=-1, keepdims=True)
        e = jnp.where(valid, jnp.exp(s - m), 0.0)
        den = jnp.sum(e, axis=-1, keepdims=True)
        p = jnp.where(valid, e / den, 0.0)
        o_ref[:, sl] = jnp.dot(p.astype(jnp.bfloat16), vc, preferred_element_type=jnp.float32).astype(o_ref.dtype)
        psum = psum + p
    c_start = lax.broadcasted_iota(jnp.int32, (ncp, nsel), 0) * CMP_STRIDE
    blk_n = lax.broadcasted_iota(jnp.int32, (ncp, nsel), 1)
    ov = jnp.where(c_start < (blk_n + 1) * SEL_BLOCK,
                   jnp.where(c_start + CMP_BLOCK > blk_n * SEL_BLOCK, 1.0, 0.0), 0.0)
    imp = jnp.dot(psum, ov, precision=lax.Precision.HIGHEST, preferred_element_type=jnp.float32)
    t = lax.broadcasted_iota(jnp.int32, (tq, nsel), 0) + i * tq
    blk = lax.broadcasted_iota(jnp.int32, (tq, nsel), 1)
    cur = lax.shift_right_logical(t, 6)
    forced = jnp.where(blk == 0, 1.0, jnp.where(blk == cur, 1.0, jnp.where(blk == cur - 1, 1.0, 0.0)))
    causal = blk * SEL_BLOCK <= t
    score = jnp.where(forced > 0.5, FORCE_SCORE, jnp.where(causal, imp, NEG_INF))
    rank = jnp.zeros((tq, nsel), jnp.float32)
    for k in range(nsel):
        col = score[:, k:k + 1]
        rank = rank + jnp.where(blk > k, jnp.where(col >= score, 1.0, 0.0), jnp.where(col > score, 1.0, 0.0))
    n_top = min(SEL_TOPN, nsel)
    neg = jnp.where(rank < n_top, jnp.where(causal, 0.0, NEG_INF), NEG_INF)
    neg_ref[0, 0] = neg.astype(neg_ref.dtype)


def _cmp_attn(z, cmp_kv, tab, b, s):
    ncp = s // CMP_STRIDE
    nsel = s // SEL_BLOCK
    nq = s // TQ
    kern = functools.partial(_cmp_attn_kernel, tq=TQ, ncp=ncp, nsel=nsel)
    return pl.pallas_call(
        kern,
        grid=(b, KV_HEADS, nq),
        in_specs=[pl.BlockSpec((TQ, QW), lambda bi, g, i: (bi * nq + i, OFF_Q // QW + g)),
                  pl.BlockSpec((1, 1, ncp, HEAD_DIM), lambda bi, g, i: (bi, g, 0, 0)),
                  pl.BlockSpec((1, 1, ncp, HEAD_DIM), lambda bi, g, i: (bi, KV_HEADS + g, 0, 0)),
                  pl.BlockSpec((HEADS_PER_GROUP, 1, TQ, ncp), lambda bi, g, i: (g, 0, i, 0))],
        out_specs=[pl.BlockSpec((TQ, QW), lambda bi, g, i: (bi * nq + i, g)),
                   pl.BlockSpec((1, 1, TQ, nsel), lambda bi, g, i: (bi, g, i, 0))],
        out_shape=[jax.ShapeDtypeStruct((b * s, NSA_WIDTH), jnp.bfloat16),
                   jax.ShapeDtypeStruct((b, KV_HEADS, s, nsel), jnp.bfloat16)],
        compiler_params=_cparams(("parallel", "parallel", "parallel")),
        name="cmp_attn",
    )(z, cmp_kv, cmp_kv, tab)


def _flash_kernel(*refs, selected, tq, tk, nsel):
    if selected:
        q_ref, k_ref, v_ref, tab_ref, neg_ref, o_ref, m_scr, l_scr, acc_scr = refs
    else:
        q_ref, k_ref, v_ref, tab_ref, o_ref, m_scr, l_scr, acc_scr = refs
    i = pl.program_id(2)
    rows = HEADS_PER_GROUP * tq
    q4 = jnp.concatenate([q_ref[:, h * HEAD_DIM:(h + 1) * HEAD_DIM] for h in range(HEADS_PER_GROUP)], axis=0)
    m_scr[...] = jnp.full((rows, LANE), NEG_INF, jnp.float32)
    l_scr[...] = jnp.zeros((rows, LANE), jnp.float32)
    acc_scr[...] = jnp.zeros((rows, HEAD_DIM), jnp.float32)
    sub = tk // LANE
    if selected:
        neg = neg_ref[0, 0]
        lo = 0
        hi = (i * tq + tq - 1) // tk + 1
        n_far = tab_ref.shape[1] - 1
    else:
        lo = jnp.maximum(N_PREV - i, 0)
        hi = N_PREV + 1

    def body(j, carry):
        if selected:
            kb = j
            tidx = jnp.minimum(i * (tq // LANE) - j * sub, n_far)
        else:
            kb = i - N_PREV + j
            tidx = j
        start = pl.multiple_of(kb * tk, tk)
        kt = k_ref[pl.ds(start, tk), :]
        vt = v_ref[pl.ds(start, tk), :]
        s = lax.dot_general(q4, kt, (((1,), (1,)), ((), ())), preferred_element_type=jnp.float32) * SCALE
        s = s + tab_ref[0, tidx]
        if selected:
            n_i = lax.broadcasted_iota(jnp.int32, (nsel, tk), 0)
            c_i = lax.broadcasted_iota(jnp.int32, (nsel, tk), 1)
            expand = jnp.where(lax.shift_right_logical(c_i + kb * tk, 6) == n_i, 1.0, 0.0).astype(jnp.bfloat16)
            add = jnp.dot(neg, expand, preferred_element_type=jnp.float32)
            s = s + jnp.concatenate([add] * HEADS_PER_GROUP, axis=0)
        m_prev = m_scr[...]
        m_new = jnp.maximum(m_prev, jnp.max(s, axis=-1, keepdims=True))
        alpha = jnp.exp(m_prev - m_new)
        p = jnp.exp(s - jnp.concatenate([m_new] * sub, axis=1))
        l_scr[...] = alpha * l_scr[...] + jnp.sum(p, axis=-1, keepdims=True)
        acc_scr[...] = alpha * acc_scr[...] + jnp.dot(p.astype(jnp.bfloat16), vt,
                                                      preferred_element_type=jnp.float32)
        m_scr[...] = m_new
        return carry

    lax.fori_loop(lo, hi, body, 0)
    out = acc_scr[...] / l_scr[...]
    for h in range(HEADS_PER_GROUP):
        o_ref[:, h * HEAD_DIM:(h + 1) * HEAD_DIM] = out[h * tq:(h + 1) * tq].astype(o_ref.dtype)


def _flash(z, tab, neg, b, s, *, selected):
    nq = s // TQ
    tk = SEL_TK if selected else WIN_QBLOCK
    nsel = s // SEL_BLOCK
    k_off, v_off = (OFF_KS, OFF_VS) if selected else (OFF_KW, OFF_VW)
    rows = HEADS_PER_GROUP * TQ
    n_tab = tab.shape[1]
    in_specs = [pl.BlockSpec((TQ, QW), lambda bi, g, i: (bi * nq + i, OFF_Q // QW + g)),
                pl.BlockSpec((s, HEAD_DIM), lambda bi, g, i: (bi, k_off // HEAD_DIM + g)),
                pl.BlockSpec((s, HEAD_DIM), lambda bi, g, i: (bi, v_off // HEAD_DIM + g)),
                pl.BlockSpec((1, n_tab, rows, tk), lambda bi, g, i: (g, 0, 0, 0))]
    args = [z, z, z, tab]
    if selected:
        in_specs.append(pl.BlockSpec((1, 1, TQ, nsel), lambda bi, g, i: (bi, g, i, 0)))
        args.append(neg)
    kern = functools.partial(_flash_kernel, selected=selected, tq=TQ, tk=tk, nsel=nsel)
    return pl.pallas_call(
        kern,
        grid=(b, KV_HEADS, nq),
        in_specs=in_specs,
        out_specs=pl.BlockSpec((TQ, QW), lambda bi, g, i: (bi * nq + i, g)),
        out_shape=jax.ShapeDtypeStruct((b * s, NSA_WIDTH), jnp.bfloat16),
        scratch_shapes=[pltpu.VMEM((rows, LANE), jnp.float32), pltpu.VMEM((rows, LANE), jnp.float32),
                        pltpu.VMEM((rows, HEAD_DIM), jnp.float32)],
        compiler_params=_cparams(("parallel", "parallel", "parallel")),
        name="flash_sel" if selected else "flash_win",
    )(*args)


HALO = 32
CONV_RC = 64


def _conv_kernel(a_ref, g_ref, ah_ref, gh_ref, bg_ref, cg_ref, xs_ref, cgh_ref, xsh_ref,
                 cw_ref, cb_ref, sw_ref, uo_ref, so_ref, ext_scr, ext2_scr, *, ts):
    first = pl.program_id(1) == 0
    f32 = jnp.float32
    u = a_ref[...].astype(f32) * jax.nn.sigmoid(g_ref[...].astype(f32))
    uh = ah_ref[...].astype(f32) * jax.nn.sigmoid(gh_ref[...].astype(f32))
    ext_scr[0:HALO, :] = jnp.where(first, 0.0, uh)
    ext_scr[HALO:HALO + ts, :] = u
    v = cg_ref[...].astype(f32) * xs_ref[...].astype(f32)
    vh = cgh_ref[...].astype(f32) * xsh_ref[...].astype(f32)
    ext2_scr[0:HALO, :] = jnp.where(first, 0.0, vh)
    ext2_scr[HALO:HALO + ts, :] = v
    base = HALO - (CONF_CONV_WIDTH - 1)
    base2 = HALO - (SC_CONV_WIDTH - 1)
    for r0 in range(0, ts, CONV_RC):
        acc = jnp.zeros((CONV_RC, a_ref.shape[1]), f32) + cb_ref[...]
        for k in range(CONF_CONV_WIDTH):
            acc = acc + cw_ref[k:k + 1, :] * ext_scr[r0 + base + k:r0 + base + k + CONV_RC, :]
        uo_ref[r0:r0 + CONV_RC, :] = acc.astype(uo_ref.dtype)
        acc2 = jnp.zeros((CONV_RC, a_ref.shape[1]), f32)
        for k in range(SC_CONV_WIDTH):
            acc2 = acc2 + sw_ref[k:k + 1, :] * ext2_scr[r0 + base2 + k:r0 + base2 + k + CONV_RC, :]
        so_ref[r0:r0 + CONV_RC, :] = (bg_ref[r0:r0 + CONV_RC, :].astype(f32) * acc2).astype(so_ref.dtype)


def _conv(z, cw, cb, sw, b, s, ts, tc):
    ns = s // ts
    t = b * s

    def cur(off):
        return pl.BlockSpec((ts, tc), lambda bi, i, c: (bi * ns + i, off // tc + c))

    def halo(off):
        return pl.BlockSpec((HALO, tc), lambda bi, i, c: (jnp.maximum((bi * s + i * ts) // HALO - 1, 0), off // tc + c))

    return pl.pallas_call(
        functools.partial(_conv_kernel, ts=ts),
        grid=(b, ns, CONF_WIDTH // tc),
        in_specs=[cur(OFF_CONF), cur(OFF_CONF + CONF_WIDTH), halo(OFF_CONF), halo(OFF_CONF + CONF_WIDTH),
                  cur(OFF_SC), cur(OFF_SC + SC_WIDTH), cur(OFF_SC + 2 * SC_WIDTH),
                  halo(OFF_SC + SC_WIDTH), halo(OFF_SC + 2 * SC_WIDTH),
                  pl.BlockSpec((CONF_CONV_WIDTH, tc), lambda bi, i, c: (0, c)),
                  pl.BlockSpec((1, tc), lambda bi, i, c: (0, c)),
                  pl.BlockSpec((SC_CONV_WIDTH, tc), lambda bi, i, c: (0, c))],
        out_specs=[pl.BlockSpec((ts, tc), lambda bi, i, c: (bi * ns + i, c)),
                   pl.BlockSpec((ts, tc), lambda bi, i, c: (bi * ns + i, c))],
        out_shape=[jax.ShapeDtypeStruct((t, CONF_WIDTH), jnp.bfloat16),
                   jax.ShapeDtypeStruct((t, SC_WIDTH), jnp.bfloat16)],
        scratch_shapes=[pltpu.VMEM((HALO + ts, tc), jnp.float32), pltpu.VMEM((HALO + ts, tc), jnp.float32)],
        compiler_params=_cparams(("parallel", "parallel", "parallel")),
        name="conv",
    )(z, z, z, z, z, z, z, z, z, cw, cb, sw)


def _merge_kernel(ocmp_ref, oslc_ref, owin_ref, ng_ref, uc_ref, lng_ref, lnb_ref, osc_ref,
                  mg0_ref, mg1_ref, mg2_ref, wn_ref, wc_ref, ws_ref, o_ref, nsa_scr, conf_scr):
    f32 = jnp.float32

    @pl.when(pl.program_id(1) == 0)
    def _():
        gt = jax.nn.sigmoid(ng_ref[...].astype(f32))
        for h in range(N_HEADS):
            sl = slice(h * HEAD_DIM, (h + 1) * HEAD_DIM)
            c = N_NSA_BRANCHES * h
            o = (gt[:, c:c + 1] * ocmp_ref[:, sl].astype(f32) + gt[:, c + 1:c + 2] * oslc_ref[:, sl].astype(f32)
                 + gt[:, c + 2:c + 3] * owin_ref[:, sl].astype(f32))
            nsa_scr[:, sl] = o.astype(nsa_scr.dtype)
        u = uc_ref[...].astype(f32)
        mu = jnp.mean(u, axis=-1, keepdims=True)
        var = jnp.mean(jnp.square(u - mu), axis=-1, keepdims=True)
        y = (u - mu) * lax.rsqrt(var + EPS) * lng_ref[...] + lnb_ref[...]
        conf_scr[...] = (y * jax.nn.sigmoid(y)).astype(conf_scr.dtype)

    a = jnp.dot(nsa_scr[...], wn_ref[...], preferred_element_type=f32)
    b = jnp.dot(conf_scr[...], wc_ref[...], preferred_element_type=f32)
    c = jnp.dot(osc_ref[...], ws_ref[...], preferred_element_type=f32)
    m = (jax.nn.sigmoid(mg0_ref[...].astype(f32)) * a + jax.nn.sigmoid(mg1_ref[...].astype(f32)) * b
         + jax.nn.sigmoid(mg2_ref[...].astype(f32)) * c)
    o_ref[...] = m.astype(o_ref.dtype)


def _merge(z, ocmp, oslc, owin, uconv, lng, lnb, osc, wb, tm, tn):
    t = z.shape[0]
    nj = D_MODEL // tn
    row = lambda w: pl.BlockSpec((tm, w), lambda i, j: (i, 0))
    return pl.pallas_call(
        _merge_kernel,
        grid=(t // tm, nj),
        in_specs=[row(NSA_WIDTH), row(NSA_WIDTH), row(NSA_WIDTH),
                  pl.BlockSpec((tm, LANE), lambda i, j: (i, OFF_NSAG // LANE)),
                  row(CONF_WIDTH),
                  pl.BlockSpec((1, CONF_WIDTH), lambda i, j: (0, 0)),
                  pl.BlockSpec((1, CONF_WIDTH), lambda i, j: (0, 0)),
                  row(SC_WIDTH),
                  pl.BlockSpec((tm, tn), lambda i, j: (i, j)),
                  pl.BlockSpec((tm, tn), lambda i, j: (i, nj + j)),
                  pl.BlockSpec((tm, tn), lambda i, j: (i, 2 * nj + j)),
                  pl.BlockSpec((NSA_WIDTH, tn), lambda i, j: (0, j)),
                  pl.BlockSpec((CONF_WIDTH, tn), lambda i, j: (NSA_WIDTH // CONF_WIDTH, j)),
                  pl.BlockSpec((SC_WIDTH, tn), lambda i, j: ((NSA_WIDTH + CONF_WIDTH) // SC_WIDTH, j))],
        out_specs=pl.BlockSpec((tm, tn), lambda i, j: (i, j)),
        out_shape=jax.ShapeDtypeStruct((t, D_MODEL), jnp.bfloat16),
        scratch_shapes=[pltpu.VMEM((tm, NSA_WIDTH), jnp.bfloat16), pltpu.VMEM((tm, CONF_WIDTH), jnp.bfloat16)],
        compiler_params=_cparams(("parallel", "arbitrary")),
        name="merge",
    )(ocmp, oslc, owin, z, uconv, lng, lnb, osc, z, z, z, wb, wb, wb)


R_E1, R_E2, R_W1, R_W2, R_RANK1, R_RANK2 = range(6)
GROUP_LANE0 = N_EXPERTS


def _out_router_kernel(x_ref, m_ref, wo_ref, fn_ref, wr_ref, br_ref, x1_ref, h2_ref, route_ref, cnt_ref,
                       base_scr, *, tm):
    f32 = jnp.float32

    @pl.when(pl.program_id(0) == 0)
    def _():
        base_scr[...] = jnp.zeros_like(base_scr)

    x1 = x_ref[...] + jnp.dot(m_ref[...], wo_ref[...], preferred_element_type=f32)
    x1_ref[...] = x1
    var = jnp.mean(x1 * x1, axis=-1, keepdims=True)
    h2 = x1 * lax.rsqrt(var + EPS) * fn_ref[...]
    h2_ref[...] = h2
    logits = jnp.dot(h2, wr_ref[...], precision=lax.Precision.HIGHEST, preferred_element_type=f32) + br_ref[...]
    lane = lax.broadcasted_iota(jnp.int32, (tm, ROUTE_LANES), 1).astype(f32)
    big = float(ROUTE_LANES)
    is_g = jnp.where(lane >= GROUP_LANE0, jnp.where(lane < GROUP_LANE0 + N_GROUPS, 1.0, 0.0), 0.0) > 0.5
    gl = jnp.where(is_g, logits, NEG_INF)
    gmax = jnp.max(gl, axis=-1, keepdims=True)
    glane = jnp.min(jnp.where(gl == gmax, lane, big), axis=-1, keepdims=True)
    gsum = jnp.sum(jnp.where(is_g, jnp.exp(gl - gmax), 0.0), axis=-1, keepdims=True)
    g_w = 1.0 / gsum
    grp = glane - GROUP_LANE0
    in_grp = jnp.floor(lane * (1.0 / EXPERTS_PER_GROUP)) == grp
    el = jnp.where(in_grp, logits, NEG_INF)
    emax = jnp.max(el, axis=-1, keepdims=True)
    ee = jnp.where(in_grp, jnp.exp(el - emax), 0.0)
    ep = ee / jnp.sum(ee, axis=-1, keepdims=True)
    ep = jnp.where(in_grp, ep, -1.0)
    p1 = jnp.max(ep, axis=-1, keepdims=True)
    i1 = jnp.min(jnp.where(ep == p1, lane, big), axis=-1, keepdims=True)
    ep2 = jnp.where(lane == i1, -1.0, ep)
    p2 = jnp.max(ep2, axis=-1, keepdims=True)
    i2 = jnp.min(jnp.where(ep2 == p2, lane, big), axis=-1, keepdims=True)
    psum = p1 + p2
    w1 = g_w * (p1 / psum)
    w2 = g_w * (p2 / psum)
    onehot = jnp.where(lane == i1, 1.0, jnp.where(lane == i2, 1.0, 0.0))
    r_i = lax.broadcasted_iota(jnp.int32, (tm, tm), 0)
    c_i = lax.broadcasted_iota(jnp.int32, (tm, tm), 1)
    tri = jnp.where(c_i < r_i, 1.0, 0.0).astype(jnp.bfloat16)
    cum = jnp.dot(tri, onehot.astype(jnp.bfloat16), preferred_element_type=f32) + base_scr[0:1, :]
    rank1 = jnp.sum(jnp.where(lane == i1, cum, 0.0), axis=-1, keepdims=True)
    rank2 = jnp.sum(jnp.where(lane == i2, cum, 0.0), axis=-1, keepdims=True)
    new_base = base_scr[0:1, :] + jnp.sum(onehot, axis=0, keepdims=True)
    base_scr[...] = jnp.broadcast_to(new_base, base_scr.shape)
    cnt_ref[...] = jnp.broadcast_to(new_base, cnt_ref.shape)
    rec = jnp.zeros((tm, ROUTE_LANES), f32)
    for ln, val in ((R_E1, i1), (R_E2, i2), (R_W1, w1), (R_W2, w2), (R_RANK1, rank1), (R_RANK2, rank2)):
        rec = jnp.where(lane == ln, val, rec)
    route_ref[...] = rec


def _out_router(x, merged, wo, fn, wr, br, tm):
    t = x.shape[0]
    n = t // tm
    return pl.pallas_call(
        functools.partial(_out_router_kernel, tm=tm),
        grid=(n,),
        in_specs=[pl.BlockSpec((tm, D_MODEL), lambda i: (i, 0)),
                  pl.BlockSpec((tm, D_MODEL), lambda i: (i, 0)),
                  pl.BlockSpec((D_MODEL, D_MODEL), lambda i: (0, 0)),
                  pl.BlockSpec((1, D_MODEL), lambda i: (0, 0)),
                  pl.BlockSpec((D_MODEL, ROUTE_LANES), lambda i: (0, 0)),
                  pl.BlockSpec((1, ROUTE_LANES), lambda i: (0, 0))],
        out_specs=[pl.BlockSpec((tm, D_MODEL), lambda i: (i, 0)),
                   pl.BlockSpec((tm, D_MODEL), lambda i: (i, 0)),
                   pl.BlockSpec((tm, ROUTE_LANES), lambda i: (i, 0)),
                   pl.BlockSpec((8, ROUTE_LANES), lambda i: (i, 0))],
        out_shape=[jax.ShapeDtypeStruct((t, D_MODEL), jnp.float32),
                   jax.ShapeDtypeStruct((t, D_MODEL), jnp.float32),
                   jax.ShapeDtypeStruct((t, ROUTE_LANES), jnp.float32),
                   jax.ShapeDtypeStruct((n * 8, ROUTE_LANES), jnp.float32)],
        scratch_shapes=[pltpu.VMEM((8, ROUTE_LANES), jnp.float32)],
        compiler_params=_cparams(("arbitrary",)),
        name="out_router",
    )(x, merged, wo, fn, wr, br)


def _dispatch_kernel(zs_ref, dest_ref, h_ref, xs_ref, zero_scr, sem, zsem, *, tmd):
    @pl.when(pl.program_id(0) == 0)
    def _():
        zero_scr[...] = jnp.zeros_like(zero_scr)
        for e in range(N_EXPERTS):
            @pl.when(zs_ref[e] >= 0)
            def _():
                cp = pltpu.make_async_copy(zero_scr, xs_ref.at[pl.ds(pl.multiple_of(zs_ref[e], TMX), TMX)], zsem)
                cp.start()
                cp.wait()

        def zero_tail(tile, c):
            cp = pltpu.make_async_copy(zero_scr, xs_ref.at[pl.ds(pl.multiple_of(tile * TMX, TMX), TMX)], zsem)
            cp.start()
            cp.wait()
            return c

        lax.fori_loop(zs_ref[N_EXPERTS] // TMX, xs_ref.shape[0] // TMX, zero_tail, 0)

    def row_copy(r, k):
        return pltpu.make_async_copy(h_ref.at[pl.ds(r, 1)], xs_ref.at[pl.ds(dest_ref[0, 0, 2 * r + k], 1)], sem)

    def issue(r, c):
        row_copy(r, 0).start()
        row_copy(r, 1).start()
        return c

    lax.fori_loop(0, tmd, issue, 0)

    def drain(r, c):
        row_copy(r, 0).wait()
        row_copy(r, 1).wait()
        return c

    lax.fori_loop(0, tmd, drain, 0)


def _dispatch(zstart, dest3, h2, p_rows, tmd):
    t = h2.shape[0]
    grid_spec = pltpu.PrefetchScalarGridSpec(
        num_scalar_prefetch=1,
        grid=(t // tmd,),
        in_specs=[pl.BlockSpec((1, 1, 2 * tmd), lambda i, zs: (i, 0, 0), memory_space=pltpu.SMEM),
                  pl.BlockSpec((tmd, D_MODEL), lambda i, zs: (i, 0))],
        out_specs=pl.BlockSpec(memory_space=pl.ANY),
        scratch_shapes=[pltpu.VMEM((TMX, D_MODEL), jnp.float32), pltpu.SemaphoreType.DMA(()),
                        pltpu.SemaphoreType.DMA(())],
    )
    return pl.pallas_call(
        functools.partial(_dispatch_kernel, tmd=tmd),
        grid_spec=grid_spec,
        out_shape=jax.ShapeDtypeStruct((p_rows, D_MODEL), jnp.float32),
        compiler_params=_cparams(("arbitrary",)),
        name="dispatch",
    )(zstart, dest3, h2)


def _expert_kernel(te_ref, tv_ref, tb_ref, xs_ref, wg_ref, wu_ref, wd_ref, y_ref, wgu_scr, wd_scr):
    i = pl.program_id(0)
    e = te_ref[i]
    prev = te_ref[jnp.maximum(i - 1, 0)]

    @pl.when(jnp.logical_or(i == 0, e != prev))
    def _():
        wgu_scr[:, 0:D_EXPERT] = wg_ref[0, 0].astype(jnp.bfloat16)
        wgu_scr[:, D_EXPERT:2 * D_EXPERT] = wu_ref[0, 0].astype(jnp.bfloat16)
        wd_scr[...] = wd_ref[0, 0].astype(jnp.bfloat16)

    @pl.when(tv_ref[i] == 1)
    def _():
        x = xs_ref[...].astype(jnp.bfloat16)
        gu = jnp.dot(x, wgu_scr[...], preferred_element_type=jnp.float32)
        gate = gu[:, 0:D_EXPERT]
        he = (gate * jax.nn.sigmoid(gate)) * gu[:, D_EXPERT:2 * D_EXPERT]
        y_ref[...] = jnp.dot(he.astype(jnp.bfloat16), wd_scr[...], preferred_element_type=jnp.float32)

    @pl.when(tv_ref[i] == 0)
    def _():
        y_ref[...] = jnp.zeros_like(y_ref)


def _experts(tile_e, tile_v, tile_b, xs, wg, wu, wd, layer):
    p_rows = xs.shape[0]
    grid_spec = pltpu.PrefetchScalarGridSpec(
        num_scalar_prefetch=3,
        grid=(p_rows // TMX,),
        in_specs=[pl.BlockSpec((TMX, D_MODEL), lambda i, te, tv, tb: (tb[i], 0)),
                  pl.BlockSpec((1, 1, D_MODEL, D_EXPERT), lambda i, te, tv, tb: (layer, te[i], 0, 0)),
                  pl.BlockSpec((1, 1, D_MODEL, D_EXPERT), lambda i, te, tv, tb: (layer, te[i], 0, 0)),
                  pl.BlockSpec((1, 1, D_EXPERT, D_MODEL), lambda i, te, tv, tb: (layer, te[i], 0, 0))],
        out_specs=pl.BlockSpec((TMX, D_MODEL), lambda i, te, tv, tb: (i, 0)),
        scratch_shapes=[pltpu.VMEM((D_MODEL, 2 * D_EXPERT), jnp.bfloat16),
                        pltpu.VMEM((D_EXPERT, D_MODEL), jnp.bfloat16)],
    )
    return pl.pallas_call(
        _expert_kernel,
        grid_spec=grid_spec,
        out_shape=jax.ShapeDtypeStruct((p_rows, D_MODEL), jnp.float32),
        compiler_params=_cparams(("arbitrary",)),
        name="experts",
    )(tile_e, tile_v, tile_b, xs, wg, wu, wd)


def _combine_ple_kernel(dest_ref, x1_ref, route_ref, y_ref, p_ref, pn_ref, wpg_ref, wpp_ref, fn_ref, o_ref,
                        buf, sem, *, tmc, final):
    f32 = jnp.float32

    def row_copy(r, k):
        return pltpu.make_async_copy(y_ref.at[pl.ds(dest_ref[0, 0, 2 * r + k], 1)], buf.at[k, pl.ds(r, 1)], sem)

    def issue(r, c):
        row_copy(r, 0).start()
        row_copy(r, 1).start()
        return c

    lax.fori_loop(0, tmc, issue, 0)

    def drain(r, c):
        row_copy(r, 0).wait()
        row_copy(r, 1).wait()
        return c

    lax.fori_loop(0, tmc, drain, 0)

    route = route_ref[...]
    x2 = x1_ref[...] + route[:, R_W1:R_W1 + 1] * buf[0] + route[:, R_W2:R_W2 + 1] * buf[1]
    var = jnp.mean(x2 * x2, axis=-1, keepdims=True)
    hp = (x2 * lax.rsqrt(var + EPS) * pn_ref[...]).astype(jnp.bfloat16)
    gate = jax.nn.sigmoid(jnp.dot(hp, wpg_ref[...], preferred_element_type=f32))
    pp = jnp.dot(p_ref[...].astype(jnp.bfloat16), wpp_ref[...], preferred_element_type=f32)
    x3 = x2 + gate * pp
    if final:
        var3 = jnp.mean(x3 * x3, axis=-1, keepdims=True)
        x3 = x3 * lax.rsqrt(var3 + EPS) * fn_ref[...]
    o_ref[...] = x3


def _combine_ple(dest3, x1, route, y, p, pn, wpg, wpp, fn, tmc, final):
    t = x1.shape[0]
    return pl.pallas_call(
        functools.partial(_combine_ple_kernel, tmc=tmc, final=final),
        grid=(t // tmc,),
        in_specs=[pl.BlockSpec((1, 1, 2 * tmc), lambda i: (i, 0, 0), memory_space=pltpu.SMEM),
                  pl.BlockSpec((tmc, D_MODEL), lambda i: (i, 0)),
                  pl.BlockSpec((tmc, ROUTE_LANES), lambda i: (i, 0)),
                  pl.BlockSpec(memory_space=pl.ANY),
                  pl.BlockSpec((tmc, PLE_DIM), lambda i: (i, 0)),
                  pl.BlockSpec((1, D_MODEL), lambda i: (0, 0)),
                  pl.BlockSpec((D_MODEL, D_MODEL), lambda i: (0, 0)),
                  pl.BlockSpec((PLE_DIM, D_MODEL), lambda i: (0, 0)),
                  pl.BlockSpec((1, D_MODEL), lambda i: (0, 0))],
        out_specs=pl.BlockSpec((tmc, D_MODEL), lambda i: (i, 0)),
        out_shape=jax.ShapeDtypeStruct((t, D_MODEL), jnp.float32),
        scratch_shapes=[pltpu.VMEM((2, tmc, D_MODEL), jnp.float32), pltpu.SemaphoreType.DMA(())],
        compiler_params=_cparams(("arbitrary",)),
        name="combine_ple",
    )(dest3, x1, route, y, p, pn, wpg, wpp, fn)


def _regroup_w_in(w):
    o_q, o_kv, o_ng = 0, NSA_WIDTH, NSA_WIDTH + 6 * KV_WIDTH
    o_conf = o_ng + N_HEADS * N_NSA_BRANCHES
    o_sc = o_conf + 2 * CONF_WIDTH
    o_mg = o_sc + 3 * SC_WIDTH
    pad = jnp.zeros((w.shape[0], Z_WIDTH - OFF_NSAG - N_HEADS * N_NSA_BRANCHES), w.dtype)
    return jnp.concatenate([w[:, o_mg:], w[:, o_sc:o_mg], w[:, o_conf:o_sc], w[:, o_q:o_ng],
                            w[:, o_ng:o_conf], pad], axis=1).astype(jnp.bfloat16)


def _route_plan(route, counts, t, tmd):
    eid = route[:, R_E1:R_E2 + 1].astype(jnp.int32)
    rank = route[:, R_RANK1:R_RANK2 + 1].astype(jnp.int32)
    cnt = counts[:N_EXPERTS].astype(jnp.int32)
    padded = ((cnt + TMX - 1) // TMX) * TMX
    ends = jnp.cumsum(padded)
    starts = ends - padded
    dest = jnp.take(starts, eid) + rank
    p_rows = 2 * t + N_EXPERTS * TMX
    n_tiles = p_rows // TMX
    tile_start = jnp.arange(n_tiles, dtype=jnp.int32) * TMX
    tile_e = jnp.minimum(jnp.searchsorted(ends, tile_start, side="right"), N_EXPERTS - 1).astype(jnp.int32)
    tile_v = (tile_start < ends[-1]).astype(jnp.int32)
    tile_b = jnp.where(tile_v == 1, jnp.arange(n_tiles, dtype=jnp.int32), 0)
    zstart = jnp.concatenate([jnp.where(padded > cnt, ends - TMX, -1), ends[-1:]]).astype(jnp.int32)
    return dest.reshape(t // tmd, 1, 2 * tmd), tile_e, tile_v, tile_b, zstart, p_rows


def kernel(x, p, rel_bias, attn_norm, w_in, cmp_pe, cmp_w1, cmp_w2, conf_conv_w, conf_conv_b, conf_ln_g, conf_ln_b, sc_conv_w, w_branch, w_out, ffn_norm, router_group_w, router_group_b, router_expert_w, router_expert_b, expert_w_gate, expert_w_up, expert_w_down, ple_norm, ple_gate_w, ple_proj_w, final_norm):
    b, s, d = x.shape
    t = b * s
    depth = w_in.shape[0]
    bf16 = jnp.bfloat16
    ncp = s // CMP_STRIDE
    rows4 = HEADS_PER_GROUP * TQ

    cmp_tab = _bias_tables(rel_bias, 1, s, TQ, ncp, col_mult=CMP_STRIDE, d0_base=-(CMP_BLOCK - 1), d0_step=0,
                           lo=0, hi=1 << 30)
    toe = _bias_tables(rel_bias, 6, TQ, TQ, LANE, col_mult=1, d0_base=-LANE, d0_step=LANE,
                       lo=0, hi=WINDOW)
    toe = toe.reshape(KV_HEADS, HEADS_PER_GROUP, 6, TQ, LANE).transpose(0, 2, 1, 3, 4).reshape(KV_HEADS, 6, rows4, LANE)
    sel_tab = jnp.stack([jnp.concatenate([toe[:, dd + 1], toe[:, dd]], axis=-1) for dd in range(4)], axis=1)
    win_tab = jnp.stack([toe[:, N_PREV - jj + 1] for jj in range(N_PREV + 1)], axis=1)

    x2d = x.reshape(t, d)
    tm_in = min(1024, t)
    tm = min(512, t)
    tmd = min(256, t)
    for i in range(depth):
        w_z = _regroup_w_in(w_in[i])
        wr = jnp.zeros((d, ROUTE_LANES), jnp.float32)
        wr = wr.at[:, 0:N_EXPERTS].set(router_expert_w[i]).at[:, GROUP_LANE0:GROUP_LANE0 + N_GROUPS].set(router_group_w[i])
        br = jnp.zeros((1, ROUTE_LANES), jnp.float32)
        br = br.at[0, 0:N_EXPERTS].set(router_expert_b[i]).at[0, GROUP_LANE0:GROUP_LANE0 + N_GROUPS].set(router_group_b[i])

        z = _in_proj(x2d, attn_norm[i].reshape(1, d), w_z, tm_in, 1024)
        kcv = z[:, OFF_KC:OFF_KC + 2 * KV_WIDTH].reshape(b, ncp, CMP_STRIDE, 2 * KV_HEADS, HEAD_DIM)
        kcv = kcv.transpose(0, 3, 2, 1, 4)
        cmp_kv = _compress(kcv, cmp_pe[i], cmp_w1[i].astype(bf16), cmp_w2[i].astype(bf16))
        o_cmp, neg = _cmp_attn(z, cmp_kv, cmp_tab, b, s)
        o_slc = _flash(z, sel_tab, neg, b, s, selected=True)
        o_win = _flash(z, win_tab, None, b, s, selected=False)
        uconv, o_sc = _conv(z, conf_conv_w[i], conf_conv_b[i].reshape(1, -1), sc_conv_w[i], b, s, min(512, s), 256)
        merged = _merge(z, o_cmp, o_slc, o_win, uconv, conf_ln_g[i].reshape(1, -1), conf_ln_b[i].reshape(1, -1),
                        o_sc, w_branch[i].astype(bf16), tm, 1024)

        x1, h2, route, cnts = _out_router(x2d, merged, w_out[i].astype(bf16), ffn_norm[i].reshape(1, d), wr, br, tm)
        dest3, tile_e, tile_v, tile_b, zstart, p_rows = _route_plan(route, cnts[-1], t, tmd)
        xs = _dispatch(zstart, dest3, h2, p_rows, tmd)
        y = _experts(tile_e, tile_v, tile_b, xs, expert_w_gate, expert_w_up, expert_w_down, i)

        x2d = _combine_ple(dest3, x1, route, y, p[i].reshape(t, PLE_DIM), ple_norm[i].reshape(1, d),
                           ple_gate_w[i].astype(bf16), ple_proj_w[i].astype(bf16), final_norm.reshape(1, d),
                           tmd, i == depth - 1)
    return x2d.reshape(b, s, d)
```

```python
import functools
import math

import jax
import jax.numpy as jnp
from jax import lax
from jax.experimental import pallas as pl
from jax.experimental.pallas import tpu as pltpu

D_MODEL = 2048
N_HEADS = 8
HEAD_DIM = 128
KV_HEADS = 2
HEADS_PER_GROUP = N_HEADS // KV_HEADS
NSA_WIDTH = N_HEADS * HEAD_DIM
KV_WIDTH = KV_HEADS * HEAD_DIM
N_NSA_BRANCHES = 3
CMP_BLOCK = 32
CMP_STRIDE = 16
CMP_HIDDEN = 512
SEL_BLOCK = 64
SEL_TOPN = 16
WINDOW = 512
WIN_QBLOCK = 128
CONF_WIDTH = 1024
CONF_CONV_WIDTH = 31
SC_WIDTH = 1024
SC_CONV_WIDTH = 3
REL_BUCKETS = 32
REL_MAX_DIST = 128
N_GROUPS = 4
EXPERTS_PER_GROUP = 8
N_EXPERTS = N_GROUPS * EXPERTS_PER_GROUP
D_EXPERT = 512
PLE_DIM = 256
EPS = 1e-6
NEG_INF = -1e30
FORCE_SCORE = 1e9
SCALE = HEAD_DIM ** -0.5
LOG2E = math.log2(math.e)

LANE = 128
SUBLANE = 8
VMEM_LIMIT = 56 * 1024 * 1024

OFF_MERGE = 0
OFF_SC = OFF_MERGE + 3 * D_MODEL
OFF_CONF = OFF_SC + 3 * SC_WIDTH
OFF_Q = OFF_CONF + 2 * CONF_WIDTH
OFF_KC = OFF_Q + NSA_WIDTH
OFF_VC = OFF_KC + KV_WIDTH
OFF_KS = OFF_VC + KV_WIDTH
OFF_VS = OFF_KS + KV_WIDTH
OFF_KW = OFF_VS + KV_WIDTH
OFF_VW = OFF_KW + KV_WIDTH
OFF_NSAG = OFF_VW + KV_WIDTH
Z_WIDTH = OFF_NSAG + 512

QW = HEADS_PER_GROUP * HEAD_DIM
TQ = 128
SEL_TQ = 256
SEL_TK = 256
N_PREV = WINDOW // WIN_QBLOCK
ROUTE_LANES = LANE
TMX = 256


def _cparams(sem, vmem=VMEM_LIMIT):
    return pltpu.CompilerParams(dimension_semantics=sem, vmem_limit_bytes=vmem)


def _t5_bucket(dist):
    n = jnp.maximum(dist, 0)
    max_exact = REL_BUCKETS // 2
    nf = jnp.maximum(n, 1).astype(jnp.float32)
    large = max_exact + (jnp.log(nf / max_exact) / math.log(REL_MAX_DIST / max_exact)
                         * (REL_BUCKETS - max_exact)).astype(jnp.int32)
    large = jnp.minimum(large, REL_BUCKETS - 1)
    return jnp.where(n < max_exact, n, large)


def _bias_table_kernel(rel_ref, o_ref, *, col_mult, d0_base, d0_step, lo, hi, rows, cols, out_scale):
    h = pl.program_id(0)
    k = pl.program_id(1)
    rb = pl.program_id(2)
    r = lax.broadcasted_iota(jnp.int32, (rows, cols), 0) + rb * rows
    c = lax.broadcasted_iota(jnp.int32, (rows, cols), 1)
    dist = r - col_mult * c + d0_base + d0_step * k
    bucket = _t5_bucket(dist)
    val = jnp.zeros((rows, cols), jnp.float32)
    for b in range(REL_BUCKETS):
        val = jnp.where(bucket == b, rel_ref[b, h], val)
    ok = jnp.where(dist >= lo, jnp.where(dist < hi, 1.0, 0.0), 0.0)
    o_ref[0, 0] = jnp.where(ok > 0.5, val, NEG_INF) * out_scale


def _bias_tables(rel_bias, n_k, n_rows, rows, cols, **kw):
    kern = functools.partial(_bias_table_kernel, rows=rows, cols=cols, **kw)
    return pl.pallas_call(
        kern,
        grid=(N_HEADS, n_k, n_rows // rows),
        in_specs=[pl.BlockSpec(memory_space=pltpu.SMEM)],
        out_specs=pl.BlockSpec((1, 1, rows, cols), lambda h, k, r: (h, k, r, 0)),
        out_shape=jax.ShapeDtypeStruct((N_HEADS, n_k, n_rows, cols), jnp.float32),
        compiler_params=_cparams(("parallel", "parallel", "parallel")),
        name="bias_tables",
    )(rel_bias)


def _inproj_kernel(x_ref, g_ref, w_ref, o_ref, h_scr):
    @pl.when(pl.program_id(1) == 0)
    def _():
        x = x_ref[...]
        var = jnp.mean(x * x, axis=-1, keepdims=True)
        h_scr[...] = (x * lax.rsqrt(var + EPS) * g_ref[...]).astype(jnp.bfloat16)

    o_ref[...] = jnp.dot(h_scr[...], w_ref[...], preferred_element_type=jnp.float32).astype(o_ref.dtype)


def _in_proj(x, g, w, tm, tn):
    t = x.shape[0]
    return pl.pallas_call(
        _inproj_kernel,
        grid=(t // tm, Z_WIDTH // tn),
        in_specs=[pl.BlockSpec((tm, D_MODEL), lambda i, j: (i, 0)),
                  pl.BlockSpec((1, D_MODEL), lambda i, j: (0, 0)),
                  pl.BlockSpec((D_MODEL, tn), lambda i, j: (0, j))],
        out_specs=pl.BlockSpec((tm, tn), lambda i, j: (i, j)),
        out_shape=jax.ShapeDtypeStruct((t, Z_WIDTH), jnp.bfloat16),
        scratch_shapes=[pltpu.VMEM((tm, D_MODEL), jnp.bfloat16)],
        compiler_params=_cparams(("parallel", "arbitrary")),
        name="in_proj",
    )(x, g, w)


def _compress_kernel(a_ref, pe_ref, w1_ref, w2_ref, o_ref, *, ncp):
    lo = jnp.zeros((ncp, CMP_HIDDEN), jnp.float32)
    hi = jnp.zeros((ncp, CMP_HIDDEN), jnp.float32)
    for l in range(CMP_STRIDE):
        a = a_ref[0, 0, l].astype(jnp.float32)
        a_lo = (a + pe_ref[0, l:l + 1, :]).astype(jnp.bfloat16)
        a_hi = (a + pe_ref[0, CMP_STRIDE + l:CMP_STRIDE + l + 1, :]).astype(jnp.bfloat16)
        lo = lo + jnp.dot(a_lo, w1_ref[0, l * HEAD_DIM:(l + 1) * HEAD_DIM, :],
                          preferred_element_type=jnp.float32)
        hi = hi + jnp.dot(a_hi, w1_ref[0, (CMP_STRIDE + l) * HEAD_DIM:(CMP_STRIDE + l + 1) * HEAD_DIM, :],
                          preferred_element_type=jnp.float32)
    hidden = lo + pltpu.roll(hi, ncp - 1, 0)
    act = jax.nn.gelu(hidden).astype(jnp.bfloat16)
    out = jnp.dot(act, w2_ref[0], preferred_element_type=jnp.float32)
    row = lax.broadcasted_iota(jnp.int32, (ncp, HEAD_DIM), 0)
    o_ref[0, 0] = jnp.where(row < ncp - 1, out, 0.0).astype(o_ref.dtype)


def _compress(a, pe, w1, w2):
    b, _, _, ncp, _ = a.shape
    return pl.pallas_call(
        functools.partial(_compress_kernel, ncp=ncp),
        grid=(b, 2 * KV_HEADS),
        in_specs=[pl.BlockSpec((1, 1, CMP_STRIDE, ncp, HEAD_DIM), lambda i, j: (i, j, 0, 0, 0)),
                  pl.BlockSpec((1, CMP_BLOCK, HEAD_DIM), lambda i, j: (j // KV_HEADS, 0, 0)),
                  pl.BlockSpec((1, CMP_BLOCK * HEAD_DIM, CMP_HIDDEN), lambda i, j: (j // KV_HEADS, 0, 0)),
                  pl.BlockSpec((1, CMP_HIDDEN, HEAD_DIM), lambda i, j: (j // KV_HEADS, 0, 0))],
        out_specs=pl.BlockSpec((1, 1, ncp, HEAD_DIM), lambda i, j: (i, j, 0, 0)),
        out_shape=jax.ShapeDtypeStruct((b, 2 * KV_HEADS, ncp, HEAD_DIM), jnp.bfloat16),
        compiler_params=_cparams(("parallel", "parallel")),
        name="compress",
    )(a, pe, w1, w2)


def _cmp_attn_kernel(q_ref, kc_ref, vc_ref, tab_ref, o_ref, neg_ref, *, tq, ncp, nsel):
    i = pl.program_id(2)
    kc = kc_ref[0, 0]
    vc = vc_ref[0, 0]
    q4 = jnp.concatenate([q_ref[:, h * HEAD_DIM:(h + 1) * HEAD_DIM] for h in range(HEADS_PER_GROUP)], axis=0)
    tab = jnp.concatenate([tab_ref[h, 0] for h in range(HEADS_PER_GROUP)], axis=0)
    s = lax.dot_general(q4, kc, (((1,), (1,)), ((), ())), preferred_element_type=jnp.float32) * SCALE + tab
    valid = tab > 0.5 * NEG_INF
    m = jnp.max(s, axis=-1, keepdims=True)
    e = jnp.where(valid, jnp.exp(s - m), 0.0)
    den = jnp.sum(e, axis=-1, keepdims=True)
    p = jnp.where(valid, e / den, 0.0)
    o = jnp.dot(p.astype(jnp.bfloat16), vc, preferred_element_type=jnp.float32)
    psum = p[0:tq]
    for h in range(HEADS_PER_GROUP):
        o_ref[:, h * HEAD_DIM:(h + 1) * HEAD_DIM] = o[h * tq:(h + 1) * tq].astype(o_ref.dtype)
        if h:
            psum = psum + p[h * tq:(h + 1) * tq]
    blk_n = lax.broadcasted_iota(jnp.int32, (nsel, ncp), 0)
    c_start = lax.broadcasted_iota(jnp.int32, (nsel, ncp), 1) * CMP_STRIDE
    ov = jnp.where(c_start < (blk_n + 1) * SEL_BLOCK,
                   jnp.where(c_start + CMP_BLOCK > blk_n * SEL_BLOCK, 1.0, 0.0), 0.0)
    imp = lax.dot_general(ov, psum, (((1,), (1,)), ((), ())), precision=lax.Precision.HIGHEST,
                          preferred_element_type=jnp.float32)
    blk = lax.broadcasted_iota(jnp.int32, (nsel, tq), 0)
    t = lax.broadcasted_iota(jnp.int32, (nsel, tq), 1) + i * tq
    cur = lax.shift_right_logical(t, 6)
    forced = jnp.where(blk == 0, 1.0, jnp.where(blk == cur, 1.0, jnp.where(blk == cur - 1, 1.0, 0.0)))
    causal = blk * SEL_BLOCK <= t
    score = jnp.where(forced > 0.5, FORCE_SCORE, jnp.where(causal, imp, NEG_INF))
    rank = jnp.zeros((nsel, tq), jnp.float32)
    for k in range(nsel):
        row = score[k:k + 1, :]
        rank = rank + jnp.where(blk > k, jnp.where(row >= score, 1.0, 0.0), jnp.where(row > score, 1.0, 0.0))
    n_top = min(SEL_TOPN, nsel)
    neg = jnp.where(rank < n_top, jnp.where(causal, 0.0, NEG_INF), NEG_INF)
    neg_ref[0, 0] = neg.astype(neg_ref.dtype)


def _cmp_attn(z, cmp_kv, tab, b, s):
    ncp = s // CMP_STRIDE
    nsel = s // SEL_BLOCK
    nq = s // TQ
    kern = functools.partial(_cmp_attn_kernel, tq=TQ, ncp=ncp, nsel=nsel)
    return pl.pallas_call(
        kern,
        grid=(b, KV_HEADS, nq),
        in_specs=[pl.BlockSpec((TQ, QW), lambda bi, g, i: (bi * nq + i, OFF_Q // QW + g)),
                  pl.BlockSpec((1, 1, ncp, HEAD_DIM), lambda bi, g, i: (bi, g, 0, 0)),
                  pl.BlockSpec((1, 1, ncp, HEAD_DIM), lambda bi, g, i: (bi, KV_HEADS + g, 0, 0)),
                  pl.BlockSpec((HEADS_PER_GROUP, 1, TQ, ncp), lambda bi, g, i: (g, 0, i, 0))],
        out_specs=[pl.BlockSpec((TQ, QW), lambda bi, g, i: (bi * nq + i, g)),
                   pl.BlockSpec((1, 1, nsel, TQ), lambda bi, g, i: (bi, g, 0, i))],
        out_shape=[jax.ShapeDtypeStruct((b * s, NSA_WIDTH), jnp.bfloat16),
                   jax.ShapeDtypeStruct((b, KV_HEADS, nsel, s), jnp.bfloat16)],
        compiler_params=_cparams(("parallel", "parallel", "parallel")),
        name="cmp_attn",
    )(z, cmp_kv, cmp_kv, tab)


def _stack_heads(q_ref, g, tq):
    del tq
    return jnp.concatenate(
        [q_ref[:, (g * HEADS_PER_GROUP + h) * HEAD_DIM:(g * HEADS_PER_GROUP + h + 1) * HEAD_DIM]
         for h in range(HEADS_PER_GROUP)], axis=0)


def _unstack_heads(o_ref, g, out, tq):
    for h in range(HEADS_PER_GROUP):
        c0 = (g * HEADS_PER_GROUP + h) * HEAD_DIM
        o_ref[:, c0:c0 + HEAD_DIM] = out[h * tq:(h + 1) * tq].astype(o_ref.dtype)


def _sel_kernel(q_ref, k_ref, v_ref, tab_ref, neg_ref, o_ref, m0_scr, m1_scr, acc0_scr, acc1_scr, *, tq, tk, nsel):
    i = pl.program_id(1)
    sub = tk // LANE
    q4 = [_stack_heads(q_ref, g, tq) for g in range(KV_HEADS)]
    m_scr = (m0_scr, m1_scr)
    acc_scr = (acc0_scr, acc1_scr)
    for g in range(KV_HEADS):
        m_scr[g][...] = jnp.full(m_scr[g].shape, NEG_INF, jnp.float32)
        acc_scr[g][...] = jnp.zeros(acc_scr[g].shape, jnp.float32)
    u_min = 1 - tq // tk
    n_far = tab_ref.shape[1] - 1
    ones = jnp.ones((tk, LANE), jnp.bfloat16)

    def body(j, carry):
        tidx = jnp.minimum(i * (tq // tk) - j - u_min, n_far)
        n_i = lax.broadcasted_iota(jnp.int32, (nsel, tk), 0)
        c_i = lax.broadcasted_iota(jnp.int32, (nsel, tk), 1)
        expand = jnp.where(lax.shift_right_logical(c_i + j * tk, 6) == n_i, 1.0, 0.0).astype(jnp.bfloat16)
        start = pl.multiple_of(j * tk, tk)
        for g in range(KV_HEADS):
            gs = slice(g * HEAD_DIM, (g + 1) * HEAD_DIM)
            kt = k_ref[pl.ds(start, tk), gs]
            vt = jnp.concatenate([v_ref[pl.ds(start, tk), gs], ones], axis=1)
            s = lax.dot_general(q4[g], kt, (((1,), (1,)), ((), ())), preferred_element_type=jnp.float32)
            add = jnp.dot(neg_ref[0, g], expand, preferred_element_type=jnp.float32)
            s = s * (SCALE * LOG2E) + tab_ref[g, tidx] + jnp.concatenate([add] * HEADS_PER_GROUP, axis=0)
            m_prev = m_scr[g][...]
            m_new = jnp.maximum(m_prev, jnp.max(s, axis=-1, keepdims=True))
            alpha = jnp.exp2(m_prev - m_new)
            p = jnp.exp2(s - jnp.concatenate([m_new] * sub, axis=1))
            acc_scr[g][...] = (jnp.concatenate([alpha, alpha], axis=1) * acc_scr[g][...]
                               + jnp.dot(p.astype(jnp.bfloat16), vt, preferred_element_type=jnp.float32))
            m_scr[g][...] = m_new
        return carry

    lax.fori_loop(0, (i + 1) * (tq // tk), body, 0)
    for g in range(KV_HEADS):
        _unstack_heads(o_ref, g, acc_scr[g][:, 0:HEAD_DIM] / acc_scr[g][:, HEAD_DIM:2 * HEAD_DIM], tq)


def _win_kernel(q_ref, k_ref, v_ref, tab_ref, o_ref, *, tq):
    i = pl.program_id(1)
    nb = N_PREV + 1
    sb = jnp.maximum(i - N_PREV, 0)
    off = sb - i + N_PREV
    start = pl.multiple_of(sb * WIN_QBLOCK, WIN_QBLOCK)
    ones = jnp.ones((nb * WIN_QBLOCK, LANE), jnp.bfloat16)
    for g in range(KV_HEADS):
        gs = slice(g * HEAD_DIM, (g + 1) * HEAD_DIM)
        kt = k_ref[pl.ds(start, nb * WIN_QBLOCK), gs]
        vt = jnp.concatenate([v_ref[pl.ds(start, nb * WIN_QBLOCK), gs], ones], axis=1)
        s = lax.dot_general(_stack_heads(q_ref, g, tq), kt, (((1,), (1,)), ((), ())),
                            preferred_element_type=jnp.float32)
        tab = jnp.concatenate([tab_ref[g, jnp.minimum(u + off, nb)] for u in range(nb)], axis=1)
        s = s * (SCALE * LOG2E) + tab
        p = jnp.exp2(s - jnp.max(s, axis=-1, keepdims=True))
        pv = jnp.dot(p.astype(jnp.bfloat16), vt, preferred_element_type=jnp.float32)
        _unstack_heads(o_ref, g, pv[:, 0:HEAD_DIM] / pv[:, HEAD_DIM:2 * HEAD_DIM], tq)


def _flash(z, tab, neg, b, s, *, selected):
    tq = SEL_TQ if selected else TQ
    nq = s // tq
    tk = SEL_TK if selected else WIN_QBLOCK
    k_off, v_off = (OFF_KS, OFF_VS) if selected else (OFF_KW, OFF_VW)
    rows = HEADS_PER_GROUP * tq
    n_tab = tab.shape[1]
    in_specs = [pl.BlockSpec((tq, NSA_WIDTH), lambda bi, i: (bi * nq + i, OFF_Q // NSA_WIDTH)),
                pl.BlockSpec((s, KV_WIDTH), lambda bi, i: (bi, k_off // KV_WIDTH)),
                pl.BlockSpec((s, KV_WIDTH), lambda bi, i: (bi, v_off // KV_WIDTH)),
                pl.BlockSpec((KV_HEADS, n_tab, rows, tk), lambda bi, i: (0, 0, 0, 0))]
    args = [z, z, z, tab]
    scratch = []
    if selected:
        nsel = s // SEL_BLOCK
        in_specs.append(pl.BlockSpec((1, KV_HEADS, tq, nsel), lambda bi, i: (bi, 0, i, 0)))
        args.append(neg)
        kern = functools.partial(_sel_kernel, tq=tq, tk=tk, nsel=nsel)
        scratch = ([pltpu.VMEM((rows, LANE), jnp.float32)] * KV_HEADS
                   + [pltpu.VMEM((rows, 2 * HEAD_DIM), jnp.float32)] * KV_HEADS)
    else:
        kern = functools.partial(_win_kernel, tq=tq)
    return pl.pallas_call(
        kern,
        grid=(b, nq),
        in_specs=in_specs,
        out_specs=pl.BlockSpec((tq, NSA_WIDTH), lambda bi, i: (bi * nq + i, 0)),
        out_shape=jax.ShapeDtypeStruct((b * s, NSA_WIDTH), jnp.bfloat16),
        scratch_shapes=scratch,
        compiler_params=_cparams(("parallel", "parallel")),
        name="flash_sel" if selected else "flash_win",
    )(*args)


HALO = 32
CONV_RC = 64


def _conv_kernel(a_ref, g_ref, ah_ref, gh_ref, bg_ref, cg_ref, xs_ref, cgh_ref, xsh_ref,
                 cw_ref, cb_ref, sw_ref, uo_ref, so_ref, ext_scr, ext2_scr, *, ts):
    first = pl.program_id(1) == 0
    f32 = jnp.float32
    n_ext = HALO + ts
    n_sh = n_ext - SUBLANE
    u = a_ref[...].astype(f32) * jax.nn.sigmoid(g_ref[...].astype(f32))
    uh = ah_ref[...].astype(f32) * jax.nn.sigmoid(gh_ref[...].astype(f32))
    ext_scr[0, 0:HALO, :] = jnp.where(first, 0.0, uh)
    ext_scr[0, HALO:n_ext, :] = u
    v = cg_ref[...].astype(f32) * xs_ref[...].astype(f32)
    vh = cgh_ref[...].astype(f32) * xsh_ref[...].astype(f32)
    ext2_scr[0, 0:HALO, :] = jnp.where(first, 0.0, vh)
    ext2_scr[0, HALO:n_ext, :] = v
    base = HALO - (CONF_CONV_WIDTH - 1)
    base2 = HALO - (SC_CONV_WIDTH - 1)
    for r in range(1, SUBLANE):
        ext_scr[r, 0:n_sh, :] = ext_scr[0, r:r + n_sh, :]
    sc_shifts = sorted({(base2 + k) % SUBLANE for k in range(SC_CONV_WIDTH)} - {0})
    for r in sc_shifts:
        ext2_scr[r, 0:n_sh, :] = ext2_scr[0, r:r + n_sh, :]

    def tap(scr, off, r0):
        r = off % SUBLANE
        return scr[r, r0 + off - r:r0 + off - r + CONV_RC, :]

    for r0 in range(0, ts, CONV_RC):
        acc = jnp.zeros((CONV_RC, a_ref.shape[1]), f32) + cb_ref[...]
        for k in range(CONF_CONV_WIDTH):
            acc = acc + cw_ref[k:k + 1, :] * tap(ext_scr, base + k, r0)
        uo_ref[r0:r0 + CONV_RC, :] = acc.astype(uo_ref.dtype)
        acc2 = jnp.zeros((CONV_RC, a_ref.shape[1]), f32)
        for k in range(SC_CONV_WIDTH):
            acc2 = acc2 + sw_ref[k:k + 1, :] * tap(ext2_scr, base2 + k, r0)
        so_ref[r0:r0 + CONV_RC, :] = (bg_ref[r0:r0 + CONV_RC, :].astype(f32) * acc2).astype(so_ref.dtype)


def _conv(z, cw, cb, sw, b, s, ts, tc):
    ns = s // ts
    t = b * s

    def cur(off):
        return pl.BlockSpec((ts, tc), lambda bi, i, c: (bi * ns + i, off // tc + c))

    def halo(off):
        return pl.BlockSpec((HALO, tc), lambda bi, i, c: (jnp.maximum((bi * s + i * ts) // HALO - 1, 0), off // tc + c))

    return pl.pallas_call(
        functools.partial(_conv_kernel, ts=ts),
        grid=(b, ns, CONF_WIDTH // tc),
        in_specs=[cur(OFF_CONF), cur(OFF_CONF + CONF_WIDTH), halo(OFF_CONF), halo(OFF_CONF + CONF_WIDTH),
                  cur(OFF_SC), cur(OFF_SC + SC_WIDTH), cur(OFF_SC + 2 * SC_WIDTH),
                  halo(OFF_SC + SC_WIDTH), halo(OFF_SC + 2 * SC_WIDTH),
                  pl.BlockSpec((CONF_CONV_WIDTH, tc), lambda bi, i, c: (0, c)),
                  pl.BlockSpec((1, tc), lambda bi, i, c: (0, c)),
                  pl.BlockSpec((SC_CONV_WIDTH, tc), lambda bi, i, c: (0, c))],
        out_specs=[pl.BlockSpec((ts, tc), lambda bi, i, c: (bi * ns + i, c)),
                   pl.BlockSpec((ts, tc), lambda bi, i, c: (bi * ns + i, c))],
        out_shape=[jax.ShapeDtypeStruct((t, CONF_WIDTH), jnp.bfloat16),
                   jax.ShapeDtypeStruct((t, SC_WIDTH), jnp.bfloat16)],
        scratch_shapes=[pltpu.VMEM((SUBLANE, HALO + ts, tc), jnp.float32),
                        pltpu.VMEM((SUBLANE, HALO + ts, tc), jnp.float32)],
        compiler_params=_cparams(("parallel", "parallel", "parallel")),
        name="conv",
    )(z, z, z, z, z, z, z, z, z, cw, cb, sw)


def _merge_kernel(ocmp_ref, oslc_ref, owin_ref, ng_ref, uc_ref, lng_ref, lnb_ref, osc_ref,
                  mg0_ref, mg1_ref, mg2_ref, wn_ref, wc_ref, ws_ref, o_ref, nsa_scr, conf_scr):
    f32 = jnp.float32

    @pl.when(pl.program_id(1) == 0)
    def _():
        gt = jax.nn.sigmoid(ng_ref[...].astype(f32))
        for h in range(N_HEADS):
            sl = slice(h * HEAD_DIM, (h + 1) * HEAD_DIM)
            c = N_NSA_BRANCHES * h
            o = (gt[:, c:c + 1] * ocmp_ref[:, sl].astype(f32) + gt[:, c + 1:c + 2] * oslc_ref[:, sl].astype(f32)
                 + gt[:, c + 2:c + 3] * owin_ref[:, sl].astype(f32))
            nsa_scr[:, sl] = o.astype(nsa_scr.dtype)
        u = uc_ref[...].astype(f32)
        mu = jnp.mean(u, axis=-1, keepdims=True)
        var = jnp.mean(jnp.square(u - mu), axis=-1, keepdims=True)
        y = (u - mu) * lax.rsqrt(var + EPS) * lng_ref[...] + lnb_ref[...]
        conf_scr[...] = (y * jax.nn.sigmoid(y)).astype(conf_scr.dtype)

    a = jnp.dot(nsa_scr[...], wn_ref[...], preferred_element_type=f32)
    b = jnp.dot(conf_scr[...], wc_ref[...], preferred_element_type=f32)
    c = jnp.dot(osc_ref[...], ws_ref[...], preferred_element_type=f32)
    m = (jax.nn.sigmoid(mg0_ref[...].astype(f32)) * a + jax.nn.sigmoid(mg1_ref[...].astype(f32)) * b
         + jax.nn.sigmoid(mg2_ref[...].astype(f32)) * c)
    o_ref[...] = m.astype(o_ref.dtype)


def _merge(z, ocmp, oslc, owin, uconv, lng, lnb, osc, wb, tm, tn):
    t = z.shape[0]
    nj = D_MODEL // tn
    row = lambda w: pl.BlockSpec((tm, w), lambda i, j: (i, 0))
    return pl.pallas_call(
        _merge_kernel,
        grid=(t // tm, nj),
        in_specs=[row(NSA_WIDTH), row(NSA_WIDTH), row(NSA_WIDTH),
                  pl.BlockSpec((tm, LANE), lambda i, j: (i, OFF_NSAG // LANE)),
                  row(CONF_WIDTH),
                  pl.BlockSpec((1, CONF_WIDTH), lambda i, j: (0, 0)),
                  pl.BlockSpec((1, CONF_WIDTH), lambda i, j: (0, 0)),
                  row(SC_WIDTH),
                  pl.BlockSpec((tm, tn), lambda i, j: (i, j)),
                  pl.BlockSpec((tm, tn), lambda i, j: (i, nj + j)),
                  pl.BlockSpec((tm, tn), lambda i, j: (i, 2 * nj + j)),
                  pl.BlockSpec((NSA_WIDTH, tn), lambda i, j: (0, j)),
                  pl.BlockSpec((CONF_WIDTH, tn), lambda i, j: (NSA_WIDTH // CONF_WIDTH, j)),
                  pl.BlockSpec((SC_WIDTH, tn), lambda i, j: ((NSA_WIDTH + CONF_WIDTH) // SC_WIDTH, j))],
        out_specs=pl.BlockSpec((tm, tn), lambda i, j: (i, j)),
        out_shape=jax.ShapeDtypeStruct((t, D_MODEL), jnp.bfloat16),
        scratch_shapes=[pltpu.VMEM((tm, NSA_WIDTH), jnp.bfloat16), pltpu.VMEM((tm, CONF_WIDTH), jnp.bfloat16)],
        compiler_params=_cparams(("parallel", "arbitrary")),
        name="merge",
    )(ocmp, oslc, owin, z, uconv, lng, lnb, osc, z, z, z, wb, wb, wb)


R_E1, R_E2, R_W1, R_W2, R_RANK1, R_RANK2 = range(6)
GROUP_LANE0 = N_EXPERTS


def _out_router_kernel(x_ref, m_ref, wo_ref, fn_ref, wr_ref, br_ref, x1_ref, h2_ref, route_ref, cnt_ref,
                       base_scr, *, tm):
    f32 = jnp.float32

    @pl.when(pl.program_id(0) == 0)
    def _():
        base_scr[...] = jnp.zeros_like(base_scr)

    x1 = x_ref[...] + jnp.dot(m_ref[...], wo_ref[...], preferred_element_type=f32)
    x1_ref[...] = x1
    var = jnp.mean(x1 * x1, axis=-1, keepdims=True)
    h2 = x1 * lax.rsqrt(var + EPS) * fn_ref[...]
    h2_ref[...] = h2
    h_hi = h2.astype(jnp.bfloat16)
    h_lo = (h2 - h_hi.astype(f32)).astype(jnp.bfloat16)
    l_hi = jnp.dot(h_hi, wr_ref[...], preferred_element_type=f32)
    l_lo = jnp.dot(h_lo, wr_ref[:, 0:ROUTE_LANES], preferred_element_type=f32)
    logits = l_hi[:, 0:ROUTE_LANES] + l_hi[:, ROUTE_LANES:2 * ROUTE_LANES] + l_lo + br_ref[...]
    lane = lax.broadcasted_iota(jnp.int32, (tm, ROUTE_LANES), 1).astype(f32)
    big = float(ROUTE_LANES)
    is_g = jnp.where(lane >= GROUP_LANE0, jnp.where(lane < GROUP_LANE0 + N_GROUPS, 1.0, 0.0), 0.0) > 0.5
    gl = jnp.where(is_g, logits, NEG_INF)
    gmax = jnp.max(gl, axis=-1, keepdims=True)
    glane = jnp.min(jnp.where(gl == gmax, lane, big), axis=-1, keepdims=True)
    gsum = jnp.sum(jnp.where(is_g, jnp.exp(gl - gmax), 0.0), axis=-1, keepdims=True)
    g_w = 1.0 / gsum
    grp = glane - GROUP_LANE0
    in_grp = jnp.floor(lane * (1.0 / EXPERTS_PER_GROUP)) == grp
    el = jnp.where(in_grp, logits, NEG_INF)
    emax = jnp.max(el, axis=-1, keepdims=True)
    ee = jnp.where(in_grp, jnp.exp(el - emax), 0.0)
    ep = ee / jnp.sum(ee, axis=-1, keepdims=True)
    ep = jnp.where(in_grp, ep, -1.0)
    p1 = jnp.max(ep, axis=-1, keepdims=True)
    i1 = jnp.min(jnp.where(ep == p1, lane, big), axis=-1, keepdims=True)
    ep2 = jnp.where(lane == i1, -1.0, ep)
    p2 = jnp.max(ep2, axis=-1, keepdims=True)
    i2 = jnp.min(jnp.where(ep2 == p2, lane, big), axis=-1, keepdims=True)
    psum = p1 + p2
    w1 = g_w * (p1 / psum)
    w2 = g_w * (p2 / psum)
    onehot = jnp.where(lane == i1, 1.0, jnp.where(lane == i2, 1.0, 0.0))
    r_i = lax.broadcasted_iota(jnp.int32, (tm, tm), 0)
    c_i = lax.broadcasted_iota(jnp.int32, (tm, tm), 1)
    tri = jnp.where(c_i < r_i, 1.0, 0.0).astype(jnp.bfloat16)
    cum = jnp.dot(tri, onehot.astype(jnp.bfloat16), preferred_element_type=f32) + base_scr[0:1, :]
    rank1 = jnp.sum(jnp.where(lane == i1, cum, 0.0), axis=-1, keepdims=True)
    rank2 = jnp.sum(jnp.where(lane == i2, cum, 0.0), axis=-1, keepdims=True)
    new_base = base_scr[0:1, :] + jnp.sum(onehot, axis=0, keepdims=True)
    base_scr[...] = jnp.broadcast_to(new_base, base_scr.shape)
    cnt_ref[...] = jnp.broadcast_to(new_base, cnt_ref.shape)
    rec = jnp.zeros((tm, ROUTE_LANES), f32)
    for ln, val in ((R_E1, i1), (R_E2, i2), (R_W1, w1), (R_W2, w2), (R_RANK1, rank1), (R_RANK2, rank2)):
        rec = jnp.where(lane == ln, val, rec)
    route_ref[...] = rec


def _out_router(x, merged, wo, fn, wr, br, tm):
    t = x.shape[0]
    n = t // tm
    return pl.pallas_call(
        functools.partial(_out_router_kernel, tm=tm),
        grid=(n,),
        in_specs=[pl.BlockSpec((tm, D_MODEL), lambda i: (i, 0)),
                  pl.BlockSpec((tm, D_MODEL), lambda i: (i, 0)),
                  pl.BlockSpec((D_MODEL, D_MODEL), lambda i: (0, 0)),
                  pl.BlockSpec((1, D_MODEL), lambda i: (0, 0)),
                  pl.BlockSpec((D_MODEL, 2 * ROUTE_LANES), lambda i: (0, 0)),
                  pl.BlockSpec((1, ROUTE_LANES), lambda i: (0, 0))],
        out_specs=[pl.BlockSpec((tm, D_MODEL), lambda i: (i, 0)),
                   pl.BlockSpec((tm, D_MODEL), lambda i: (i, 0)),
                   pl.BlockSpec((tm, ROUTE_LANES), lambda i: (i, 0)),
                   pl.BlockSpec((8, ROUTE_LANES), lambda i: (i, 0))],
        out_shape=[jax.ShapeDtypeStruct((t, D_MODEL), jnp.float32),
                   jax.ShapeDtypeStruct((t, D_MODEL), jnp.float32),
                   jax.ShapeDtypeStruct((t, ROUTE_LANES), jnp.float32),
                   jax.ShapeDtypeStruct((n * 8, ROUTE_LANES), jnp.float32)],
        scratch_shapes=[pltpu.VMEM((8, ROUTE_LANES), jnp.float32)],
        compiler_params=_cparams(("arbitrary",)),
        name="out_router",
    )(x, merged, wo, fn, wr, br)


def _dispatch_kernel(zs_ref, dest_ref, h_ref, xs_ref, zero_scr, sem, zsem, *, tmd):
    @pl.when(pl.program_id(0) == 0)
    def _():
        zero_scr[...] = jnp.zeros_like(zero_scr)
        for e in range(N_EXPERTS):
            @pl.when(zs_ref[e] >= 0)
            def _():
                cp = pltpu.make_async_copy(zero_scr, xs_ref.at[pl.ds(pl.multiple_of(zs_ref[e], TMX), TMX)], zsem)
                cp.start()
                cp.wait()

        def zero_tail(tile, c):
            cp = pltpu.make_async_copy(zero_scr, xs_ref.at[pl.ds(pl.multiple_of(tile * TMX, TMX), TMX)], zsem)
            cp.start()
            cp.wait()
            return c

        lax.fori_loop(zs_ref[N_EXPERTS] // TMX, xs_ref.shape[0] // TMX, zero_tail, 0)

    def row_copy(r, k):
        return pltpu.make_async_copy(h_ref.at[pl.ds(r, 1)], xs_ref.at[pl.ds(dest_ref[0, 0, 2 * r + k], 1)], sem)

    def issue(r, c):
        row_copy(r, 0).start()
        row_copy(r, 1).start()
        return c

    lax.fori_loop(0, tmd, issue, 0)

    def drain(r, c):
        row_copy(r, 0).wait()
        row_copy(r, 1).wait()
        return c

    lax.fori_loop(0, tmd, drain, 0)


def _dispatch(zstart, dest3, h2, p_rows, tmd):
    t = h2.shape[0]
    grid_spec = pltpu.PrefetchScalarGridSpec(
        num_scalar_prefetch=1,
        grid=(t // tmd,),
        in_specs=[pl.BlockSpec((1, 1, 2 * tmd), lambda i, zs: (i, 0, 0), memory_space=pltpu.SMEM),
                  pl.BlockSpec((tmd, D_MODEL), lambda i, zs: (i, 0))],
        out_specs=pl.BlockSpec(memory_space=pl.ANY),
        scratch_shapes=[pltpu.VMEM((TMX, D_MODEL), jnp.float32), pltpu.SemaphoreType.DMA(()),
                        pltpu.SemaphoreType.DMA(())],
    )
    return pl.pallas_call(
        functools.partial(_dispatch_kernel, tmd=tmd),
        grid_spec=grid_spec,
        out_shape=jax.ShapeDtypeStruct((p_rows, D_MODEL), jnp.float32),
        compiler_params=_cparams(("arbitrary",)),
        name="dispatch",
    )(zstart, dest3, h2)


def _expert_kernel(te_ref, tv_ref, tb_ref, xs_ref, wg_ref, wu_ref, wd_ref, y_ref, wgu_scr, wd_scr):
    i = pl.program_id(0)
    e = te_ref[i]
    prev = te_ref[jnp.maximum(i - 1, 0)]

    @pl.when(jnp.logical_or(i == 0, e != prev))
    def _():
        wgu_scr[:, 0:D_EXPERT] = wg_ref[0, 0].astype(jnp.bfloat16)
        wgu_scr[:, D_EXPERT:2 * D_EXPERT] = wu_ref[0, 0].astype(jnp.bfloat16)
        wd_scr[...] = wd_ref[0, 0].astype(jnp.bfloat16)

    @pl.when(tv_ref[i] == 1)
    def _():
        x = xs_ref[...].astype(jnp.bfloat16)
        gu = jnp.dot(x, wgu_scr[...], preferred_element_type=jnp.float32)
        gate = gu[:, 0:D_EXPERT]
        he = (gate * jax.nn.sigmoid(gate)) * gu[:, D_EXPERT:2 * D_EXPERT]
        y_ref[...] = jnp.dot(he.astype(jnp.bfloat16), wd_scr[...], preferred_element_type=jnp.float32)

    @pl.when(tv_ref[i] == 0)
    def _():
        y_ref[...] = jnp.zeros_like(y_ref)


def _experts(tile_e, tile_v, tile_b, xs, wg, wu, wd, layer):
    p_rows = xs.shape[0]
    grid_spec = pltpu.PrefetchScalarGridSpec(
        num_scalar_prefetch=3,
        grid=(p_rows // TMX,),
        in_specs=[pl.BlockSpec((TMX, D_MODEL), lambda i, te, tv, tb: (tb[i], 0)),
                  pl.BlockSpec((1, 1, D_MODEL, D_EXPERT), lambda i, te, tv, tb: (layer, te[i], 0, 0)),
                  pl.BlockSpec((1, 1, D_MODEL, D_EXPERT), lambda i, te, tv, tb: (layer, te[i], 0, 0)),
                  pl.BlockSpec((1, 1, D_EXPERT, D_MODEL), lambda i, te, tv, tb: (layer, te[i], 0, 0))],
        out_specs=pl.BlockSpec((TMX, D_MODEL), lambda i, te, tv, tb: (i, 0)),
        scratch_shapes=[pltpu.VMEM((D_MODEL, 2 * D_EXPERT), jnp.bfloat16),
                        pltpu.VMEM((D_EXPERT, D_MODEL), jnp.bfloat16)],
    )
    return pl.pallas_call(
        _expert_kernel,
        grid_spec=grid_spec,
        out_shape=jax.ShapeDtypeStruct((p_rows, D_MODEL), jnp.float32),
        compiler_params=_cparams(("arbitrary",)),
        name="experts",
    )(tile_e, tile_v, tile_b, xs, wg, wu, wd)


def _combine_ple_kernel(dest_ref, x1_ref, route_ref, y_ref, p_ref, pn_ref, wpg_ref, wpp_ref, fn_ref, o_ref,
                        buf, sem, *, tmc, final):
    f32 = jnp.float32

    def row_copy(r, k):
        return pltpu.make_async_copy(y_ref.at[pl.ds(dest_ref[0, 0, 2 * r + k], 1)], buf.at[k, pl.ds(r, 1)], sem)

    def issue(r, c):
        row_copy(r, 0).start()
        row_copy(r, 1).start()
        return c

    lax.fori_loop(0, tmc, issue, 0)

    def drain(r, c):
        row_copy(r, 0).wait()
        row_copy(r, 1).wait()
        return c

    lax.fori_loop(0, tmc, drain, 0)

    route = route_ref[...]
    x2 = x1_ref[...] + route[:, R_W1:R_W1 + 1] * buf[0] + route[:, R_W2:R_W2 + 1] * buf[1]
    var = jnp.mean(x2 * x2, axis=-1, keepdims=True)
    hp = (x2 * lax.rsqrt(var + EPS) * pn_ref[...]).astype(jnp.bfloat16)
    gate = jax.nn.sigmoid(jnp.dot(hp, wpg_ref[...], preferred_element_type=f32))
    pp = jnp.dot(p_ref[...].astype(jnp.bfloat16), wpp_ref[...], preferred_element_type=f32)
    x3 = x2 + gate * pp
    if final:
        var3 = jnp.mean(x3 * x3, axis=-1, keepdims=True)
        x3 = x3 * lax.rsqrt(var3 + EPS) * fn_ref[...]
    o_ref[...] = x3


def _combine_ple(dest3, x1, route, y, p, pn, wpg, wpp, fn, tmc, final):
    t = x1.shape[0]
    return pl.pallas_call(
        functools.partial(_combine_ple_kernel, tmc=tmc, final=final),
        grid=(t // tmc,),
        in_specs=[pl.BlockSpec((1, 1, 2 * tmc), lambda i: (i, 0, 0), memory_space=pltpu.SMEM),
                  pl.BlockSpec((tmc, D_MODEL), lambda i: (i, 0)),
                  pl.BlockSpec((tmc, ROUTE_LANES), lambda i: (i, 0)),
                  pl.BlockSpec(memory_space=pl.ANY),
                  pl.BlockSpec((tmc, PLE_DIM), lambda i: (i, 0)),
                  pl.BlockSpec((1, D_MODEL), lambda i: (0, 0)),
                  pl.BlockSpec((D_MODEL, D_MODEL), lambda i: (0, 0)),
                  pl.BlockSpec((PLE_DIM, D_MODEL), lambda i: (0, 0)),
                  pl.BlockSpec((1, D_MODEL), lambda i: (0, 0))],
        out_specs=pl.BlockSpec((tmc, D_MODEL), lambda i: (i, 0)),
        out_shape=jax.ShapeDtypeStruct((t, D_MODEL), jnp.float32),
        scratch_shapes=[pltpu.VMEM((2, tmc, D_MODEL), jnp.float32), pltpu.SemaphoreType.DMA(())],
        compiler_params=_cparams(("arbitrary",)),
        name="combine_ple",
    )(dest3, x1, route, y, p, pn, wpg, wpp, fn)


def _regroup_w_in(w):
    o_q, o_kv, o_ng = 0, NSA_WIDTH, NSA_WIDTH + 6 * KV_WIDTH
    o_conf = o_ng + N_HEADS * N_NSA_BRANCHES
    o_sc = o_conf + 2 * CONF_WIDTH
    o_mg = o_sc + 3 * SC_WIDTH
    pad = jnp.zeros((w.shape[0], Z_WIDTH - OFF_NSAG - N_HEADS * N_NSA_BRANCHES), w.dtype)
    return jnp.concatenate([w[:, o_mg:], w[:, o_sc:o_mg], w[:, o_conf:o_sc], w[:, o_q:o_ng],
                            w[:, o_ng:o_conf], pad], axis=1).astype(jnp.bfloat16)


def _route_plan(route, counts, t, tmd):
    eid = route[:, R_E1:R_E2 + 1].astype(jnp.int32)
    rank = route[:, R_RANK1:R_RANK2 + 1].astype(jnp.int32)
    cnt = counts[:N_EXPERTS].astype(jnp.int32)
    padded = ((cnt + TMX - 1) // TMX) * TMX
    ends = jnp.cumsum(padded)
    starts = ends - padded
    dest = jnp.take(starts, eid) + rank
    p_rows = 2 * t + N_EXPERTS * TMX
    n_tiles = p_rows // TMX
    tile_start = jnp.arange(n_tiles, dtype=jnp.int32) * TMX
    tile_e = jnp.minimum(jnp.sum((tile_start[:, None] >= ends[None, :]).astype(jnp.int32), axis=1), N_EXPERTS - 1)
    tile_v = (tile_start < ends[-1]).astype(jnp.int32)
    tile_b = jnp.where(tile_v == 1, jnp.arange(n_tiles, dtype=jnp.int32), 0)
    zstart = jnp.concatenate([jnp.where(padded > cnt, ends - TMX, -1), ends[-1:]]).astype(jnp.int32)
    return dest.reshape(t // tmd, 1, 2 * tmd), tile_e, tile_v, tile_b, zstart, p_rows


def kernel(x, p, rel_bias, attn_norm, w_in, cmp_pe, cmp_w1, cmp_w2, conf_conv_w, conf_conv_b, conf_ln_g, conf_ln_b, sc_conv_w, w_branch, w_out, ffn_norm, router_group_w, router_group_b, router_expert_w, router_expert_b, expert_w_gate, expert_w_up, expert_w_down, ple_norm, ple_gate_w, ple_proj_w, final_norm):
    b, s, d = x.shape
    t = b * s
    depth = w_in.shape[0]
    bf16 = jnp.bfloat16
    ncp = s // CMP_STRIDE
    rows4 = HEADS_PER_GROUP * TQ

    cmp_tab = _bias_tables(rel_bias, 1, s, TQ, ncp, col_mult=CMP_STRIDE, d0_base=-(CMP_BLOCK - 1), d0_step=0,
                           lo=0, hi=1 << 30, out_scale=1.0)
    toe = _bias_tables(rel_bias, 6, TQ, TQ, LANE, col_mult=1, d0_base=-LANE, d0_step=LANE,
                       lo=0, hi=WINDOW, out_scale=LOG2E)
    toe = toe.reshape(KV_HEADS, HEADS_PER_GROUP, 6, TQ, LANE).transpose(0, 2, 1, 3, 4).reshape(KV_HEADS, 6, rows4, LANE)
    win_tab = jnp.stack([toe[:, N_PREV - jj + 1] for jj in range(N_PREV + 2)], axis=1)
    n_sel_tab = SEL_TQ // SEL_TK + 2
    sel_tab = _bias_tables(rel_bias, n_sel_tab, SEL_TQ, SEL_TQ, SEL_TK, col_mult=1,
                           d0_base=(1 - SEL_TQ // SEL_TK) * SEL_TK, d0_step=SEL_TK, lo=0, hi=1 << 30, out_scale=LOG2E)
    sel_tab = sel_tab.reshape(KV_HEADS, HEADS_PER_GROUP, n_sel_tab, SEL_TQ, SEL_TK).transpose(0, 2, 1, 3, 4)
    sel_tab = sel_tab.reshape(KV_HEADS, n_sel_tab, HEADS_PER_GROUP * SEL_TQ, SEL_TK)

    x2d = x.reshape(t, d)
    tm_in = min(1024, t)
    tm = min(512, t)
    tmd = min(256, t)
    for i in range(depth):
        w_z = _regroup_w_in(w_in[i])
        wr = jnp.zeros((d, ROUTE_LANES), jnp.float32)
        wr = wr.at[:, 0:N_EXPERTS].set(router_expert_w[i]).at[:, GROUP_LANE0:GROUP_LANE0 + N_GROUPS].set(router_group_w[i])
        wr_hi = wr.astype(bf16)
        wr = jnp.concatenate([wr_hi, (wr - wr_hi.astype(jnp.float32)).astype(bf16)], axis=1)
        br = jnp.zeros((1, ROUTE_LANES), jnp.float32)
        br = br.at[0, 0:N_EXPERTS].set(router_expert_b[i]).at[0, GROUP_LANE0:GROUP_LANE0 + N_GROUPS].set(router_group_b[i])

        z = _in_proj(x2d, attn_norm[i].reshape(1, d), w_z, tm_in, 1024)
        kcv = z[:, OFF_KC:OFF_KC + 2 * KV_WIDTH].reshape(b, ncp, CMP_STRIDE, 2 * KV_HEADS, HEAD_DIM)
        kcv = kcv.transpose(0, 3, 2, 1, 4)
        cmp_kv = _compress(kcv, cmp_pe[i], cmp_w1[i].astype(bf16), cmp_w2[i].astype(bf16))
        o_cmp, neg_t = _cmp_attn(z, cmp_kv, cmp_tab, b, s)
        neg = jnp.swapaxes(neg_t, 2, 3)
        o_slc = _flash(z, sel_tab, neg, b, s, selected=True)
        o_win = _flash(z, win_tab, None, b, s, selected=False)
        uconv, o_sc = _conv(z, conf_conv_w[i], conf_conv_b[i].reshape(1, -1), sc_conv_w[i], b, s, min(512, s), 256)
        merged = _merge(z, o_cmp, o_slc, o_win, uconv, conf_ln_g[i].reshape(1, -1), conf_ln_b[i].reshape(1, -1),
                        o_sc, w_branch[i].astype(bf16), tm, 1024)

        x1, h2, route, cnts = _out_router(x2d, merged, w_out[i].astype(bf16), ffn_norm[i].reshape(1, d), wr, br, tm)
        dest3, tile_e, tile_v, tile_b, zstart, p_rows = _route_plan(route, cnts[-1], t, tmd)
        xs = _dispatch(zstart, dest3, h2, p_rows, tmd)
        y = _experts(tile_e, tile_v, tile_b, xs, expert_w_gate, expert_w_up, expert_w_down, i)

        x2d = _combine_ple(dest3, x1, route, y, p[i].reshape(t, PLE_DIM), ple_norm[i].reshape(1, d),
                           ple_gate_w[i].astype(bf16), ple_proj_w[i].astype(bf16), final_norm.reshape(1, d),
                           tmd, i == depth - 1)
    return x2d.reshape(b, s, d)
```

```python
import functools
import math

import jax
import jax.numpy as jnp
from jax import lax
from jax.experimental import pallas as pl
from jax.experimental.pallas import tpu as pltpu

D_MODEL = 2048
N_HEADS = 8
HEAD_DIM = 128
KV_HEADS = 2
HEADS_PER_GROUP = N_HEADS // KV_HEADS
NSA_WIDTH = N_HEADS * HEAD_DIM
KV_WIDTH = KV_HEADS * HEAD_DIM
N_NSA_BRANCHES = 3
CMP_BLOCK = 32
CMP_STRIDE = 16
CMP_HIDDEN = 512
SEL_BLOCK = 64
SEL_TOPN = 16
WINDOW = 512
WIN_QBLOCK = 128
CONF_WIDTH = 1024
CONF_CONV_WIDTH = 31
SC_WIDTH = 1024
SC_CONV_WIDTH = 3
REL_BUCKETS = 32
REL_MAX_DIST = 128
N_GROUPS = 4
EXPERTS_PER_GROUP = 8
N_EXPERTS = N_GROUPS * EXPERTS_PER_GROUP
D_EXPERT = 512
PLE_DIM = 256
EPS = 1e-6
NEG_INF = -1e30
FORCE_SCORE = 1e9
SCALE = HEAD_DIM ** -0.5
LOG2E = math.log2(math.e)

LANE = 128
SUBLANE = 8
VMEM_LIMIT = 56 * 1024 * 1024

OFF_MERGE = 0
OFF_SC = OFF_MERGE + 3 * D_MODEL
OFF_CONF = OFF_SC + 3 * SC_WIDTH
OFF_Q = OFF_CONF + 2 * CONF_WIDTH
OFF_KC = OFF_Q + NSA_WIDTH
OFF_VC = OFF_KC + KV_WIDTH
OFF_KS = OFF_VC + KV_WIDTH
OFF_VS = OFF_KS + KV_WIDTH
OFF_KW = OFF_VS + KV_WIDTH
OFF_VW = OFF_KW + KV_WIDTH
OFF_NSAG = OFF_VW + KV_WIDTH
Z_WIDTH = OFF_NSAG + 512

QW = HEADS_PER_GROUP * HEAD_DIM
TQ = 128
SEL_TQ = 512
SEL_TK = 256
N_PREV = WINDOW // WIN_QBLOCK
ROUTE_LANES = LANE
TMX = 256


def _cparams(sem, vmem=VMEM_LIMIT):
    return pltpu.CompilerParams(dimension_semantics=sem, vmem_limit_bytes=vmem)


def _t5_bucket(dist):
    n = jnp.maximum(dist, 0)
    max_exact = REL_BUCKETS // 2
    nf = jnp.maximum(n, 1).astype(jnp.float32)
    large = max_exact + (jnp.log(nf / max_exact) / math.log(REL_MAX_DIST / max_exact)
                         * (REL_BUCKETS - max_exact)).astype(jnp.int32)
    large = jnp.minimum(large, REL_BUCKETS - 1)
    return jnp.where(n < max_exact, n, large)


def _bias_table_kernel(rel_ref, o_ref, *, col_mult, d0_base, d0_step, lo, hi, rows, cols, out_scale):
    h = pl.program_id(0)
    k = pl.program_id(1)
    rb = pl.program_id(2)
    cb = pl.program_id(3)
    d0 = d0_base + d0_step * k
    d_min = rb * rows - col_mult * (cb * cols + cols - 1) + d0
    d_max = rb * rows + rows - 1 - col_mult * cb * cols + d0
    masked = jnp.logical_or(d_max < lo, d_min >= hi)
    far = jnp.logical_and(d_min >= REL_MAX_DIST, d_max < hi)

    @pl.when(masked)
    def _():
        o_ref[0, 0] = jnp.full((rows, cols), NEG_INF * out_scale, jnp.float32)

    @pl.when(far)
    def _():
        o_ref[0, 0] = jnp.full((rows, cols), rel_ref[REL_BUCKETS - 1, h] * out_scale, jnp.float32)

    @pl.when(jnp.logical_not(jnp.logical_or(masked, far)))
    def _():
        r = lax.broadcasted_iota(jnp.int32, (rows, cols), 0) + rb * rows
        c = lax.broadcasted_iota(jnp.int32, (rows, cols), 1) + cb * cols
        dist = r - col_mult * c + d0
        bucket = _t5_bucket(dist)
        val = jnp.zeros((rows, cols), jnp.float32)
        for b in range(REL_BUCKETS):
            val = jnp.where(bucket == b, rel_ref[b, h], val)
        ok = jnp.where(dist >= lo, jnp.where(dist < hi, 1.0, 0.0), 0.0)
        o_ref[0, 0] = jnp.where(ok > 0.5, val, NEG_INF) * out_scale


def _bias_tables(rel_bias, n_k, n_rows, rows, n_cols, **kw):
    cols = min(LANE, n_cols)
    kern = functools.partial(_bias_table_kernel, rows=rows, cols=cols, **kw)
    return pl.pallas_call(
        kern,
        grid=(N_HEADS, n_k, n_rows // rows, n_cols // cols),
        in_specs=[pl.BlockSpec(memory_space=pltpu.SMEM)],
        out_specs=pl.BlockSpec((1, 1, rows, cols), lambda h, k, r, c: (h, k, r, c)),
        out_shape=jax.ShapeDtypeStruct((N_HEADS, n_k, n_rows, n_cols), jnp.float32),
        compiler_params=_cparams(("parallel", "parallel", "parallel", "parallel")),
        name="bias_tables",
    )(rel_bias)


def _inproj_kernel(x_ref, g_ref, w_ref, o_ref, h_scr):
    @pl.when(pl.program_id(1) == 0)
    def _():
        x = x_ref[...]
        var = jnp.mean(x * x, axis=-1, keepdims=True)
        h_scr[...] = (x * lax.rsqrt(var + EPS) * g_ref[...]).astype(jnp.bfloat16)

    o_ref[...] = jnp.dot(h_scr[...], w_ref[...], preferred_element_type=jnp.float32).astype(o_ref.dtype)


def _in_proj(x, g, w, tm, tn):
    t = x.shape[0]
    return pl.pallas_call(
        _inproj_kernel,
        grid=(t // tm, Z_WIDTH // tn),
        in_specs=[pl.BlockSpec((tm, D_MODEL), lambda i, j: (i, 0)),
                  pl.BlockSpec((1, D_MODEL), lambda i, j: (0, 0)),
                  pl.BlockSpec((D_MODEL, tn), lambda i, j: (0, j))],
        out_specs=pl.BlockSpec((tm, tn), lambda i, j: (i, j)),
        out_shape=jax.ShapeDtypeStruct((t, Z_WIDTH), jnp.bfloat16),
        scratch_shapes=[pltpu.VMEM((tm, D_MODEL), jnp.bfloat16)],
        compiler_params=_cparams(("parallel", "arbitrary")),
        name="in_proj",
    )(x, g, w)


def _compress_kernel(a_ref, pe_ref, w1_ref, w2_ref, o_ref, *, ncp):
    lo = jnp.zeros((ncp, CMP_HIDDEN), jnp.float32)
    hi = jnp.zeros((ncp, CMP_HIDDEN), jnp.float32)
    for l in range(CMP_STRIDE):
        a = a_ref[0, 0, l].astype(jnp.float32)
        a_lo = (a + pe_ref[0, l:l + 1, :]).astype(jnp.bfloat16)
        a_hi = (a + pe_ref[0, CMP_STRIDE + l:CMP_STRIDE + l + 1, :]).astype(jnp.bfloat16)
        lo = lo + jnp.dot(a_lo, w1_ref[0, l * HEAD_DIM:(l + 1) * HEAD_DIM, :],
                          preferred_element_type=jnp.float32)
        hi = hi + jnp.dot(a_hi, w1_ref[0, (CMP_STRIDE + l) * HEAD_DIM:(CMP_STRIDE + l + 1) * HEAD_DIM, :],
                          preferred_element_type=jnp.float32)
    hidden = lo + pltpu.roll(hi, ncp - 1, 0)
    act = jax.nn.gelu(hidden).astype(jnp.bfloat16)
    out = jnp.dot(act, w2_ref[0], preferred_element_type=jnp.float32)
    row = lax.broadcasted_iota(jnp.int32, (ncp, HEAD_DIM), 0)
    o_ref[0, 0] = jnp.where(row < ncp - 1, out, 0.0).astype(o_ref.dtype)


def _compress(a, pe, w1, w2):
    b, _, _, ncp, _ = a.shape
    return pl.pallas_call(
        functools.partial(_compress_kernel, ncp=ncp),
        grid=(b, 2 * KV_HEADS),
        in_specs=[pl.BlockSpec((1, 1, CMP_STRIDE, ncp, HEAD_DIM), lambda i, j: (i, j, 0, 0, 0)),
                  pl.BlockSpec((1, CMP_BLOCK, HEAD_DIM), lambda i, j: (j // KV_HEADS, 0, 0)),
                  pl.BlockSpec((1, CMP_BLOCK * HEAD_DIM, CMP_HIDDEN), lambda i, j: (j // KV_HEADS, 0, 0)),
                  pl.BlockSpec((1, CMP_HIDDEN, HEAD_DIM), lambda i, j: (j // KV_HEADS, 0, 0))],
        out_specs=pl.BlockSpec((1, 1, ncp, HEAD_DIM), lambda i, j: (i, j, 0, 0)),
        out_shape=jax.ShapeDtypeStruct((b, 2 * KV_HEADS, ncp, HEAD_DIM), jnp.bfloat16),
        compiler_params=_cparams(("parallel", "parallel")),
        name="compress",
    )(a, pe, w1, w2)


def _cmp_attn_kernel(q_ref, kc_ref, vc_ref, tab_ref, o_ref, neg_ref, *, tq, ncp, nsel):
    i = pl.program_id(2)
    kc = kc_ref[0, 0]
    vc = vc_ref[0, 0]
    q4 = jnp.concatenate([q_ref[:, h * HEAD_DIM:(h + 1) * HEAD_DIM] for h in range(HEADS_PER_GROUP)], axis=0)
    tab = jnp.concatenate([tab_ref[h, 0] for h in range(HEADS_PER_GROUP)], axis=0)
    s = lax.dot_general(q4, kc, (((1,), (1,)), ((), ())), preferred_element_type=jnp.float32) * SCALE + tab
    valid = tab > 0.5 * NEG_INF
    m = jnp.max(s, axis=-1, keepdims=True)
    e = jnp.where(valid, jnp.exp(s - m), 0.0)
    den = jnp.sum(e, axis=-1, keepdims=True)
    p = jnp.where(valid, e / den, 0.0)
    o = jnp.dot(p.astype(jnp.bfloat16), vc, preferred_element_type=jnp.float32)
    psum = p[0:tq]
    for h in range(HEADS_PER_GROUP):
        o_ref[:, h * HEAD_DIM:(h + 1) * HEAD_DIM] = o[h * tq:(h + 1) * tq].astype(o_ref.dtype)
        if h:
            psum = psum + p[h * tq:(h + 1) * tq]
    blk_n = lax.broadcasted_iota(jnp.int32, (nsel, ncp), 0)
    c_start = lax.broadcasted_iota(jnp.int32, (nsel, ncp), 1) * CMP_STRIDE
    ov = jnp.where(c_start < (blk_n + 1) * SEL_BLOCK,
                   jnp.where(c_start + CMP_BLOCK > blk_n * SEL_BLOCK, 1.0, 0.0), 0.0)
    imp = lax.dot_general(ov, psum, (((1,), (1,)), ((), ())), precision=lax.Precision.HIGHEST,
                          preferred_element_type=jnp.float32)
    blk = lax.broadcasted_iota(jnp.int32, (nsel, tq), 0)
    t = lax.broadcasted_iota(jnp.int32, (nsel, tq), 1) + i * tq
    cur = lax.shift_right_logical(t, 6)
    forced = jnp.where(blk == 0, 1.0, jnp.where(blk == cur, 1.0, jnp.where(blk == cur - 1, 1.0, 0.0)))
    causal = blk * SEL_BLOCK <= t
    score = jnp.where(forced > 0.5, FORCE_SCORE, jnp.where(causal, imp, NEG_INF))
    rank = jnp.zeros((nsel, tq), jnp.float32)
    for k in range(nsel):
        row = score[k:k + 1, :]
        rank = rank + jnp.where(blk > k, jnp.where(row >= score, 1.0, 0.0), jnp.where(row > score, 1.0, 0.0))
    n_top = min(SEL_TOPN, nsel)
    neg = jnp.where(rank < n_top, jnp.where(causal, 0.0, NEG_INF), NEG_INF)
    neg_ref[0, 0] = neg.astype(neg_ref.dtype)


def _cmp_attn(z, cmp_kv, tab, b, s):
    ncp = s // CMP_STRIDE
    nsel = s // SEL_BLOCK
    nq = s // TQ
    kern = functools.partial(_cmp_attn_kernel, tq=TQ, ncp=ncp, nsel=nsel)
    return pl.pallas_call(
        kern,
        grid=(b, KV_HEADS, nq),
        in_specs=[pl.BlockSpec((TQ, QW), lambda bi, g, i: (bi * nq + i, OFF_Q // QW + g)),
                  pl.BlockSpec((1, 1, ncp, HEAD_DIM), lambda bi, g, i: (bi, g, 0, 0)),
                  pl.BlockSpec((1, 1, ncp, HEAD_DIM), lambda bi, g, i: (bi, KV_HEADS + g, 0, 0)),
                  pl.BlockSpec((HEADS_PER_GROUP, 1, TQ, ncp), lambda bi, g, i: (g, 0, i, 0))],
        out_specs=[pl.BlockSpec((TQ, QW), lambda bi, g, i: (bi * nq + i, g)),
                   pl.BlockSpec((1, 1, nsel, TQ), lambda bi, g, i: (bi, g, 0, i))],
        out_shape=[jax.ShapeDtypeStruct((b * s, NSA_WIDTH), jnp.bfloat16),
                   jax.ShapeDtypeStruct((b, KV_HEADS, nsel, s), jnp.bfloat16)],
        compiler_params=_cparams(("parallel", "parallel", "parallel")),
        name="cmp_attn",
    )(z, cmp_kv, cmp_kv, tab)


def _stack_heads(q_ref, g, tq):
    del tq
    return jnp.concatenate(
        [q_ref[:, (g * HEADS_PER_GROUP + h) * HEAD_DIM:(g * HEADS_PER_GROUP + h + 1) * HEAD_DIM]
         for h in range(HEADS_PER_GROUP)], axis=0)


def _unstack_heads(o_ref, g, out, tq):
    for h in range(HEADS_PER_GROUP):
        c0 = (g * HEADS_PER_GROUP + h) * HEAD_DIM
        o_ref[:, c0:c0 + HEAD_DIM] = out[h * tq:(h + 1) * tq].astype(o_ref.dtype)


def _sel_kernel(q_ref, k_ref, v_ref, tab_ref, neg_ref, o_ref, m0_scr, m1_scr, acc0_scr, acc1_scr, *, tq, tk, nsel):
    i = pl.program_id(1)
    sub = tk // LANE
    q4 = [_stack_heads(q_ref, g, tq) for g in range(KV_HEADS)]
    m_scr = (m0_scr, m1_scr)
    acc_scr = (acc0_scr, acc1_scr)
    for g in range(KV_HEADS):
        m_scr[g][...] = jnp.full(m_scr[g].shape, NEG_INF, jnp.float32)
        acc_scr[g][...] = jnp.zeros(acc_scr[g].shape, jnp.float32)
    u_min = 1 - tq // tk
    n_far = tab_ref.shape[1] - 1
    ones = jnp.ones((tk, LANE), jnp.bfloat16)

    def body(j, carry):
        tidx = jnp.minimum(i * (tq // tk) - j - u_min, n_far)
        n_i = lax.broadcasted_iota(jnp.int32, (nsel, tk), 0)
        c_i = lax.broadcasted_iota(jnp.int32, (nsel, tk), 1)
        expand = jnp.where(lax.shift_right_logical(c_i + j * tk, 6) == n_i, 1.0, 0.0).astype(jnp.bfloat16)
        start = pl.multiple_of(j * tk, tk)
        for g in range(KV_HEADS):
            gs = slice(g * HEAD_DIM, (g + 1) * HEAD_DIM)
            kt = k_ref[pl.ds(start, tk), gs]
            vt = jnp.concatenate([v_ref[pl.ds(start, tk), gs], ones], axis=1)
            s = lax.dot_general(q4[g], kt, (((1,), (1,)), ((), ())), preferred_element_type=jnp.float32)
            add = jnp.dot(neg_ref[0, g], expand, preferred_element_type=jnp.float32)
            s = s * (SCALE * LOG2E) + tab_ref[g, tidx] + jnp.concatenate([add] * HEADS_PER_GROUP, axis=0)
            m_prev = m_scr[g][...]
            m_new = jnp.maximum(m_prev, jnp.max(s, axis=-1, keepdims=True))
            alpha = jnp.exp2(m_prev - m_new)
            p = jnp.exp2(s - jnp.concatenate([m_new] * sub, axis=1))
            acc_scr[g][...] = (jnp.concatenate([alpha, alpha], axis=1) * acc_scr[g][...]
                               + jnp.dot(p.astype(jnp.bfloat16), vt, preferred_element_type=jnp.float32))
            m_scr[g][...] = m_new
        return carry

    lax.fori_loop(0, (i + 1) * (tq // tk), body, 0)
    for g in range(KV_HEADS):
        _unstack_heads(o_ref, g, acc_scr[g][:, 0:HEAD_DIM] / acc_scr[g][:, HEAD_DIM:2 * HEAD_DIM], tq)


def _win_kernel(q_ref, k_ref, v_ref, tab_ref, o_ref, *, tq):
    i = pl.program_id(1)
    nb = N_PREV + 1
    sb = jnp.maximum(i - N_PREV, 0)
    off = sb - i + N_PREV
    start = pl.multiple_of(sb * WIN_QBLOCK, WIN_QBLOCK)
    ones = jnp.ones((nb * WIN_QBLOCK, LANE), jnp.bfloat16)
    for g in range(KV_HEADS):
        gs = slice(g * HEAD_DIM, (g + 1) * HEAD_DIM)
        kt = k_ref[pl.ds(start, nb * WIN_QBLOCK), gs]
        vt = jnp.concatenate([v_ref[pl.ds(start, nb * WIN_QBLOCK), gs], ones], axis=1)
        s = lax.dot_general(_stack_heads(q_ref, g, tq), kt, (((1,), (1,)), ((), ())),
                            preferred_element_type=jnp.float32)
        tab = jnp.concatenate([tab_ref[g, jnp.minimum(u + off, nb)] for u in range(nb)], axis=1)
        s = s * (SCALE * LOG2E) + tab
        p = jnp.exp2(s - jnp.max(s, axis=-1, keepdims=True))
        pv = jnp.dot(p.astype(jnp.bfloat16), vt, preferred_element_type=jnp.float32)
        _unstack_heads(o_ref, g, pv[:, 0:HEAD_DIM] / pv[:, HEAD_DIM:2 * HEAD_DIM], tq)


def _flash(z, tab, neg, b, s, *, selected):
    tq = SEL_TQ if selected else TQ
    nq = s // tq
    tk = SEL_TK if selected else WIN_QBLOCK
    k_off, v_off = (OFF_KS, OFF_VS) if selected else (OFF_KW, OFF_VW)
    rows = HEADS_PER_GROUP * tq
    n_tab = tab.shape[1]
    in_specs = [pl.BlockSpec((tq, NSA_WIDTH), lambda bi, i: (bi * nq + i, OFF_Q // NSA_WIDTH)),
                pl.BlockSpec((s, KV_WIDTH), lambda bi, i: (bi, k_off // KV_WIDTH)),
                pl.BlockSpec((s, KV_WIDTH), lambda bi, i: (bi, v_off // KV_WIDTH)),
                pl.BlockSpec((KV_HEADS, n_tab, rows, tk), lambda bi, i: (0, 0, 0, 0),
                             pipeline_mode=pl.Buffered(1))]
    args = [z, z, z, tab]
    scratch = []
    if selected:
        nsel = s // SEL_BLOCK
        in_specs.append(pl.BlockSpec((1, KV_HEADS, tq, nsel), lambda bi, i: (bi, 0, i, 0)))
        args.append(neg)
        kern = functools.partial(_sel_kernel, tq=tq, tk=tk, nsel=nsel)
        scratch = ([pltpu.VMEM((rows, LANE), jnp.float32)] * KV_HEADS
                   + [pltpu.VMEM((rows, 2 * HEAD_DIM), jnp.float32)] * KV_HEADS)
    else:
        kern = functools.partial(_win_kernel, tq=tq)
    return pl.pallas_call(
        kern,
        grid=(b, nq),
        in_specs=in_specs,
        out_specs=pl.BlockSpec((tq, NSA_WIDTH), lambda bi, i: (bi * nq + i, 0)),
        out_shape=jax.ShapeDtypeStruct((b * s, NSA_WIDTH), jnp.bfloat16),
        scratch_shapes=scratch,
        compiler_params=_cparams(("parallel", "parallel")),
        name="flash_sel" if selected else "flash_win",
    )(*args)


HALO = 32
CONV_RC = 64


def _conv_kernel(a_ref, g_ref, ah_ref, gh_ref, bg_ref, cg_ref, xs_ref, cgh_ref, xsh_ref,
                 cw_ref, cb_ref, sw_ref, uo_ref, so_ref, ext_scr, ext2_scr, *, ts):
    first = pl.program_id(1) == 0
    f32 = jnp.float32
    n_ext = HALO + ts
    n_sh = n_ext - SUBLANE
    u = a_ref[...].astype(f32) * jax.nn.sigmoid(g_ref[...].astype(f32))
    uh = ah_ref[...].astype(f32) * jax.nn.sigmoid(gh_ref[...].astype(f32))
    ext_scr[0, 0:HALO, :] = jnp.where(first, 0.0, uh)
    ext_scr[0, HALO:n_ext, :] = u
    v = cg_ref[...].astype(f32) * xs_ref[...].astype(f32)
    vh = cgh_ref[...].astype(f32) * xsh_ref[...].astype(f32)
    ext2_scr[0, 0:HALO, :] = jnp.where(first, 0.0, vh)
    ext2_scr[0, HALO:n_ext, :] = v
    base = HALO - (CONF_CONV_WIDTH - 1)
    base2 = HALO - (SC_CONV_WIDTH - 1)
    for r in range(1, SUBLANE):
        ext_scr[r, 0:n_sh, :] = ext_scr[0, r:r + n_sh, :]
    sc_shifts = sorted({(base2 + k) % SUBLANE for k in range(SC_CONV_WIDTH)} - {0})
    for r in sc_shifts:
        ext2_scr[r, 0:n_sh, :] = ext2_scr[0, r:r + n_sh, :]

    def tap(scr, off, r0):
        r = off % SUBLANE
        return scr[r, r0 + off - r:r0 + off - r + CONV_RC, :]

    for r0 in range(0, ts, CONV_RC):
        acc = jnp.zeros((CONV_RC, a_ref.shape[1]), f32) + cb_ref[...]
        for k in range(CONF_CONV_WIDTH):
            acc = acc + cw_ref[k:k + 1, :] * tap(ext_scr, base + k, r0)
        uo_ref[r0:r0 + CONV_RC, :] = acc.astype(uo_ref.dtype)
        acc2 = jnp.zeros((CONV_RC, a_ref.shape[1]), f32)
        for k in range(SC_CONV_WIDTH):
            acc2 = acc2 + sw_ref[k:k + 1, :] * tap(ext2_scr, base2 + k, r0)
        so_ref[r0:r0 + CONV_RC, :] = (bg_ref[r0:r0 + CONV_RC, :].astype(f32) * acc2).astype(so_ref.dtype)


def _conv(z, cw, cb, sw, b, s, ts, tc):
    ns = s // ts
    t = b * s

    def cur(off):
        return pl.BlockSpec((ts, tc), lambda bi, i, c: (bi * ns + i, off // tc + c))

    def halo(off):
        return pl.BlockSpec((HALO, tc), lambda bi, i, c: (jnp.maximum((bi * s + i * ts) // HALO - 1, 0), off // tc + c))

    return pl.pallas_call(
        functools.partial(_conv_kernel, ts=ts),
        grid=(b, ns, CONF_WIDTH // tc),
        in_specs=[cur(OFF_CONF), cur(OFF_CONF + CONF_WIDTH), halo(OFF_CONF), halo(OFF_CONF + CONF_WIDTH),
                  cur(OFF_SC), cur(OFF_SC + SC_WIDTH), cur(OFF_SC + 2 * SC_WIDTH),
                  halo(OFF_SC + SC_WIDTH), halo(OFF_SC + 2 * SC_WIDTH),
                  pl.BlockSpec((CONF_CONV_WIDTH, tc), lambda bi, i, c: (0, c)),
                  pl.BlockSpec((1, tc), lambda bi, i, c: (0, c)),
                  pl.BlockSpec((SC_CONV_WIDTH, tc), lambda bi, i, c: (0, c))],
        out_specs=[pl.BlockSpec((ts, tc), lambda bi, i, c: (bi * ns + i, c)),
                   pl.BlockSpec((ts, tc), lambda bi, i, c: (bi * ns + i, c))],
        out_shape=[jax.ShapeDtypeStruct((t, CONF_WIDTH), jnp.bfloat16),
                   jax.ShapeDtypeStruct((t, SC_WIDTH), jnp.bfloat16)],
        scratch_shapes=[pltpu.VMEM((SUBLANE, HALO + ts, tc), jnp.float32),
                        pltpu.VMEM((SUBLANE, HALO + ts, tc), jnp.float32)],
        compiler_params=_cparams(("parallel", "parallel", "parallel")),
        name="conv",
    )(z, z, z, z, z, z, z, z, z, cw, cb, sw)


def _merge_kernel(ocmp_ref, oslc_ref, owin_ref, ng_ref, uc_ref, lng_ref, lnb_ref, osc_ref,
                  mg0_ref, mg1_ref, mg2_ref, wn_ref, wc_ref, ws_ref, o_ref, nsa_scr, conf_scr):
    f32 = jnp.float32

    @pl.when(pl.program_id(1) == 0)
    def _():
        gt = jax.nn.sigmoid(ng_ref[...].astype(f32))
        for h in range(N_HEADS):
            sl = slice(h * HEAD_DIM, (h + 1) * HEAD_DIM)
            c = N_NSA_BRANCHES * h
            o = (gt[:, c:c + 1] * ocmp_ref[:, sl].astype(f32) + gt[:, c + 1:c + 2] * oslc_ref[:, sl].astype(f32)
                 + gt[:, c + 2:c + 3] * owin_ref[:, sl].astype(f32))
            nsa_scr[:, sl] = o.astype(nsa_scr.dtype)
        u = uc_ref[...].astype(f32)
        mu = jnp.mean(u, axis=-1, keepdims=True)
        var = jnp.mean(jnp.square(u - mu), axis=-1, keepdims=True)
        y = (u - mu) * lax.rsqrt(var + EPS) * lng_ref[...] + lnb_ref[...]
        conf_scr[...] = (y * jax.nn.sigmoid(y)).astype(conf_scr.dtype)

    a = jnp.dot(nsa_scr[...], wn_ref[...], preferred_element_type=f32)
    b = jnp.dot(conf_scr[...], wc_ref[...], preferred_element_type=f32)
    c = jnp.dot(osc_ref[...], ws_ref[...], preferred_element_type=f32)
    m = (jax.nn.sigmoid(mg0_ref[...].astype(f32)) * a + jax.nn.sigmoid(mg1_ref[...].astype(f32)) * b
         + jax.nn.sigmoid(mg2_ref[...].astype(f32)) * c)
    o_ref[...] = m.astype(o_ref.dtype)


def _merge(z, ocmp, oslc, owin, uconv, lng, lnb, osc, wb, tm, tn):
    t = z.shape[0]
    nj = D_MODEL // tn
    row = lambda w: pl.BlockSpec((tm, w), lambda i, j: (i, 0))
    return pl.pallas_call(
        _merge_kernel,
        grid=(t // tm, nj),
        in_specs=[row(NSA_WIDTH), row(NSA_WIDTH), row(NSA_WIDTH),
                  pl.BlockSpec((tm, LANE), lambda i, j: (i, OFF_NSAG // LANE)),
                  row(CONF_WIDTH),
                  pl.BlockSpec((1, CONF_WIDTH), lambda i, j: (0, 0)),
                  pl.BlockSpec((1, CONF_WIDTH), lambda i, j: (0, 0)),
                  row(SC_WIDTH),
                  pl.BlockSpec((tm, tn), lambda i, j: (i, j)),
                  pl.BlockSpec((tm, tn), lambda i, j: (i, nj + j)),
                  pl.BlockSpec((tm, tn), lambda i, j: (i, 2 * nj + j)),
                  pl.BlockSpec((NSA_WIDTH, tn), lambda i, j: (0, j)),
                  pl.BlockSpec((CONF_WIDTH, tn), lambda i, j: (NSA_WIDTH // CONF_WIDTH, j)),
                  pl.BlockSpec((SC_WIDTH, tn), lambda i, j: ((NSA_WIDTH + CONF_WIDTH) // SC_WIDTH, j))],
        out_specs=pl.BlockSpec((tm, tn), lambda i, j: (i, j)),
        out_shape=jax.ShapeDtypeStruct((t, D_MODEL), jnp.bfloat16),
        scratch_shapes=[pltpu.VMEM((tm, NSA_WIDTH), jnp.bfloat16), pltpu.VMEM((tm, CONF_WIDTH), jnp.bfloat16)],
        compiler_params=_cparams(("parallel", "arbitrary")),
        name="merge",
    )(ocmp, oslc, owin, z, uconv, lng, lnb, osc, z, z, z, wb, wb, wb)


R_E1, R_E2, R_W1, R_W2, R_RANK1, R_RANK2 = range(6)
GROUP_LANE0 = N_EXPERTS


def _out_router_kernel(x_ref, m_ref, wo_ref, fn_ref, wr_ref, br_ref, x1_ref, h2_ref, route_ref, cnt_ref,
                       base_scr, *, tm):
    f32 = jnp.float32

    @pl.when(pl.program_id(0) == 0)
    def _():
        base_scr[...] = jnp.zeros_like(base_scr)

    x1 = x_ref[...] + jnp.dot(m_ref[...], wo_ref[...], preferred_element_type=f32)
    x1_ref[...] = x1
    var = jnp.mean(x1 * x1, axis=-1, keepdims=True)
    h2 = x1 * lax.rsqrt(var + EPS) * fn_ref[...]
    h2_ref[...] = _pack_bf16_pairs(h2)
    h_hi = h2.astype(jnp.bfloat16)
    h_lo = (h2 - h_hi.astype(f32)).astype(jnp.bfloat16)
    l_hi = jnp.dot(h_hi, wr_ref[...], preferred_element_type=f32)
    l_lo = jnp.dot(h_lo, wr_ref[:, 0:ROUTE_LANES], preferred_element_type=f32)
    logits = l_hi[:, 0:ROUTE_LANES] + l_hi[:, ROUTE_LANES:2 * ROUTE_LANES] + l_lo + br_ref[...]
    lane = lax.broadcasted_iota(jnp.int32, (tm, ROUTE_LANES), 1).astype(f32)
    big = float(ROUTE_LANES)
    is_g = jnp.where(lane >= GROUP_LANE0, jnp.where(lane < GROUP_LANE0 + N_GROUPS, 1.0, 0.0), 0.0) > 0.5
    gl = jnp.where(is_g, logits, NEG_INF)
    gmax = jnp.max(gl, axis=-1, keepdims=True)
    glane = jnp.min(jnp.where(gl == gmax, lane, big), axis=-1, keepdims=True)
    gsum = jnp.sum(jnp.where(is_g, jnp.exp(gl - gmax), 0.0), axis=-1, keepdims=True)
    g_w = 1.0 / gsum
    grp = glane - GROUP_LANE0
    in_grp = jnp.floor(lane * (1.0 / EXPERTS_PER_GROUP)) == grp
    el = jnp.where(in_grp, logits, NEG_INF)
    emax = jnp.max(el, axis=-1, keepdims=True)
    ee = jnp.where(in_grp, jnp.exp(el - emax), 0.0)
    ep = ee / jnp.sum(ee, axis=-1, keepdims=True)
    ep = jnp.where(in_grp, ep, -1.0)
    p1 = jnp.max(ep, axis=-1, keepdims=True)
    i1 = jnp.min(jnp.where(ep == p1, lane, big), axis=-1, keepdims=True)
    ep2 = jnp.where(lane == i1, -1.0, ep)
    p2 = jnp.max(ep2, axis=-1, keepdims=True)
    i2 = jnp.min(jnp.where(ep2 == p2, lane, big), axis=-1, keepdims=True)
    psum = p1 + p2
    w1 = g_w * (p1 / psum)
    w2 = g_w * (p2 / psum)
    onehot = jnp.where(lane == i1, 1.0, jnp.where(lane == i2, 1.0, 0.0))
    r_i = lax.broadcasted_iota(jnp.int32, (tm, tm), 0)
    c_i = lax.broadcasted_iota(jnp.int32, (tm, tm), 1)
    tri = jnp.where(c_i < r_i, 1.0, 0.0).astype(jnp.bfloat16)
    cum = jnp.dot(tri, onehot.astype(jnp.bfloat16), preferred_element_type=f32) + base_scr[0:1, :]
    rank1 = jnp.sum(jnp.where(lane == i1, cum, 0.0), axis=-1, keepdims=True)
    rank2 = jnp.sum(jnp.where(lane == i2, cum, 0.0), axis=-1, keepdims=True)
    new_base = base_scr[0:1, :] + jnp.sum(onehot, axis=0, keepdims=True)
    base_scr[...] = jnp.broadcast_to(new_base, base_scr.shape)
    cnt_ref[...] = jnp.broadcast_to(new_base, cnt_ref.shape)
    rec = jnp.zeros((tm, ROUTE_LANES), f32)
    for ln, val in ((R_E1, i1), (R_E2, i2), (R_W1, w1), (R_W2, w2), (R_RANK1, rank1), (R_RANK2, rank2)):
        rec = jnp.where(lane == ln, val, rec)
    route_ref[...] = rec


def _out_router(x, merged, wo, fn, wr, br, tm):
    t = x.shape[0]
    n = t // tm
    return pl.pallas_call(
        functools.partial(_out_router_kernel, tm=tm),
        grid=(n,),
        in_specs=[pl.BlockSpec((tm, D_MODEL), lambda i: (i, 0)),
                  pl.BlockSpec((tm, D_MODEL), lambda i: (i, 0)),
                  pl.BlockSpec((D_MODEL, D_MODEL), lambda i: (0, 0)),
                  pl.BlockSpec((1, D_MODEL), lambda i: (0, 0)),
                  pl.BlockSpec((D_MODEL, 2 * ROUTE_LANES), lambda i: (0, 0)),
                  pl.BlockSpec((1, ROUTE_LANES), lambda i: (0, 0))],
        out_specs=[pl.BlockSpec((tm, D_MODEL), lambda i: (i, 0)),
                   pl.BlockSpec((tm, D_MODEL // 2), lambda i: (i, 0)),
                   pl.BlockSpec((tm, ROUTE_LANES), lambda i: (i, 0)),
                   pl.BlockSpec((8, ROUTE_LANES), lambda i: (i, 0))],
        out_shape=[jax.ShapeDtypeStruct((t, D_MODEL), jnp.float32),
                   jax.ShapeDtypeStruct((t, D_MODEL // 2), jnp.uint32),
                   jax.ShapeDtypeStruct((t, ROUTE_LANES), jnp.float32),
                   jax.ShapeDtypeStruct((n * 8, ROUTE_LANES), jnp.float32)],
        scratch_shapes=[pltpu.VMEM((8, ROUTE_LANES), jnp.float32)],
        compiler_params=_cparams(("arbitrary",)),
        name="out_router",
    )(x, merged, wo, fn, wr, br)


HALF = D_MODEL // 2
DMA_UNROLL = 8


def _pack_bf16_pairs(x):
    lo = pltpu.bitcast(x[:, 0:HALF].astype(jnp.bfloat16).astype(jnp.float32), jnp.uint32)
    hi = pltpu.bitcast(x[:, HALF:D_MODEL].astype(jnp.bfloat16).astype(jnp.float32), jnp.uint32)
    return hi | (lo >> 16)


def _unpack_bf16_pairs(w):
    lo = pltpu.bitcast(w << 16, jnp.float32)
    hi = pltpu.bitcast(w & jnp.uint32(0xFFFF0000), jnp.float32)
    return lo, hi


def _dispatch_kernel(zs_ref, dest_ref, h_ref, xs_ref, zero_scr, sem, zsem, *, tmd):
    @pl.when(pl.program_id(0) == 0)
    def _():
        zero_scr[...] = jnp.zeros_like(zero_scr)
        for e in range(N_EXPERTS):
            @pl.when(zs_ref[e] >= 0)
            def _():
                cp = pltpu.make_async_copy(zero_scr, xs_ref.at[pl.ds(pl.multiple_of(zs_ref[e], TMX), TMX)], zsem)
                cp.start()
                cp.wait()

        def zero_tail(tile, c):
            cp = pltpu.make_async_copy(zero_scr, xs_ref.at[pl.ds(pl.multiple_of(tile * TMX, TMX), TMX)], zsem)
            cp.start()
            cp.wait()
            return c

        lax.fori_loop(zs_ref[N_EXPERTS] // TMX, xs_ref.shape[0] // TMX, zero_tail, 0)

    def row_copy(r, k):
        return pltpu.make_async_copy(h_ref.at[pl.ds(r, 1)], xs_ref.at[pl.ds(dest_ref[0, 0, 2 * r + k], 1)], sem)

    def issue(r, c):
        row_copy(r, 0).start(priority=0)
        row_copy(r, 1).start(priority=1)
        return c

    lax.fori_loop(0, tmd, issue, 0, unroll=DMA_UNROLL)

    def drain(r, c):
        row_copy(r, 0).wait()
        row_copy(r, 1).wait()
        return c

    lax.fori_loop(0, tmd, drain, 0, unroll=DMA_UNROLL)


def _dispatch(zstart, dest3, h2p, p_rows, tmd):
    t = h2p.shape[0]
    grid_spec = pltpu.PrefetchScalarGridSpec(
        num_scalar_prefetch=1,
        grid=(t // tmd,),
        in_specs=[pl.BlockSpec((1, 1, 2 * tmd), lambda i, zs: (i, 0, 0), memory_space=pltpu.SMEM),
                  pl.BlockSpec((tmd, HALF), lambda i, zs: (i, 0))],
        out_specs=pl.BlockSpec(memory_space=pl.ANY),
        scratch_shapes=[pltpu.VMEM((TMX, HALF), jnp.uint32), pltpu.SemaphoreType.DMA(()),
                        pltpu.SemaphoreType.DMA(())],
    )
    return pl.pallas_call(
        functools.partial(_dispatch_kernel, tmd=tmd),
        grid_spec=grid_spec,
        out_shape=jax.ShapeDtypeStruct((p_rows, HALF), jnp.uint32),
        compiler_params=_cparams(("arbitrary",)),
        name="dispatch",
    )(zstart, dest3, h2p)


def _expert_kernel(te_ref, tv_ref, tb_ref, xs_ref, wg_ref, wu_ref, wd_ref, y_ref, wgu_scr, wd_scr):
    i = pl.program_id(0)
    e = te_ref[i]
    prev = te_ref[jnp.maximum(i - 1, 0)]

    @pl.when(jnp.logical_or(i == 0, e != prev))
    def _():
        wgu_scr[:, 0:D_EXPERT] = wg_ref[0, 0].astype(jnp.bfloat16)
        wgu_scr[:, D_EXPERT:2 * D_EXPERT] = wu_ref[0, 0].astype(jnp.bfloat16)
        wd_scr[...] = wd_ref[0, 0].astype(jnp.bfloat16)

    @pl.when(tv_ref[i] == 1)
    def _():
        x_lo, x_hi = _unpack_bf16_pairs(xs_ref[...])
        gu = (jnp.dot(x_lo.astype(jnp.bfloat16), wgu_scr[0:HALF, :], preferred_element_type=jnp.float32)
              + jnp.dot(x_hi.astype(jnp.bfloat16), wgu_scr[HALF:D_MODEL, :], preferred_element_type=jnp.float32))
        gate = gu[:, 0:D_EXPERT]
        he = (gate * jax.nn.sigmoid(gate)) * gu[:, D_EXPERT:2 * D_EXPERT]
        y = jnp.dot(he.astype(jnp.bfloat16), wd_scr[...], preferred_element_type=jnp.float32)
        y_ref[...] = _pack_bf16_pairs(y)

    @pl.when(tv_ref[i] == 0)
    def _():
        y_ref[...] = jnp.zeros_like(y_ref)


def _experts(tile_e, tile_v, tile_b, xs, wg, wu, wd, layer):
    p_rows = xs.shape[0]
    grid_spec = pltpu.PrefetchScalarGridSpec(
        num_scalar_prefetch=3,
        grid=(p_rows // TMX,),
        in_specs=[pl.BlockSpec((TMX, HALF), lambda i, te, tv, tb: (tb[i], 0)),
                  pl.BlockSpec((1, 1, D_MODEL, D_EXPERT), lambda i, te, tv, tb: (layer, te[i], 0, 0)),
                  pl.BlockSpec((1, 1, D_MODEL, D_EXPERT), lambda i, te, tv, tb: (layer, te[i], 0, 0)),
                  pl.BlockSpec((1, 1, D_EXPERT, D_MODEL), lambda i, te, tv, tb: (layer, te[i], 0, 0))],
        out_specs=pl.BlockSpec((TMX, HALF), lambda i, te, tv, tb: (i, 0)),
        scratch_shapes=[pltpu.VMEM((D_MODEL, 2 * D_EXPERT), jnp.bfloat16),
                        pltpu.VMEM((D_EXPERT, D_MODEL), jnp.bfloat16)],
    )
    return pl.pallas_call(
        _expert_kernel,
        grid_spec=grid_spec,
        out_shape=jax.ShapeDtypeStruct((p_rows, HALF), jnp.uint32),
        compiler_params=_cparams(("arbitrary",)),
        name="experts",
    )(tile_e, tile_v, tile_b, xs, wg, wu, wd)


def _combine_ple_kernel(dcur_ref, dnext_ref, x1_ref, route_ref, y_ref, p_ref, pn_ref, wpg_ref, wpp_ref, fn_ref,
                        o_ref, buf, sems, *, tmc, final):
    f32 = jnp.float32
    i = pl.program_id(0)
    slot = lax.rem(i, 2)

    def row_copy(d_ref, s, r, k):
        return pltpu.make_async_copy(y_ref.at[pl.ds(d_ref[0, 0, 2 * r + k], 1)], buf.at[s, k, pl.ds(r, 1)],
                                     sems.at[s])

    def issue(d_ref, s):
        def body(r, c):
            row_copy(d_ref, s, r, 0).start(priority=0)
            row_copy(d_ref, s, r, 1).start(priority=1)
            return c
        lax.fori_loop(0, tmc, body, 0, unroll=DMA_UNROLL)

    @pl.when(i == 0)
    def _():
        issue(dcur_ref, 0)

    @pl.when(i + 1 < pl.num_programs(0))
    def _():
        issue(dnext_ref, 1 - slot)

    def drain(r, c):
        row_copy(dcur_ref, slot, r, 0).wait()
        row_copy(dcur_ref, slot, r, 1).wait()
        return c

    lax.fori_loop(0, tmc, drain, 0, unroll=DMA_UNROLL)

    route = route_ref[...]
    w1 = route[:, R_W1:R_W1 + 1]
    w2 = route[:, R_W2:R_W2 + 1]
    y1_lo, y1_hi = _unpack_bf16_pairs(buf[slot, 0])
    y2_lo, y2_hi = _unpack_bf16_pairs(buf[slot, 1])
    x2 = x1_ref[...] + jnp.concatenate([w1 * y1_lo + w2 * y2_lo, w1 * y1_hi + w2 * y2_hi], axis=1)
    var = jnp.mean(x2 * x2, axis=-1, keepdims=True)
    hp = (x2 * lax.rsqrt(var + EPS) * pn_ref[...]).astype(jnp.bfloat16)
    gate = jax.nn.sigmoid(jnp.dot(hp, wpg_ref[...], preferred_element_type=f32))
    pp = jnp.dot(p_ref[...].astype(jnp.bfloat16), wpp_ref[...], preferred_element_type=f32)
    x3 = x2 + gate * pp
    if final:
        var3 = jnp.mean(x3 * x3, axis=-1, keepdims=True)
        x3 = x3 * lax.rsqrt(var3 + EPS) * fn_ref[...]
    o_ref[...] = x3


def _combine_ple(dest3, x1, route, y, p, pn, wpg, wpp, fn, tmc, final):
    t = x1.shape[0]
    n = t // tmc
    return pl.pallas_call(
        functools.partial(_combine_ple_kernel, tmc=tmc, final=final),
        grid=(n,),
        in_specs=[pl.BlockSpec((1, 1, 2 * tmc), lambda i: (i, 0, 0), memory_space=pltpu.SMEM),
                  pl.BlockSpec((1, 1, 2 * tmc), lambda i: (jnp.minimum(i + 1, n - 1), 0, 0), memory_space=pltpu.SMEM),
                  pl.BlockSpec((tmc, D_MODEL), lambda i: (i, 0)),
                  pl.BlockSpec((tmc, ROUTE_LANES), lambda i: (i, 0)),
                  pl.BlockSpec(memory_space=pl.ANY),
                  pl.BlockSpec((tmc, PLE_DIM), lambda i: (i, 0)),
                  pl.BlockSpec((1, D_MODEL), lambda i: (0, 0)),
                  pl.BlockSpec((D_MODEL, D_MODEL), lambda i: (0, 0)),
                  pl.BlockSpec((PLE_DIM, D_MODEL), lambda i: (0, 0)),
                  pl.BlockSpec((1, D_MODEL), lambda i: (0, 0))],
        out_specs=pl.BlockSpec((tmc, D_MODEL), lambda i: (i, 0)),
        out_shape=jax.ShapeDtypeStruct((t, D_MODEL), jnp.float32),
        scratch_shapes=[pltpu.VMEM((2, 2, tmc, HALF), jnp.uint32), pltpu.SemaphoreType.DMA((2,))],
        compiler_params=_cparams(("arbitrary",)),
        name="combine_ple",
    )(dest3, dest3, x1, route, y, p, pn, wpg, wpp, fn)


def _regroup_w_in(w):
    o_q, o_kv, o_ng = 0, NSA_WIDTH, NSA_WIDTH + 6 * KV_WIDTH
    o_conf = o_ng + N_HEADS * N_NSA_BRANCHES
    o_sc = o_conf + 2 * CONF_WIDTH
    o_mg = o_sc + 3 * SC_WIDTH
    pad = jnp.zeros((w.shape[0], Z_WIDTH - OFF_NSAG - N_HEADS * N_NSA_BRANCHES), w.dtype)
    return jnp.concatenate([w[:, o_mg:], w[:, o_sc:o_mg], w[:, o_conf:o_sc], w[:, o_q:o_ng],
                            w[:, o_ng:o_conf], pad], axis=1).astype(jnp.bfloat16)


def _route_plan(route, counts, t, tmd):
    eid = route[:, R_E1:R_E2 + 1].astype(jnp.int32)
    rank = route[:, R_RANK1:R_RANK2 + 1].astype(jnp.int32)
    cnt = counts[:N_EXPERTS].astype(jnp.int32)
    padded = ((cnt + TMX - 1) // TMX) * TMX
    ends = jnp.cumsum(padded)
    starts = ends - padded
    dest = jnp.take(starts, eid) + rank
    p_rows = 2 * t + N_EXPERTS * TMX
    n_tiles = p_rows // TMX
    tile_start = jnp.arange(n_tiles, dtype=jnp.int32) * TMX
    tile_e = jnp.minimum(jnp.sum((tile_start[:, None] >= ends[None, :]).astype(jnp.int32), axis=1), N_EXPERTS - 1)
    tile_v = (tile_start < ends[-1]).astype(jnp.int32)
    tile_b = jnp.where(tile_v == 1, jnp.arange(n_tiles, dtype=jnp.int32), 0)
    zstart = jnp.concatenate([jnp.where(padded > cnt, ends - TMX, -1), ends[-1:]]).astype(jnp.int32)
    return dest.reshape(t // tmd, 1, 2 * tmd), tile_e, tile_v, tile_b, zstart, p_rows


def kernel(x, p, rel_bias, attn_norm, w_in, cmp_pe, cmp_w1, cmp_w2, conf_conv_w, conf_conv_b, conf_ln_g, conf_ln_b, sc_conv_w, w_branch, w_out, ffn_norm, router_group_w, router_group_b, router_expert_w, router_expert_b, expert_w_gate, expert_w_up, expert_w_down, ple_norm, ple_gate_w, ple_proj_w, final_norm):
    b, s, d = x.shape
    t = b * s
    depth = w_in.shape[0]
    bf16 = jnp.bfloat16
    ncp = s // CMP_STRIDE
    rows4 = HEADS_PER_GROUP * TQ

    cmp_tab = _bias_tables(rel_bias, 1, s, TQ, ncp, col_mult=CMP_STRIDE, d0_base=-(CMP_BLOCK - 1), d0_step=0,
                           lo=0, hi=1 << 30, out_scale=1.0)
    toe = _bias_tables(rel_bias, 6, TQ, TQ, LANE, col_mult=1, d0_base=-LANE, d0_step=LANE,
                       lo=0, hi=WINDOW, out_scale=LOG2E)
    toe = toe.reshape(KV_HEADS, HEADS_PER_GROUP, 6, TQ, LANE).transpose(0, 2, 1, 3, 4).reshape(KV_HEADS, 6, rows4, LANE)
    win_tab = jnp.stack([toe[:, N_PREV - jj + 1] for jj in range(N_PREV + 2)], axis=1)
    n_sel_tab = SEL_TQ // SEL_TK + 2
    sel_tab = _bias_tables(rel_bias, n_sel_tab, SEL_TQ, SEL_TQ, SEL_TK, col_mult=1,
                           d0_base=(1 - SEL_TQ // SEL_TK) * SEL_TK, d0_step=SEL_TK, lo=0, hi=1 << 30, out_scale=LOG2E)
    sel_tab = sel_tab.reshape(KV_HEADS, HEADS_PER_GROUP, n_sel_tab, SEL_TQ, SEL_TK).transpose(0, 2, 1, 3, 4)
    sel_tab = sel_tab.reshape(KV_HEADS, n_sel_tab, HEADS_PER_GROUP * SEL_TQ, SEL_TK)

    x2d = x.reshape(t, d)
    tm_in = min(1024, t)
    tm = min(512, t)
    tmd = min(256, t)
    for i in range(depth):
        w_z = _regroup_w_in(w_in[i])
        wr = jnp.zeros((d, ROUTE_LANES), jnp.float32)
        wr = wr.at[:, 0:N_EXPERTS].set(router_expert_w[i]).at[:, GROUP_LANE0:GROUP_LANE0 + N_GROUPS].set(router_group_w[i])
        wr_hi = wr.astype(bf16)
        wr = jnp.concatenate([wr_hi, (wr - wr_hi.astype(jnp.float32)).astype(bf16)], axis=1)
        br = jnp.zeros((1, ROUTE_LANES), jnp.float32)
        br = br.at[0, 0:N_EXPERTS].set(router_expert_b[i]).at[0, GROUP_LANE0:GROUP_LANE0 + N_GROUPS].set(router_group_b[i])

        z = _in_proj(x2d, attn_norm[i].reshape(1, d), w_z, tm_in, 1024)
        kcv = z[:, OFF_KC:OFF_KC + 2 * KV_WIDTH].reshape(b, ncp, CMP_STRIDE, 2 * KV_HEADS, HEAD_DIM)
        kcv = kcv.transpose(0, 3, 2, 1, 4)
        cmp_kv = _compress(kcv, cmp_pe[i], cmp_w1[i].astype(bf16), cmp_w2[i].astype(bf16))
        o_cmp, neg_t = _cmp_attn(z, cmp_kv, cmp_tab, b, s)
        neg = jnp.swapaxes(neg_t, 2, 3)
        o_slc = _flash(z, sel_tab, neg, b, s, selected=True)
        o_win = _flash(z, win_tab, None, b, s, selected=False)
        uconv, o_sc = _conv(z, conf_conv_w[i], conf_conv_b[i].reshape(1, -1), sc_conv_w[i], b, s, min(512, s), 256)
        merged = _merge(z, o_cmp, o_slc, o_win, uconv, conf_ln_g[i].reshape(1, -1), conf_ln_b[i].reshape(1, -1),
                        o_sc, w_branch[i].astype(bf16), tm, 1024)

        x1, h2, route, cnts = _out_router(x2d, merged, w_out[i].astype(bf16), ffn_norm[i].reshape(1, d), wr, br, tm)
        dest3, tile_e, tile_v, tile_b, zstart, p_rows = _route_plan(route, cnts[-1], t, tmd)
        xs = _dispatch(zstart, dest3, h2, p_rows, tmd)
        y = _experts(tile_e, tile_v, tile_b, xs, expert_w_gate, expert_w_up, expert_w_down, i)

        x2d = _combine_ple(dest3, x1, route, y, p[i].reshape(t, PLE_DIM), ple_norm[i].reshape(1, d),
                           ple_gate_w[i].astype(bf16), ple_proj_w[i].astype(bf16), final_norm.reshape(1, d),
                           tmd, i == depth - 1)
    return x2d.reshape(b, s, d)
```

```python
import functools
import math

import jax
import jax.numpy as jnp
from jax import lax
from jax.experimental import pallas as pl
from jax.experimental.pallas import tpu as pltpu

D_MODEL = 2048
N_HEADS = 8
HEAD_DIM = 128
KV_HEADS = 2
HEADS_PER_GROUP = N_HEADS // KV_HEADS
NSA_WIDTH = N_HEADS * HEAD_DIM
KV_WIDTH = KV_HEADS * HEAD_DIM
N_NSA_BRANCHES = 3
CMP_BLOCK = 32
CMP_STRIDE = 16
CMP_HIDDEN = 512
SEL_BLOCK = 64
SEL_TOPN = 16
WINDOW = 512
WIN_QBLOCK = 128
CONF_WIDTH = 1024
CONF_CONV_WIDTH = 31
SC_WIDTH = 1024
SC_CONV_WIDTH = 3
REL_BUCKETS = 32
REL_MAX_DIST = 128
N_GROUPS = 4
EXPERTS_PER_GROUP = 8
N_EXPERTS = N_GROUPS * EXPERTS_PER_GROUP
D_EXPERT = 512
PLE_DIM = 256
EPS = 1e-6
NEG_INF = -1e30
FORCE_SCORE = 1e9
SCALE = HEAD_DIM ** -0.5
LOG2E = math.log2(math.e)

LANE = 128
SUBLANE = 8
VMEM_LIMIT = 56 * 1024 * 1024

OFF_MERGE = 0
OFF_SC = OFF_MERGE + 3 * D_MODEL
OFF_CONF = OFF_SC + 3 * SC_WIDTH
OFF_Q = OFF_CONF + 2 * CONF_WIDTH
OFF_KC = OFF_Q + NSA_WIDTH
OFF_VC = OFF_KC + KV_WIDTH
OFF_KS = OFF_VC + KV_WIDTH
OFF_VS = OFF_KS + KV_WIDTH
OFF_KW = OFF_VS + KV_WIDTH
OFF_VW = OFF_KW + KV_WIDTH
OFF_NSAG = OFF_VW + KV_WIDTH
Z_WIDTH = OFF_NSAG + 512

QW = HEADS_PER_GROUP * HEAD_DIM
TQ = 128
SEL_TQ = 512
SEL_TK = 256
N_PREV = WINDOW // WIN_QBLOCK
ROUTE_LANES = LANE
TMX = 256


def _cparams(sem, vmem=VMEM_LIMIT):
    return pltpu.CompilerParams(dimension_semantics=sem, vmem_limit_bytes=vmem)


def _t5_bucket(dist):
    n = jnp.maximum(dist, 0)
    max_exact = REL_BUCKETS // 2
    nf = jnp.maximum(n, 1).astype(jnp.float32)
    large = max_exact + (jnp.log(nf / max_exact) / math.log(REL_MAX_DIST / max_exact)
                         * (REL_BUCKETS - max_exact)).astype(jnp.int32)
    large = jnp.minimum(large, REL_BUCKETS - 1)
    return jnp.where(n < max_exact, n, large)


def _bias_table_kernel(rel_ref, o_ref, *, col_mult, d0_base, d0_step, lo, hi, rows, cols, out_scale):
    h = pl.program_id(0)
    k = pl.program_id(1)
    rb = pl.program_id(2)
    cb = pl.program_id(3)
    d0 = d0_base + d0_step * k
    d_min = rb * rows - col_mult * (cb * cols + cols - 1) + d0
    d_max = rb * rows + rows - 1 - col_mult * cb * cols + d0
    masked = jnp.logical_or(d_max < lo, d_min >= hi)
    far = jnp.logical_and(d_min >= REL_MAX_DIST, d_max < hi)

    @pl.when(masked)
    def _():
        o_ref[0, 0] = jnp.full((rows, cols), NEG_INF * out_scale, jnp.float32)

    @pl.when(far)
    def _():
        o_ref[0, 0] = jnp.full((rows, cols), rel_ref[REL_BUCKETS - 1, h] * out_scale, jnp.float32)

    @pl.when(jnp.logical_not(jnp.logical_or(masked, far)))
    def _():
        r = lax.broadcasted_iota(jnp.int32, (rows, cols), 0) + rb * rows
        c = lax.broadcasted_iota(jnp.int32, (rows, cols), 1) + cb * cols
        dist = r - col_mult * c + d0
        bucket = _t5_bucket(dist)
        val = jnp.zeros((rows, cols), jnp.float32)
        for b in range(REL_BUCKETS):
            val = jnp.where(bucket == b, rel_ref[b, h], val)
        ok = jnp.where(dist >= lo, jnp.where(dist < hi, 1.0, 0.0), 0.0)
        o_ref[0, 0] = jnp.where(ok > 0.5, val, NEG_INF) * out_scale


def _bias_tables(rel_bias, n_k, n_rows, rows, n_cols, **kw):
    cols = min(LANE, n_cols)
    kern = functools.partial(_bias_table_kernel, rows=rows, cols=cols, **kw)
    return pl.pallas_call(
        kern,
        grid=(N_HEADS, n_k, n_rows // rows, n_cols // cols),
        in_specs=[pl.BlockSpec(memory_space=pltpu.SMEM)],
        out_specs=pl.BlockSpec((1, 1, rows, cols), lambda h, k, r, c: (h, k, r, c)),
        out_shape=jax.ShapeDtypeStruct((N_HEADS, n_k, n_rows, n_cols), jnp.float32),
        compiler_params=_cparams(("parallel", "parallel", "parallel", "parallel")),
        name="bias_tables",
    )(rel_bias)


def _inproj_kernel(x_ref, g_ref, w_ref, o_ref, h_scr):
    @pl.when(pl.program_id(1) == 0)
    def _():
        x = x_ref[...]
        var = jnp.mean(x * x, axis=-1, keepdims=True)
        h_scr[...] = (x * lax.rsqrt(var + EPS) * g_ref[...]).astype(jnp.bfloat16)

    o_ref[...] = jnp.dot(h_scr[...], w_ref[...], preferred_element_type=jnp.float32).astype(o_ref.dtype)


def _in_proj(x, g, w, tm, tn):
    t = x.shape[0]
    return pl.pallas_call(
        _inproj_kernel,
        grid=(t // tm, Z_WIDTH // tn),
        in_specs=[pl.BlockSpec((tm, D_MODEL), lambda i, j: (i, 0)),
                  pl.BlockSpec((1, D_MODEL), lambda i, j: (0, 0)),
                  pl.BlockSpec((D_MODEL, tn), lambda i, j: (0, j))],
        out_specs=pl.BlockSpec((tm, tn), lambda i, j: (i, j)),
        out_shape=jax.ShapeDtypeStruct((t, Z_WIDTH), jnp.bfloat16),
        scratch_shapes=[pltpu.VMEM((tm, D_MODEL), jnp.bfloat16)],
        compiler_params=_cparams(("parallel", "arbitrary")),
        name="in_proj",
    )(x, g, w)


def _compress_kernel(a_ref, pe_ref, w1_ref, w2_ref, o_ref, *, ncp):
    lo = jnp.zeros((ncp, CMP_HIDDEN), jnp.float32)
    hi = jnp.zeros((ncp, CMP_HIDDEN), jnp.float32)
    for l in range(CMP_STRIDE):
        a = a_ref[0, 0, l].astype(jnp.float32)
        a_lo = (a + pe_ref[0, l:l + 1, :]).astype(jnp.bfloat16)
        a_hi = (a + pe_ref[0, CMP_STRIDE + l:CMP_STRIDE + l + 1, :]).astype(jnp.bfloat16)
        lo = lo + jnp.dot(a_lo, w1_ref[0, l * HEAD_DIM:(l + 1) * HEAD_DIM, :],
                          preferred_element_type=jnp.float32)
        hi = hi + jnp.dot(a_hi, w1_ref[0, (CMP_STRIDE + l) * HEAD_DIM:(CMP_STRIDE + l + 1) * HEAD_DIM, :],
                          preferred_element_type=jnp.float32)
    hidden = lo + pltpu.roll(hi, ncp - 1, 0)
    act = jax.nn.gelu(hidden).astype(jnp.bfloat16)
    out = jnp.dot(act, w2_ref[0], preferred_element_type=jnp.float32)
    row = lax.broadcasted_iota(jnp.int32, (ncp, HEAD_DIM), 0)
    o_ref[0, 0] = jnp.where(row < ncp - 1, out, 0.0).astype(o_ref.dtype)


def _compress(a, pe, w1, w2):
    b, _, _, ncp, _ = a.shape
    return pl.pallas_call(
        functools.partial(_compress_kernel, ncp=ncp),
        grid=(b, 2 * KV_HEADS),
        in_specs=[pl.BlockSpec((1, 1, CMP_STRIDE, ncp, HEAD_DIM), lambda i, j: (i, j, 0, 0, 0)),
                  pl.BlockSpec((1, CMP_BLOCK, HEAD_DIM), lambda i, j: (j // KV_HEADS, 0, 0)),
                  pl.BlockSpec((1, CMP_BLOCK * HEAD_DIM, CMP_HIDDEN), lambda i, j: (j // KV_HEADS, 0, 0)),
                  pl.BlockSpec((1, CMP_HIDDEN, HEAD_DIM), lambda i, j: (j // KV_HEADS, 0, 0))],
        out_specs=pl.BlockSpec((1, 1, ncp, HEAD_DIM), lambda i, j: (i, j, 0, 0)),
        out_shape=jax.ShapeDtypeStruct((b, 2 * KV_HEADS, ncp, HEAD_DIM), jnp.bfloat16),
        compiler_params=_cparams(("parallel", "parallel")),
        name="compress",
    )(a, pe, w1, w2)


def _cmp_attn_kernel(q_ref, kc_ref, vc_ref, tab_ref, o_ref, neg_ref, *, tq, ncp, nsel):
    i = pl.program_id(2)
    kc = kc_ref[0, 0]
    vc = vc_ref[0, 0]
    q4 = jnp.concatenate([q_ref[:, h * HEAD_DIM:(h + 1) * HEAD_DIM] for h in range(HEADS_PER_GROUP)], axis=0)
    tab = jnp.concatenate([tab_ref[h, 0] for h in range(HEADS_PER_GROUP)], axis=0)
    s = lax.dot_general(q4, kc, (((1,), (1,)), ((), ())), preferred_element_type=jnp.float32) * SCALE + tab
    valid = tab > 0.5 * NEG_INF
    m = jnp.max(s, axis=-1, keepdims=True)
    e = jnp.where(valid, jnp.exp(s - m), 0.0)
    den = jnp.sum(e, axis=-1, keepdims=True)
    p = jnp.where(valid, e / den, 0.0)
    o = jnp.dot(p.astype(jnp.bfloat16), vc, preferred_element_type=jnp.float32)
    psum = p[0:tq]
    for h in range(HEADS_PER_GROUP):
        o_ref[:, h * HEAD_DIM:(h + 1) * HEAD_DIM] = o[h * tq:(h + 1) * tq].astype(o_ref.dtype)
        if h:
            psum = psum + p[h * tq:(h + 1) * tq]
    blk_n = lax.broadcasted_iota(jnp.int32, (nsel, ncp), 0)
    c_start = lax.broadcasted_iota(jnp.int32, (nsel, ncp), 1) * CMP_STRIDE
    ov = jnp.where(c_start < (blk_n + 1) * SEL_BLOCK,
                   jnp.where(c_start + CMP_BLOCK > blk_n * SEL_BLOCK, 1.0, 0.0), 0.0)
    imp = lax.dot_general(ov, psum, (((1,), (1,)), ((), ())), precision=lax.Precision.HIGHEST,
                          preferred_element_type=jnp.float32)
    blk = lax.broadcasted_iota(jnp.int32, (nsel, tq), 0)
    t = lax.broadcasted_iota(jnp.int32, (nsel, tq), 1) + i * tq
    cur = lax.shift_right_logical(t, 6)
    forced = jnp.where(blk == 0, 1.0, jnp.where(blk == cur, 1.0, jnp.where(blk == cur - 1, 1.0, 0.0)))
    causal = blk * SEL_BLOCK <= t
    score = jnp.where(forced > 0.5, FORCE_SCORE, jnp.where(causal, imp, NEG_INF))
    rank = jnp.zeros((nsel, tq), jnp.float32)
    for k in range(nsel):
        row = score[k:k + 1, :]
        rank = rank + jnp.where(blk > k, jnp.where(row >= score, 1.0, 0.0), jnp.where(row > score, 1.0, 0.0))
    n_top = min(SEL_TOPN, nsel)
    neg = jnp.where(rank < n_top, jnp.where(causal, 0.0, NEG_INF), NEG_INF)
    neg_ref[0, 0] = neg.astype(neg_ref.dtype)


def _cmp_attn(z, cmp_kv, tab, b, s):
    ncp = s // CMP_STRIDE
    nsel = s // SEL_BLOCK
    nq = s // TQ
    kern = functools.partial(_cmp_attn_kernel, tq=TQ, ncp=ncp, nsel=nsel)
    return pl.pallas_call(
        kern,
        grid=(b, KV_HEADS, nq),
        in_specs=[pl.BlockSpec((TQ, QW), lambda bi, g, i: (bi * nq + i, OFF_Q // QW + g)),
                  pl.BlockSpec((1, 1, ncp, HEAD_DIM), lambda bi, g, i: (bi, g, 0, 0)),
                  pl.BlockSpec((1, 1, ncp, HEAD_DIM), lambda bi, g, i: (bi, KV_HEADS + g, 0, 0)),
                  pl.BlockSpec((HEADS_PER_GROUP, 1, TQ, ncp), lambda bi, g, i: (g, 0, i, 0))],
        out_specs=[pl.BlockSpec((TQ, QW), lambda bi, g, i: (bi * nq + i, g)),
                   pl.BlockSpec((1, 1, nsel, TQ), lambda bi, g, i: (bi, g, 0, i))],
        out_shape=[jax.ShapeDtypeStruct((b * s, NSA_WIDTH), jnp.bfloat16),
                   jax.ShapeDtypeStruct((b, KV_HEADS, nsel, s), jnp.bfloat16)],
        compiler_params=_cparams(("parallel", "parallel", "parallel")),
        name="cmp_attn",
    )(z, cmp_kv, cmp_kv, tab)


def _stack_heads(q_ref, g):
    q = jnp.concatenate(
        [q_ref[:, (g * HEADS_PER_GROUP + h) * HEAD_DIM:(g * HEADS_PER_GROUP + h + 1) * HEAD_DIM]
         for h in range(HEADS_PER_GROUP)], axis=0)
    return (q.astype(jnp.float32) * (SCALE * LOG2E)).astype(jnp.bfloat16)


def _unstack_heads(o_ref, g, out, tq):
    for h in range(HEADS_PER_GROUP):
        c0 = (g * HEADS_PER_GROUP + h) * HEAD_DIM
        o_ref[:, c0:c0 + HEAD_DIM] = out[h * tq:(h + 1) * tq].astype(o_ref.dtype)


def _sel_kernel(q_ref, k_ref, v_ref, tab_ref, neg_ref, o_ref, m0_scr, m1_scr, acc0_scr, acc1_scr, *, tq, tk, nsel):
    i = pl.program_id(1)
    sub = tk // LANE
    pad = jnp.zeros((HEADS_PER_GROUP * tq, HEAD_DIM - nsel), jnp.bfloat16)
    q4 = [jnp.concatenate([_stack_heads(q_ref, g), jnp.concatenate([neg_ref[0, g]] * HEADS_PER_GROUP, axis=0), pad],
                          axis=1) for g in range(KV_HEADS)]
    kpad = jnp.zeros((tk, HEAD_DIM - nsel), jnp.bfloat16)
    m_scr = (m0_scr, m1_scr)
    acc_scr = (acc0_scr, acc1_scr)
    for g in range(KV_HEADS):
        m_scr[g][...] = jnp.full(m_scr[g].shape, NEG_INF, jnp.float32)
        acc_scr[g][...] = jnp.zeros(acc_scr[g].shape, jnp.float32)
    u_min = 1 - tq // tk
    n_far = tab_ref.shape[1] - 1
    ones = jnp.ones((tk, LANE), jnp.bfloat16)

    def body(j, carry):
        tidx = jnp.minimum(i * (tq // tk) - j - u_min, n_far)
        k_i = lax.broadcasted_iota(jnp.int32, (tk, nsel), 0)
        n_i = lax.broadcasted_iota(jnp.int32, (tk, nsel), 1)
        onehot = jnp.where(lax.shift_right_logical(k_i + j * tk, 6) == n_i, 1.0, 0.0).astype(jnp.bfloat16)
        start = pl.multiple_of(j * tk, tk)
        for g in range(KV_HEADS):
            gs = slice(g * HEAD_DIM, (g + 1) * HEAD_DIM)
            kt = jnp.concatenate([k_ref[pl.ds(start, tk), gs], onehot, kpad], axis=1)
            vt = jnp.concatenate([v_ref[pl.ds(start, tk), gs], ones], axis=1)
            s = lax.dot_general(q4[g], kt, (((1,), (1,)), ((), ())), preferred_element_type=jnp.float32)
            s = s + tab_ref[g, tidx]
            m_prev = m_scr[g][...]
            m_new = jnp.maximum(m_prev, jnp.max(s, axis=-1, keepdims=True))
            alpha = jnp.exp2(m_prev - m_new)
            p = jnp.exp2(s - jnp.concatenate([m_new] * sub, axis=1))
            acc_scr[g][...] = (jnp.concatenate([alpha, alpha], axis=1) * acc_scr[g][...]
                               + jnp.dot(p.astype(jnp.bfloat16), vt, preferred_element_type=jnp.float32))
            m_scr[g][...] = m_new
        return carry

    lax.fori_loop(0, (i + 1) * (tq // tk), body, 0)
    for g in range(KV_HEADS):
        _unstack_heads(o_ref, g, acc_scr[g][:, 0:HEAD_DIM] / acc_scr[g][:, HEAD_DIM:2 * HEAD_DIM], tq)


def _win_kernel(q_ref, k_ref, v_ref, tab_ref, o_ref, *, tq):
    i = pl.program_id(1)
    nb = N_PREV + 1
    sb = jnp.maximum(i - N_PREV, 0)
    off = sb - i + N_PREV
    start = pl.multiple_of(sb * WIN_QBLOCK, WIN_QBLOCK)
    ones = jnp.ones((nb * WIN_QBLOCK, LANE), jnp.bfloat16)
    for g in range(KV_HEADS):
        gs = slice(g * HEAD_DIM, (g + 1) * HEAD_DIM)
        kt = k_ref[pl.ds(start, nb * WIN_QBLOCK), gs]
        vt = jnp.concatenate([v_ref[pl.ds(start, nb * WIN_QBLOCK), gs], ones], axis=1)
        s = lax.dot_general(_stack_heads(q_ref, g), kt, (((1,), (1,)), ((), ())),
                            preferred_element_type=jnp.float32)
        s = s + jnp.concatenate([tab_ref[g, jnp.minimum(u + off, nb)] for u in range(nb)], axis=1)
        p = jnp.exp2(s - jnp.max(s, axis=-1, keepdims=True))
        pv = jnp.dot(p.astype(jnp.bfloat16), vt, preferred_element_type=jnp.float32)
        _unstack_heads(o_ref, g, pv[:, 0:HEAD_DIM] / pv[:, HEAD_DIM:2 * HEAD_DIM], tq)


def _flash(z, tab, neg, b, s, *, selected):
    tq = SEL_TQ if selected else TQ
    nq = s // tq
    tk = SEL_TK if selected else WIN_QBLOCK
    k_off, v_off = (OFF_KS, OFF_VS) if selected else (OFF_KW, OFF_VW)
    rows = HEADS_PER_GROUP * tq
    n_tab = tab.shape[1]
    in_specs = [pl.BlockSpec((tq, NSA_WIDTH), lambda bi, i: (bi * nq + i, OFF_Q // NSA_WIDTH)),
                pl.BlockSpec((s, KV_WIDTH), lambda bi, i: (bi, k_off // KV_WIDTH)),
                pl.BlockSpec((s, KV_WIDTH), lambda bi, i: (bi, v_off // KV_WIDTH)),
                pl.BlockSpec((KV_HEADS, n_tab, rows, tk), lambda bi, i: (0, 0, 0, 0),
                             pipeline_mode=pl.Buffered(1))]
    args = [z, z, z, tab]
    scratch = []
    if selected:
        nsel = s // SEL_BLOCK
        in_specs.append(pl.BlockSpec((1, KV_HEADS, tq, nsel), lambda bi, i: (bi, 0, i, 0)))
        args.append(neg)
        kern = functools.partial(_sel_kernel, tq=tq, tk=tk, nsel=nsel)
        scratch = ([pltpu.VMEM((rows, LANE), jnp.float32)] * KV_HEADS
                   + [pltpu.VMEM((rows, 2 * HEAD_DIM), jnp.float32)] * KV_HEADS)
    else:
        kern = functools.partial(_win_kernel, tq=tq)
    return pl.pallas_call(
        kern,
        grid=(b, nq),
        in_specs=in_specs,
        out_specs=pl.BlockSpec((tq, NSA_WIDTH), lambda bi, i: (bi * nq + i, 0)),
        out_shape=jax.ShapeDtypeStruct((b * s, NSA_WIDTH), jnp.bfloat16),
        scratch_shapes=scratch,
        compiler_params=_cparams(("parallel", "parallel")),
        name="flash_sel" if selected else "flash_win",
    )(*args)


HALO = 32
CONV_RC = 64


def _conv_kernel(a_ref, g_ref, ah_ref, gh_ref, bg_ref, cg_ref, xs_ref, cgh_ref, xsh_ref,
                 cw_ref, cb_ref, sw_ref, uo_ref, so_ref, ext_scr, ext2_scr, *, ts):
    first = pl.program_id(1) == 0
    f32 = jnp.float32
    n_ext = HALO + ts
    n_sh = n_ext - SUBLANE
    u = a_ref[...].astype(f32) * jax.nn.sigmoid(g_ref[...].astype(f32))
    uh = ah_ref[...].astype(f32) * jax.nn.sigmoid(gh_ref[...].astype(f32))
    ext_scr[0, 0:HALO, :] = jnp.where(first, 0.0, uh)
    ext_scr[0, HALO:n_ext, :] = u
    v = cg_ref[...].astype(f32) * xs_ref[...].astype(f32)
    vh = cgh_ref[...].astype(f32) * xsh_ref[...].astype(f32)
    ext2_scr[0, 0:HALO, :] = jnp.where(first, 0.0, vh)
    ext2_scr[0, HALO:n_ext, :] = v
    base = HALO - (CONF_CONV_WIDTH - 1)
    base2 = HALO - (SC_CONV_WIDTH - 1)
    for r in range(1, SUBLANE):
        ext_scr[r, 0:n_sh, :] = ext_scr[0, r:r + n_sh, :]
    sc_shifts = sorted({(base2 + k) % SUBLANE for k in range(SC_CONV_WIDTH)} - {0})
    for r in sc_shifts:
        ext2_scr[r, 0:n_sh, :] = ext2_scr[0, r:r + n_sh, :]

    def tap(scr, off, r0):
        r = off % SUBLANE
        return scr[r, r0 + off - r:r0 + off - r + CONV_RC, :]

    for r0 in range(0, ts, CONV_RC):
        acc = jnp.zeros((CONV_RC, a_ref.shape[1]), f32) + cb_ref[...]
        for k in range(CONF_CONV_WIDTH):
            acc = acc + cw_ref[k:k + 1, :] * tap(ext_scr, base + k, r0)
        uo_ref[r0:r0 + CONV_RC, :] = acc.astype(uo_ref.dtype)
        acc2 = jnp.zeros((CONV_RC, a_ref.shape[1]), f32)
        for k in range(SC_CONV_WIDTH):
            acc2 = acc2 + sw_ref[k:k + 1, :] * tap(ext2_scr, base2 + k, r0)
        so_ref[r0:r0 + CONV_RC, :] = (bg_ref[r0:r0 + CONV_RC, :].astype(f32) * acc2).astype(so_ref.dtype)


def _conv(z, cw, cb, sw, b, s, ts, tc):
    ns = s // ts
    t = b * s

    def cur(off):
        return pl.BlockSpec((ts, tc), lambda bi, i, c: (bi * ns + i, off // tc + c))

    def halo(off):
        return pl.BlockSpec((HALO, tc), lambda bi, i, c: (jnp.maximum((bi * s + i * ts) // HALO - 1, 0), off // tc + c))

    return pl.pallas_call(
        functools.partial(_conv_kernel, ts=ts),
        grid=(b, ns, CONF_WIDTH // tc),
        in_specs=[cur(OFF_CONF), cur(OFF_CONF + CONF_WIDTH), halo(OFF_CONF), halo(OFF_CONF + CONF_WIDTH),
                  cur(OFF_SC), cur(OFF_SC + SC_WIDTH), cur(OFF_SC + 2 * SC_WIDTH),
                  halo(OFF_SC + SC_WIDTH), halo(OFF_SC + 2 * SC_WIDTH),
                  pl.BlockSpec((CONF_CONV_WIDTH, tc), lambda bi, i, c: (0, c)),
                  pl.BlockSpec((1, tc), lambda bi, i, c: (0, c)),
                  pl.BlockSpec((SC_CONV_WIDTH, tc), lambda bi, i, c: (0, c))],
        out_specs=[pl.BlockSpec((ts, tc), lambda bi, i, c: (bi * ns + i, c)),
                   pl.BlockSpec((ts, tc), lambda bi, i, c: (bi * ns + i, c))],
        out_shape=[jax.ShapeDtypeStruct((t, CONF_WIDTH), jnp.bfloat16),
                   jax.ShapeDtypeStruct((t, SC_WIDTH), jnp.bfloat16)],
        scratch_shapes=[pltpu.VMEM((SUBLANE, HALO + ts, tc), jnp.float32),
                        pltpu.VMEM((SUBLANE, HALO + ts, tc), jnp.float32)],
        compiler_params=_cparams(("parallel", "parallel", "parallel")),
        name="conv",
    )(z, z, z, z, z, z, z, z, z, cw, cb, sw)


def _merge_kernel(ocmp_ref, oslc_ref, owin_ref, ng_ref, uc_ref, lng_ref, lnb_ref, osc_ref,
                  mg0_ref, mg1_ref, mg2_ref, wn_ref, wc_ref, ws_ref, o_ref, nsa_scr, conf_scr):
    f32 = jnp.float32

    @pl.when(pl.program_id(1) == 0)
    def _():
        gt = jax.nn.sigmoid(ng_ref[...].astype(f32))
        for h in range(N_HEADS):
            sl = slice(h * HEAD_DIM, (h + 1) * HEAD_DIM)
            c = N_NSA_BRANCHES * h
            o = (gt[:, c:c + 1] * ocmp_ref[:, sl].astype(f32) + gt[:, c + 1:c + 2] * oslc_ref[:, sl].astype(f32)
                 + gt[:, c + 2:c + 3] * owin_ref[:, sl].astype(f32))
            nsa_scr[:, sl] = o.astype(nsa_scr.dtype)
        u = uc_ref[...].astype(f32)
        mu = jnp.mean(u, axis=-1, keepdims=True)
        var = jnp.mean(jnp.square(u - mu), axis=-1, keepdims=True)
        y = (u - mu) * lax.rsqrt(var + EPS) * lng_ref[...] + lnb_ref[...]
        conf_scr[...] = (y * jax.nn.sigmoid(y)).astype(conf_scr.dtype)

    a = jnp.dot(nsa_scr[...], wn_ref[...], preferred_element_type=f32)
    b = jnp.dot(conf_scr[...], wc_ref[...], preferred_element_type=f32)
    c = jnp.dot(osc_ref[...], ws_ref[...], preferred_element_type=f32)
    m = (jax.nn.sigmoid(mg0_ref[...].astype(f32)) * a + jax.nn.sigmoid(mg1_ref[...].astype(f32)) * b
         + jax.nn.sigmoid(mg2_ref[...].astype(f32)) * c)
    o_ref[...] = m.astype(o_ref.dtype)


def _merge(z, ocmp, oslc, owin, uconv, lng, lnb, osc, wb, tm, tn):
    t = z.shape[0]
    nj = D_MODEL // tn
    row = lambda w: pl.BlockSpec((tm, w), lambda i, j: (i, 0))
    return pl.pallas_call(
        _merge_kernel,
        grid=(t // tm, nj),
        in_specs=[row(NSA_WIDTH), row(NSA_WIDTH), row(NSA_WIDTH),
                  pl.BlockSpec((tm, LANE), lambda i, j: (i, OFF_NSAG // LANE)),
                  row(CONF_WIDTH),
                  pl.BlockSpec((1, CONF_WIDTH), lambda i, j: (0, 0)),
                  pl.BlockSpec((1, CONF_WIDTH), lambda i, j: (0, 0)),
                  row(SC_WIDTH),
                  pl.BlockSpec((tm, tn), lambda i, j: (i, j)),
                  pl.BlockSpec((tm, tn), lambda i, j: (i, nj + j)),
                  pl.BlockSpec((tm, tn), lambda i, j: (i, 2 * nj + j)),
                  pl.BlockSpec((NSA_WIDTH, tn), lambda i, j: (0, j)),
                  pl.BlockSpec((CONF_WIDTH, tn), lambda i, j: (NSA_WIDTH // CONF_WIDTH, j)),
                  pl.BlockSpec((SC_WIDTH, tn), lambda i, j: ((NSA_WIDTH + CONF_WIDTH) // SC_WIDTH, j))],
        out_specs=pl.BlockSpec((tm, tn), lambda i, j: (i, j)),
        out_shape=jax.ShapeDtypeStruct((t, D_MODEL), jnp.bfloat16),
        scratch_shapes=[pltpu.VMEM((tm, NSA_WIDTH), jnp.bfloat16), pltpu.VMEM((tm, CONF_WIDTH), jnp.bfloat16)],
        compiler_params=_cparams(("parallel", "arbitrary")),
        name="merge",
    )(ocmp, oslc, owin, z, uconv, lng, lnb, osc, z, z, z, wb, wb, wb)


R_E1, R_E2, R_W1, R_W2, R_RANK1, R_RANK2 = range(6)
GROUP_LANE0 = N_EXPERTS


def _out_router_kernel(x_ref, m_ref, wo_ref, fn_ref, wr_ref, br_ref, x1_ref, h2_ref, route_ref, cnt_ref,
                       base_scr, *, tm):
    f32 = jnp.float32

    @pl.when(pl.program_id(0) == 0)
    def _():
        base_scr[...] = jnp.zeros_like(base_scr)

    x1 = x_ref[...] + jnp.dot(m_ref[...], wo_ref[...], preferred_element_type=f32)
    x1_ref[...] = x1
    var = jnp.mean(x1 * x1, axis=-1, keepdims=True)
    h2 = x1 * lax.rsqrt(var + EPS) * fn_ref[...]
    h2_ref[...] = _pack_bf16_pairs(h2)
    h_hi = h2.astype(jnp.bfloat16)
    h_lo = (h2 - h_hi.astype(f32)).astype(jnp.bfloat16)
    l_hi = jnp.dot(h_hi, wr_ref[...], preferred_element_type=f32)
    l_lo = jnp.dot(h_lo, wr_ref[:, 0:ROUTE_LANES], preferred_element_type=f32)
    logits = l_hi[:, 0:ROUTE_LANES] + l_hi[:, ROUTE_LANES:2 * ROUTE_LANES] + l_lo + br_ref[...]
    lane = lax.broadcasted_iota(jnp.int32, (tm, ROUTE_LANES), 1).astype(f32)
    big = float(ROUTE_LANES)
    is_g = jnp.where(lane >= GROUP_LANE0, jnp.where(lane < GROUP_LANE0 + N_GROUPS, 1.0, 0.0), 0.0) > 0.5
    gl = jnp.where(is_g, logits, NEG_INF)
    gmax = jnp.max(gl, axis=-1, keepdims=True)
    glane = jnp.min(jnp.where(gl == gmax, lane, big), axis=-1, keepdims=True)
    gsum = jnp.sum(jnp.where(is_g, jnp.exp(gl - gmax), 0.0), axis=-1, keepdims=True)
    g_w = 1.0 / gsum
    grp = glane - GROUP_LANE0
    in_grp = jnp.floor(lane * (1.0 / EXPERTS_PER_GROUP)) == grp
    el = jnp.where(in_grp, logits, NEG_INF)
    emax = jnp.max(el, axis=-1, keepdims=True)
    ee = jnp.where(in_grp, jnp.exp(el - emax), 0.0)
    ep = ee / jnp.sum(ee, axis=-1, keepdims=True)
    ep = jnp.where(in_grp, ep, -1.0)
    p1 = jnp.max(ep, axis=-1, keepdims=True)
    i1 = jnp.min(jnp.where(ep == p1, lane, big), axis=-1, keepdims=True)
    ep2 = jnp.where(lane == i1, -1.0, ep)
    p2 = jnp.max(ep2, axis=-1, keepdims=True)
    i2 = jnp.min(jnp.where(ep2 == p2, lane, big), axis=-1, keepdims=True)
    psum = p1 + p2
    w1 = g_w * (p1 / psum)
    w2 = g_w * (p2 / psum)
    onehot = jnp.where(lane == i1, 1.0, jnp.where(lane == i2, 1.0, 0.0))
    r_i = lax.broadcasted_iota(jnp.int32, (tm, tm), 0)
    c_i = lax.broadcasted_iota(jnp.int32, (tm, tm), 1)
    tri = jnp.where(c_i < r_i, 1.0, 0.0).astype(jnp.bfloat16)
    cum = jnp.dot(tri, onehot.astype(jnp.bfloat16), preferred_element_type=f32) + base_scr[0:1, :]
    rank1 = jnp.sum(jnp.where(lane == i1, cum, 0.0), axis=-1, keepdims=True)
    rank2 = jnp.sum(jnp.where(lane == i2, cum, 0.0), axis=-1, keepdims=True)
    new_base = base_scr[0:1, :] + jnp.sum(onehot, axis=0, keepdims=True)
    base_scr[...] = jnp.broadcast_to(new_base, base_scr.shape)
    cnt_ref[...] = jnp.broadcast_to(new_base, cnt_ref.shape)
    rec = jnp.zeros((tm, ROUTE_LANES), f32)
    for ln, val in ((R_E1, i1), (R_E2, i2), (R_W1, w1), (R_W2, w2), (R_RANK1, rank1), (R_RANK2, rank2)):
        rec = jnp.where(lane == ln, val, rec)
    route_ref[...] = rec


def _out_router(x, merged, wo, fn, wr, br, tm):
    t = x.shape[0]
    n = t // tm
    return pl.pallas_call(
        functools.partial(_out_router_kernel, tm=tm),
        grid=(n,),
        in_specs=[pl.BlockSpec((tm, D_MODEL), lambda i: (i, 0)),
                  pl.BlockSpec((tm, D_MODEL), lambda i: (i, 0)),
                  pl.BlockSpec((D_MODEL, D_MODEL), lambda i: (0, 0)),
                  pl.BlockSpec((1, D_MODEL), lambda i: (0, 0)),
                  pl.BlockSpec((D_MODEL, 2 * ROUTE_LANES), lambda i: (0, 0)),
                  pl.BlockSpec((1, ROUTE_LANES), lambda i: (0, 0))],
        out_specs=[pl.BlockSpec((tm, D_MODEL), lambda i: (i, 0)),
                   pl.BlockSpec((tm, D_MODEL // 2), lambda i: (i, 0)),
                   pl.BlockSpec((tm, ROUTE_LANES), lambda i: (i, 0)),
                   pl.BlockSpec((8, ROUTE_LANES), lambda i: (i, 0))],
        out_shape=[jax.ShapeDtypeStruct((t, D_MODEL), jnp.float32),
                   jax.ShapeDtypeStruct((t, D_MODEL // 2), jnp.uint32),
                   jax.ShapeDtypeStruct((t, ROUTE_LANES), jnp.float32),
                   jax.ShapeDtypeStruct((n * 8, ROUTE_LANES), jnp.float32)],
        scratch_shapes=[pltpu.VMEM((8, ROUTE_LANES), jnp.float32)],
        compiler_params=_cparams(("arbitrary",)),
        name="out_router",
    )(x, merged, wo, fn, wr, br)


HALF = D_MODEL // 2
DMA_UNROLL = 8


def _pack_bf16_pairs(x):
    lo = pltpu.bitcast(x[:, 0:HALF].astype(jnp.bfloat16).astype(jnp.float32), jnp.uint32)
    hi = pltpu.bitcast(x[:, HALF:D_MODEL].astype(jnp.bfloat16).astype(jnp.float32), jnp.uint32)
    return hi | (lo >> 16)


def _unpack_bf16_pairs(w):
    lo = pltpu.bitcast(w << 16, jnp.float32)
    hi = pltpu.bitcast(w & jnp.uint32(0xFFFF0000), jnp.float32)
    return lo, hi


def _dispatch_kernel(zs_ref, dest_ref, h_ref, xs_ref, zero_scr, sem, zsem, *, tmd):
    @pl.when(pl.program_id(0) == 0)
    def _():
        zero_scr[...] = jnp.zeros_like(zero_scr)
        for e in range(N_EXPERTS):
            @pl.when(zs_ref[e] >= 0)
            def _():
                cp = pltpu.make_async_copy(zero_scr, xs_ref.at[pl.ds(pl.multiple_of(zs_ref[e], TMX), TMX)], zsem)
                cp.start()
                cp.wait()

        def zero_tail(tile, c):
            cp = pltpu.make_async_copy(zero_scr, xs_ref.at[pl.ds(pl.multiple_of(tile * TMX, TMX), TMX)], zsem)
            cp.start()
            cp.wait()
            return c

        lax.fori_loop(zs_ref[N_EXPERTS] // TMX, xs_ref.shape[0] // TMX, zero_tail, 0)

    def row_copy(r, k):
        return pltpu.make_async_copy(h_ref.at[pl.ds(r, 1)], xs_ref.at[pl.ds(dest_ref[0, 0, 2 * r + k], 1)], sem)

    def issue(r, c):
        row_copy(r, 0).start(priority=0)
        row_copy(r, 1).start(priority=1)
        return c

    lax.fori_loop(0, tmd, issue, 0, unroll=DMA_UNROLL)

    def drain(r, c):
        row_copy(r, 0).wait()
        row_copy(r, 1).wait()
        return c

    lax.fori_loop(0, tmd, drain, 0, unroll=DMA_UNROLL)


def _dispatch(zstart, dest3, h2p, p_rows, tmd):
    t = h2p.shape[0]
    grid_spec = pltpu.PrefetchScalarGridSpec(
        num_scalar_prefetch=1,
        grid=(t // tmd,),
        in_specs=[pl.BlockSpec((1, 1, 2 * tmd), lambda i, zs: (i, 0, 0), memory_space=pltpu.SMEM),
                  pl.BlockSpec((tmd, HALF), lambda i, zs: (i, 0))],
        out_specs=pl.BlockSpec(memory_space=pl.ANY),
        scratch_shapes=[pltpu.VMEM((TMX, HALF), jnp.uint32), pltpu.SemaphoreType.DMA(()),
                        pltpu.SemaphoreType.DMA(())],
    )
    return pl.pallas_call(
        functools.partial(_dispatch_kernel, tmd=tmd),
        grid_spec=grid_spec,
        out_shape=jax.ShapeDtypeStruct((p_rows, HALF), jnp.uint32),
        compiler_params=_cparams(("arbitrary",)),
        name="dispatch",
    )(zstart, dest3, h2p)


TM_EXPERT, TM_VALID, TM_BLOCK, TM_FIRST, TM_NEXT, TM_SLOT = range(6)


def _expert_kernel(tm_ref, xs_ref, wg_hbm, wu_hbm, wd_hbm, y_ref, wg_buf, wu_buf, wd_buf, wgu_scr, wd_scr, sems,
                   *, layer):
    i = pl.program_id(0)
    e = tm_ref[TM_EXPERT, i]
    slot = tm_ref[TM_SLOT, i]

    def weight_copies(expert, s):
        return (pltpu.make_async_copy(wg_hbm.at[layer, expert], wg_buf.at[s], sems.at[s, 0]),
                pltpu.make_async_copy(wu_hbm.at[layer, expert], wu_buf.at[s], sems.at[s, 1]),
                pltpu.make_async_copy(wd_hbm.at[layer, expert], wd_buf.at[s], sems.at[s, 2]))

    @pl.when(tm_ref[TM_FIRST, i] == 1)
    def _():
        @pl.when(i == 0)
        def _():
            for cp in weight_copies(e, slot):
                cp.start()

        for cp in weight_copies(e, slot):
            cp.wait()

        @pl.when(tm_ref[TM_NEXT, i] >= 0)
        def _():
            for cp in weight_copies(tm_ref[TM_NEXT, i], 1 - slot):
                cp.start()

        wgu_scr[:, 0:D_EXPERT] = wg_buf[slot].astype(jnp.bfloat16)
        wgu_scr[:, D_EXPERT:2 * D_EXPERT] = wu_buf[slot].astype(jnp.bfloat16)
        wd_scr[...] = wd_buf[slot].astype(jnp.bfloat16)

    @pl.when(tm_ref[TM_VALID, i] == 1)
    def _():
        x_lo, x_hi = _unpack_bf16_pairs(xs_ref[...])
        gu = (jnp.dot(x_lo.astype(jnp.bfloat16), wgu_scr[0:HALF, :], preferred_element_type=jnp.float32)
              + jnp.dot(x_hi.astype(jnp.bfloat16), wgu_scr[HALF:D_MODEL, :], preferred_element_type=jnp.float32))
        gate = gu[:, 0:D_EXPERT]
        he = (gate * jax.nn.sigmoid(gate)) * gu[:, D_EXPERT:2 * D_EXPERT]
        y = jnp.dot(he.astype(jnp.bfloat16), wd_scr[...], preferred_element_type=jnp.float32)
        y_ref[...] = _pack_bf16_pairs(y)

    @pl.when(tm_ref[TM_VALID, i] == 0)
    def _():
        y_ref[...] = jnp.zeros_like(y_ref)


def _experts(tile_meta, xs, wg, wu, wd, layer):
    p_rows = xs.shape[0]
    f32 = jnp.float32
    grid_spec = pltpu.PrefetchScalarGridSpec(
        num_scalar_prefetch=1,
        grid=(p_rows // TMX,),
        in_specs=[pl.BlockSpec((TMX, HALF), lambda i, tm: (tm[TM_BLOCK, i], 0)),
                  pl.BlockSpec(memory_space=pl.ANY),
                  pl.BlockSpec(memory_space=pl.ANY),
                  pl.BlockSpec(memory_space=pl.ANY)],
        out_specs=pl.BlockSpec((TMX, HALF), lambda i, tm: (i, 0)),
        scratch_shapes=[pltpu.VMEM((2, D_MODEL, D_EXPERT), f32), pltpu.VMEM((2, D_MODEL, D_EXPERT), f32),
                        pltpu.VMEM((2, D_EXPERT, D_MODEL), f32),
                        pltpu.VMEM((D_MODEL, 2 * D_EXPERT), jnp.bfloat16),
                        pltpu.VMEM((D_EXPERT, D_MODEL), jnp.bfloat16),
                        pltpu.SemaphoreType.DMA((2, 3))],
    )
    return pl.pallas_call(
        functools.partial(_expert_kernel, layer=layer),
        grid_spec=grid_spec,
        out_shape=jax.ShapeDtypeStruct((p_rows, HALF), jnp.uint32),
        compiler_params=_cparams(("arbitrary",)),
        name="experts",
    )(tile_meta, xs, wg, wu, wd)


def _combine_ple_kernel(dcur_ref, dnext_ref, x1_ref, route_ref, y_ref, p_ref, pn_ref, wpg_ref, wpp_ref, fn_ref,
                        o_ref, buf, sems, *, tmc, final):
    f32 = jnp.float32
    i = pl.program_id(0)
    slot = lax.rem(i, 2)

    def row_copy(d_ref, s, r, k):
        return pltpu.make_async_copy(y_ref.at[pl.ds(d_ref[0, 0, 2 * r + k], 1)], buf.at[s, k, pl.ds(r, 1)],
                                     sems.at[s])

    def issue(d_ref, s):
        def body(r, c):
            row_copy(d_ref, s, r, 0).start(priority=0)
            row_copy(d_ref, s, r, 1).start(priority=1)
            return c
        lax.fori_loop(0, tmc, body, 0, unroll=DMA_UNROLL)

    @pl.when(i == 0)
    def _():
        issue(dcur_ref, 0)

    @pl.when(i + 1 < pl.num_programs(0))
    def _():
        issue(dnext_ref, 1 - slot)

    def drain(r, c):
        row_copy(dcur_ref, slot, r, 0).wait()
        row_copy(dcur_ref, slot, r, 1).wait()
        return c

    lax.fori_loop(0, tmc, drain, 0, unroll=DMA_UNROLL)

    route = route_ref[...]
    w1 = route[:, R_W1:R_W1 + 1]
    w2 = route[:, R_W2:R_W2 + 1]
    y1_lo, y1_hi = _unpack_bf16_pairs(buf[slot, 0])
    y2_lo, y2_hi = _unpack_bf16_pairs(buf[slot, 1])
    x2 = x1_ref[...] + jnp.concatenate([w1 * y1_lo + w2 * y2_lo, w1 * y1_hi + w2 * y2_hi], axis=1)
    var = jnp.mean(x2 * x2, axis=-1, keepdims=True)
    hp = (x2 * lax.rsqrt(var + EPS) * pn_ref[...]).astype(jnp.bfloat16)
    gate = jax.nn.sigmoid(jnp.dot(hp, wpg_ref[...], preferred_element_type=f32))
    pp = jnp.dot(p_ref[...].astype(jnp.bfloat16), wpp_ref[...], preferred_element_type=f32)
    x3 = x2 + gate * pp
    if final:
        var3 = jnp.mean(x3 * x3, axis=-1, keepdims=True)
        x3 = x3 * lax.rsqrt(var3 + EPS) * fn_ref[...]
    o_ref[...] = x3


def _combine_ple(dest3, x1, route, y, p, pn, wpg, wpp, fn, tmc, final):
    t = x1.shape[0]
    n = t // tmc
    return pl.pallas_call(
        functools.partial(_combine_ple_kernel, tmc=tmc, final=final),
        grid=(n,),
        in_specs=[pl.BlockSpec((1, 1, 2 * tmc), lambda i: (i, 0, 0), memory_space=pltpu.SMEM),
                  pl.BlockSpec((1, 1, 2 * tmc), lambda i: (jnp.minimum(i + 1, n - 1), 0, 0), memory_space=pltpu.SMEM),
                  pl.BlockSpec((tmc, D_MODEL), lambda i: (i, 0)),
                  pl.BlockSpec((tmc, ROUTE_LANES), lambda i: (i, 0)),
                  pl.BlockSpec(memory_space=pl.ANY),
                  pl.BlockSpec((tmc, PLE_DIM), lambda i: (i, 0)),
                  pl.BlockSpec((1, D_MODEL), lambda i: (0, 0)),
                  pl.BlockSpec((D_MODEL, D_MODEL), lambda i: (0, 0)),
                  pl.BlockSpec((PLE_DIM, D_MODEL), lambda i: (0, 0)),
                  pl.BlockSpec((1, D_MODEL), lambda i: (0, 0))],
        out_specs=pl.BlockSpec((tmc, D_MODEL), lambda i: (i, 0)),
        out_shape=jax.ShapeDtypeStruct((t, D_MODEL), jnp.float32),
        scratch_shapes=[pltpu.VMEM((2, 2, tmc, HALF), jnp.uint32), pltpu.SemaphoreType.DMA((2,))],
        compiler_params=_cparams(("arbitrary",)),
        name="combine_ple",
    )(dest3, dest3, x1, route, y, p, pn, wpg, wpp, fn)


def _regroup_w_in(w):
    o_q, o_kv, o_ng = 0, NSA_WIDTH, NSA_WIDTH + 6 * KV_WIDTH
    o_conf = o_ng + N_HEADS * N_NSA_BRANCHES
    o_sc = o_conf + 2 * CONF_WIDTH
    o_mg = o_sc + 3 * SC_WIDTH
    pad = jnp.zeros((w.shape[0], Z_WIDTH - OFF_NSAG - N_HEADS * N_NSA_BRANCHES), w.dtype)
    return jnp.concatenate([w[:, o_mg:], w[:, o_sc:o_mg], w[:, o_conf:o_sc], w[:, o_q:o_ng],
                            w[:, o_ng:o_conf], pad], axis=1).astype(jnp.bfloat16)


def _route_plan(route, counts, t, tmd):
    eid = route[:, R_E1:R_E2 + 1].astype(jnp.int32)
    rank = route[:, R_RANK1:R_RANK2 + 1].astype(jnp.int32)
    cnt = counts[:N_EXPERTS].astype(jnp.int32)
    padded = ((cnt + TMX - 1) // TMX) * TMX
    ends = jnp.cumsum(padded)
    starts = ends - padded
    dest = jnp.take(starts, eid) + rank
    p_rows = 2 * t + N_EXPERTS * TMX
    n_tiles = p_rows // TMX
    tile_start = jnp.arange(n_tiles, dtype=jnp.int32) * TMX
    tile_e = jnp.minimum(jnp.sum((tile_start[:, None] >= ends[None, :]).astype(jnp.int32), axis=1), N_EXPERTS - 1)
    tile_v = (tile_start < ends[-1]).astype(jnp.int32)
    tile_b = jnp.where(tile_v == 1, jnp.arange(n_tiles, dtype=jnp.int32), 0)
    tile_f = tile_v * jnp.concatenate([jnp.ones((1,), jnp.int32), (tile_e[1:] != tile_e[:-1]).astype(jnp.int32)])
    has = padded > 0
    e_idx = jnp.arange(N_EXPERTS, dtype=jnp.int32)
    later = jnp.where(has[None, :] & (e_idx[None, :] > e_idx[:, None]), e_idx[None, :], N_EXPERTS)
    next_e = jnp.min(later, axis=1)
    next_e = jnp.where(next_e < N_EXPERTS, next_e, -1)
    seg_slot = (jnp.cumsum(has.astype(jnp.int32)) - 1) % 2
    tile_meta = jnp.stack([tile_e, tile_v, tile_b, tile_f, jnp.take(next_e, tile_e), jnp.take(seg_slot, tile_e)])
    zstart = jnp.concatenate([jnp.where(padded > cnt, ends - TMX, -1), ends[-1:]]).astype(jnp.int32)
    return dest.reshape(t // tmd, 1, 2 * tmd), tile_meta.astype(jnp.int32), zstart, p_rows


def kernel(x, p, rel_bias, attn_norm, w_in, cmp_pe, cmp_w1, cmp_w2, conf_conv_w, conf_conv_b, conf_ln_g, conf_ln_b, sc_conv_w, w_branch, w_out, ffn_norm, router_group_w, router_group_b, router_expert_w, router_expert_b, expert_w_gate, expert_w_up, expert_w_down, ple_norm, ple_gate_w, ple_proj_w, final_norm):
    b, s, d = x.shape
    t = b * s
    depth = w_in.shape[0]
    bf16 = jnp.bfloat16
    ncp = s // CMP_STRIDE
    rows4 = HEADS_PER_GROUP * TQ

    cmp_tab = _bias_tables(rel_bias, 1, s, min(1024, s), ncp, col_mult=CMP_STRIDE, d0_base=-(CMP_BLOCK - 1), d0_step=0,
                           lo=0, hi=1 << 30, out_scale=1.0)
    toe = _bias_tables(rel_bias, 6, TQ, TQ, LANE, col_mult=1, d0_base=-LANE, d0_step=LANE,
                       lo=0, hi=WINDOW, out_scale=LOG2E)
    toe = toe.reshape(KV_HEADS, HEADS_PER_GROUP, 6, TQ, LANE).transpose(0, 2, 1, 3, 4).reshape(KV_HEADS, 6, rows4, LANE)
    win_tab = jnp.stack([toe[:, N_PREV - jj + 1] for jj in range(N_PREV + 2)], axis=1)
    n_sel_tab = SEL_TQ // SEL_TK + 2
    sel_tab = _bias_tables(rel_bias, n_sel_tab, SEL_TQ, SEL_TQ, SEL_TK, col_mult=1,
                           d0_base=(1 - SEL_TQ // SEL_TK) * SEL_TK, d0_step=SEL_TK, lo=0, hi=1 << 30, out_scale=LOG2E)
    sel_tab = sel_tab.reshape(KV_HEADS, HEADS_PER_GROUP, n_sel_tab, SEL_TQ, SEL_TK).transpose(0, 2, 1, 3, 4)
    sel_tab = sel_tab.reshape(KV_HEADS, n_sel_tab, HEADS_PER_GROUP * SEL_TQ, SEL_TK)

    x2d = x.reshape(t, d)
    tm_in = min(1024, t)
    tm = min(512, t)
    tmd = min(256, t)
    for i in range(depth):
        w_z = _regroup_w_in(w_in[i])
        n_pad = ROUTE_LANES - N_EXPERTS - N_GROUPS
        wr = jnp.concatenate([router_expert_w[i], router_group_w[i], jnp.zeros((d, n_pad), jnp.float32)], axis=1)
        wr_hi = wr.astype(bf16)
        wr = jnp.concatenate([wr_hi, (wr - wr_hi.astype(jnp.float32)).astype(bf16)], axis=1)
        br = jnp.concatenate([router_expert_b[i], router_group_b[i], jnp.zeros((n_pad,), jnp.float32)]).reshape(1, -1)

        z = _in_proj(x2d, attn_norm[i].reshape(1, d), w_z, tm_in, 2048)
        kcv = z[:, OFF_KC:OFF_KC + 2 * KV_WIDTH].reshape(b, ncp, CMP_STRIDE, 2 * KV_HEADS, HEAD_DIM)
        kcv = kcv.transpose(0, 3, 2, 1, 4)
        cmp_kv = _compress(kcv, cmp_pe[i], cmp_w1[i].astype(bf16), cmp_w2[i].astype(bf16))
        o_cmp, neg_t = _cmp_attn(z, cmp_kv, cmp_tab, b, s)
        neg = jnp.swapaxes(neg_t, 2, 3)
        o_slc = _flash(z, sel_tab, neg, b, s, selected=True)
        o_win = _flash(z, win_tab, None, b, s, selected=False)
        uconv, o_sc = _conv(z, conf_conv_w[i], conf_conv_b[i].reshape(1, -1), sc_conv_w[i], b, s, min(512, s), 256)
        merged = _merge(z, o_cmp, o_slc, o_win, uconv, conf_ln_g[i].reshape(1, -1), conf_ln_b[i].reshape(1, -1),
                        o_sc, w_branch[i].astype(bf16), tm, 1024)

        x1, h2, route, cnts = _out_router(x2d, merged, w_out[i].astype(bf16), ffn_norm[i].reshape(1, d), wr, br, tm)
        dest3, tile_meta, zstart, p_rows = _route_plan(route, cnts[-1], t, tmd)
        xs = _dispatch(zstart, dest3, h2, p_rows, tmd)
        y = _experts(tile_meta, xs, expert_w_gate, expert_w_up, expert_w_down, i)

        x2d = _combine_ple(dest3, x1, route, y, p[i].reshape(t, PLE_DIM), ple_norm[i].reshape(1, d),
                           ple_gate_w[i].astype(bf16), ple_proj_w[i].astype(bf16), final_norm.reshape(1, d),
                           tmd, i == depth - 1)
    return x2d.reshape(b, s, d)
```

```python
import functools
import math

import jax
import jax.numpy as jnp
from jax import lax
from jax.experimental import pallas as pl
from jax.experimental.pallas import tpu as pltpu

D_MODEL = 2048
N_HEADS = 8
HEAD_DIM = 128
KV_HEADS = 2
HEADS_PER_GROUP = N_HEADS // KV_HEADS
NSA_WIDTH = N_HEADS * HEAD_DIM
KV_WIDTH = KV_HEADS * HEAD_DIM
N_NSA_BRANCHES = 3
CMP_BLOCK = 32
CMP_STRIDE = 16
CMP_HIDDEN = 512
SEL_BLOCK = 64
SEL_TOPN = 16
WINDOW = 512
WIN_QBLOCK = 128
CONF_WIDTH = 1024
CONF_CONV_WIDTH = 31
SC_WIDTH = 1024
SC_CONV_WIDTH = 3
REL_BUCKETS = 32
REL_MAX_DIST = 128
N_GROUPS = 4
EXPERTS_PER_GROUP = 8
N_EXPERTS = N_GROUPS * EXPERTS_PER_GROUP
D_EXPERT = 512
PLE_DIM = 256
EPS = 1e-6
NEG_INF = -1e30
FORCE_SCORE = 1e9
SCALE = HEAD_DIM ** -0.5
LOG2E = math.log2(math.e)

LANE = 128
SUBLANE = 8
VMEM_LIMIT = 56 * 1024 * 1024

OFF_MERGE = 0
OFF_SC = OFF_MERGE + 3 * D_MODEL
OFF_CONF = OFF_SC + 3 * SC_WIDTH
OFF_Q = OFF_CONF + 2 * CONF_WIDTH
OFF_KC = OFF_Q + NSA_WIDTH
OFF_VC = OFF_KC + KV_WIDTH
OFF_KS = OFF_VC + KV_WIDTH
OFF_VS = OFF_KS + KV_WIDTH
OFF_KW = OFF_VS + KV_WIDTH
OFF_VW = OFF_KW + KV_WIDTH
OFF_NSAG = OFF_VW + KV_WIDTH
Z_WIDTH = OFF_NSAG + 512

QW = HEADS_PER_GROUP * HEAD_DIM
TQ = 128
SEL_TQ = 512
SEL_TK = 256
N_PREV = WINDOW // WIN_QBLOCK
ROUTE_LANES = LANE
TMX = 256


def _cparams(sem, vmem=VMEM_LIMIT):
    return pltpu.CompilerParams(dimension_semantics=sem, vmem_limit_bytes=vmem)


def _t5_bucket(dist):
    n = jnp.maximum(dist, 0)
    max_exact = REL_BUCKETS // 2
    nf = jnp.maximum(n, 1).astype(jnp.float32)
    large = max_exact + (jnp.log(nf / max_exact) / math.log(REL_MAX_DIST / max_exact)
                         * (REL_BUCKETS - max_exact)).astype(jnp.int32)
    large = jnp.minimum(large, REL_BUCKETS - 1)
    return jnp.where(n < max_exact, n, large)


def _bias_table_kernel(rel_ref, o_ref, *, col_mult, d0_base, d0_step, lo, hi, rows, cols, out_scale):
    h = pl.program_id(0)
    k = pl.program_id(1)
    rb = pl.program_id(2)
    cb = pl.program_id(3)
    d0 = d0_base + d0_step * k
    d_min = rb * rows - col_mult * (cb * cols + cols - 1) + d0
    d_max = rb * rows + rows - 1 - col_mult * cb * cols + d0
    masked = jnp.logical_or(d_max < lo, d_min >= hi)
    far = jnp.logical_and(d_min >= REL_MAX_DIST, d_max < hi)

    @pl.when(masked)
    def _():
        o_ref[0, 0] = jnp.full((rows, cols), NEG_INF * out_scale, jnp.float32)

    @pl.when(far)
    def _():
        o_ref[0, 0] = jnp.full((rows, cols), rel_ref[REL_BUCKETS - 1, h] * out_scale, jnp.float32)

    @pl.when(jnp.logical_not(jnp.logical_or(masked, far)))
    def _():
        r = lax.broadcasted_iota(jnp.int32, (rows, cols), 0) + rb * rows
        c = lax.broadcasted_iota(jnp.int32, (rows, cols), 1) + cb * cols
        dist = r - col_mult * c + d0
        bucket = _t5_bucket(dist)
        val = jnp.zeros((rows, cols), jnp.float32)
        for b in range(REL_BUCKETS):
            val = jnp.where(bucket == b, rel_ref[b, h], val)
        ok = jnp.where(dist >= lo, jnp.where(dist < hi, 1.0, 0.0), 0.0)
        o_ref[0, 0] = jnp.where(ok > 0.5, val, NEG_INF) * out_scale


def _bias_tables(rel_bias, n_k, n_rows, rows, n_cols, **kw):
    cols = min(LANE, n_cols)
    kern = functools.partial(_bias_table_kernel, rows=rows, cols=cols, **kw)
    return pl.pallas_call(
        kern,
        grid=(N_HEADS, n_k, n_rows // rows, n_cols // cols),
        in_specs=[pl.BlockSpec(memory_space=pltpu.SMEM)],
        out_specs=pl.BlockSpec((1, 1, rows, cols), lambda h, k, r, c: (h, k, r, c)),
        out_shape=jax.ShapeDtypeStruct((N_HEADS, n_k, n_rows, n_cols), jnp.float32),
        compiler_params=_cparams(("parallel", "parallel", "parallel", "parallel")),
        name="bias_tables",
    )(rel_bias)


def _inproj_kernel(x_ref, g_ref, w_ref, o_ref, kcv_ref, h_scr, *, kcv_tile, kcv_col):
    @pl.when(pl.program_id(1) == 0)
    def _():
        x = x_ref[...]
        var = jnp.mean(x * x, axis=-1, keepdims=True)
        h_scr[...] = (x * lax.rsqrt(var + EPS) * g_ref[...]).astype(jnp.bfloat16)

    acc = jnp.dot(h_scr[...], w_ref[0], preferred_element_type=jnp.float32)
    o_ref[...] = acc.astype(o_ref.dtype)

    @pl.when(pl.program_id(1) == kcv_tile)
    def _():
        kcv_ref[...] = acc[:, kcv_col:kcv_col + 2 * KV_WIDTH]


def _in_proj(x, g, w, layer, tm, tn):
    t = x.shape[0]
    assert OFF_KC % tn + 2 * KV_WIDTH <= tn
    return pl.pallas_call(
        functools.partial(_inproj_kernel, kcv_tile=OFF_KC // tn, kcv_col=OFF_KC % tn),
        grid=(t // tm, Z_WIDTH // tn),
        in_specs=[pl.BlockSpec((tm, D_MODEL), lambda i, j: (i, 0)),
                  pl.BlockSpec((1, D_MODEL), lambda i, j: (0, 0)),
                  pl.BlockSpec((1, D_MODEL, tn), lambda i, j: (layer, 0, j))],
        out_specs=[pl.BlockSpec((tm, tn), lambda i, j: (i, j)),
                   pl.BlockSpec((tm, 2 * KV_WIDTH), lambda i, j: (i, 0))],
        out_shape=[jax.ShapeDtypeStruct((t, Z_WIDTH), jnp.bfloat16),
                   jax.ShapeDtypeStruct((t, 2 * KV_WIDTH), jnp.float32)],
        scratch_shapes=[pltpu.VMEM((tm, D_MODEL), jnp.bfloat16)],
        compiler_params=_cparams(("parallel", "arbitrary")),
        name="in_proj",
    )(x, g, w)


def _compress_kernel(a_ref, pe_ref, w1_ref, w2_ref, o_ref, *, ncp):
    lo = jnp.zeros((ncp, CMP_HIDDEN), jnp.float32)
    hi = jnp.zeros((ncp, CMP_HIDDEN), jnp.float32)
    for l in range(CMP_STRIDE):
        a = a_ref[pl.ds(l, ncp, stride=CMP_STRIDE), :]
        a_lo = (a + pe_ref[0, l:l + 1, :]).astype(jnp.bfloat16)
        a_hi = (a + pe_ref[0, CMP_STRIDE + l:CMP_STRIDE + l + 1, :]).astype(jnp.bfloat16)
        lo = lo + jnp.dot(a_lo, w1_ref[0, l * HEAD_DIM:(l + 1) * HEAD_DIM, :],
                          preferred_element_type=jnp.float32)
        hi = hi + jnp.dot(a_hi, w1_ref[0, (CMP_STRIDE + l) * HEAD_DIM:(CMP_STRIDE + l + 1) * HEAD_DIM, :],
                          preferred_element_type=jnp.float32)
    hidden = lo + pltpu.roll(hi, ncp - 1, 0)
    act = jax.nn.gelu(hidden).astype(jnp.bfloat16)
    out = jnp.dot(act, w2_ref[0], preferred_element_type=jnp.float32)
    row = lax.broadcasted_iota(jnp.int32, (ncp, HEAD_DIM), 0)
    o_ref[0, 0] = jnp.where(row < ncp - 1, out, 0.0).astype(o_ref.dtype)


def _compress(a, pe, w1, w2, b, s):
    ncp = s // CMP_STRIDE
    return pl.pallas_call(
        functools.partial(_compress_kernel, ncp=ncp),
        grid=(b, 2 * KV_HEADS),
        in_specs=[pl.BlockSpec((s, HEAD_DIM), lambda i, j: (i, j)),
                  pl.BlockSpec((1, CMP_BLOCK, HEAD_DIM), lambda i, j: (j // KV_HEADS, 0, 0)),
                  pl.BlockSpec((1, CMP_BLOCK * HEAD_DIM, CMP_HIDDEN), lambda i, j: (j // KV_HEADS, 0, 0)),
                  pl.BlockSpec((1, CMP_HIDDEN, HEAD_DIM), lambda i, j: (j // KV_HEADS, 0, 0))],
        out_specs=pl.BlockSpec((1, 1, ncp, HEAD_DIM), lambda i, j: (i, j, 0, 0)),
        out_shape=jax.ShapeDtypeStruct((b, 2 * KV_HEADS, ncp, HEAD_DIM), jnp.bfloat16),
        compiler_params=_cparams(("parallel", "parallel")),
        name="compress",
    )(a, pe, w1, w2)


def _cmp_attn_kernel(q_ref, kc_ref, vc_ref, tab_ref, o_ref, neg_ref, *, tq, ncp, nsel):
    i = pl.program_id(2)
    kc = kc_ref[0, 0]
    vc = vc_ref[0, 0]
    q4 = jnp.concatenate([q_ref[:, h * HEAD_DIM:(h + 1) * HEAD_DIM] for h in range(HEADS_PER_GROUP)], axis=0)
    tab = jnp.concatenate([tab_ref[h, 0] for h in range(HEADS_PER_GROUP)], axis=0)
    s = lax.dot_general(q4, kc, (((1,), (1,)), ((), ())), preferred_element_type=jnp.float32) * SCALE + tab
    valid = tab > 0.5 * NEG_INF
    m = jnp.max(s, axis=-1, keepdims=True)
    e = jnp.where(valid, jnp.exp(s - m), 0.0)
    den = jnp.sum(e, axis=-1, keepdims=True)
    p = jnp.where(valid, e / den, 0.0)
    o = jnp.dot(p.astype(jnp.bfloat16), vc, preferred_element_type=jnp.float32)
    psum = p[0:tq]
    for h in range(HEADS_PER_GROUP):
        o_ref[:, h * HEAD_DIM:(h + 1) * HEAD_DIM] = o[h * tq:(h + 1) * tq].astype(o_ref.dtype)
        if h:
            psum = psum + p[h * tq:(h + 1) * tq]
    blk_n = lax.broadcasted_iota(jnp.int32, (nsel, ncp), 0)
    c_start = lax.broadcasted_iota(jnp.int32, (nsel, ncp), 1) * CMP_STRIDE
    ov = jnp.where(c_start < (blk_n + 1) * SEL_BLOCK,
                   jnp.where(c_start + CMP_BLOCK > blk_n * SEL_BLOCK, 1.0, 0.0), 0.0)
    imp = lax.dot_general(ov, psum, (((1,), (1,)), ((), ())), precision=lax.Precision.HIGHEST,
                          preferred_element_type=jnp.float32)
    blk = lax.broadcasted_iota(jnp.int32, (nsel, tq), 0)
    t = lax.broadcasted_iota(jnp.int32, (nsel, tq), 1) + i * tq
    cur = lax.shift_right_logical(t, 6)
    forced = jnp.where(blk == 0, 1.0, jnp.where(blk == cur, 1.0, jnp.where(blk == cur - 1, 1.0, 0.0)))
    causal = blk * SEL_BLOCK <= t
    score = jnp.where(forced > 0.5, FORCE_SCORE, jnp.where(causal, imp, NEG_INF))
    rank = jnp.zeros((nsel, tq), jnp.float32)
    for k in range(nsel):
        row = score[k:k + 1, :]
        rank = rank + jnp.where(blk > k, jnp.where(row >= score, 1.0, 0.0), jnp.where(row > score, 1.0, 0.0))
    n_top = min(SEL_TOPN, nsel)
    neg = jnp.where(rank < n_top, jnp.where(causal, 0.0, NEG_INF), NEG_INF)
    neg_ref[0, 0] = neg.astype(neg_ref.dtype)


def _cmp_attn(z, cmp_kv, tab, b, s):
    ncp = s // CMP_STRIDE
    nsel = s // SEL_BLOCK
    nq = s // TQ
    kern = functools.partial(_cmp_attn_kernel, tq=TQ, ncp=ncp, nsel=nsel)
    return pl.pallas_call(
        kern,
        grid=(b, KV_HEADS, nq),
        in_specs=[pl.BlockSpec((TQ, QW), lambda bi, g, i: (bi * nq + i, OFF_Q // QW + g)),
                  pl.BlockSpec((1, 1, ncp, HEAD_DIM), lambda bi, g, i: (bi, g, 0, 0)),
                  pl.BlockSpec((1, 1, ncp, HEAD_DIM), lambda bi, g, i: (bi, KV_HEADS + g, 0, 0)),
                  pl.BlockSpec((HEADS_PER_GROUP, 1, TQ, ncp), lambda bi, g, i: (g, 0, i, 0))],
        out_specs=[pl.BlockSpec((TQ, QW), lambda bi, g, i: (bi * nq + i, g)),
                   pl.BlockSpec((1, 1, nsel, TQ), lambda bi, g, i: (bi, g, 0, i))],
        out_shape=[jax.ShapeDtypeStruct((b * s, NSA_WIDTH), jnp.bfloat16),
                   jax.ShapeDtypeStruct((b, KV_HEADS, nsel, s), jnp.bfloat16)],
        compiler_params=_cparams(("parallel", "parallel", "parallel")),
        name="cmp_attn",
    )(z, cmp_kv, cmp_kv, tab)


def _stack_heads(q_ref, g):
    q = jnp.concatenate(
        [q_ref[:, (g * HEADS_PER_GROUP + h) * HEAD_DIM:(g * HEADS_PER_GROUP + h + 1) * HEAD_DIM]
         for h in range(HEADS_PER_GROUP)], axis=0)
    return (q.astype(jnp.float32) * (SCALE * LOG2E)).astype(jnp.bfloat16)


def _unstack_heads(o_ref, g, out, tq):
    for h in range(HEADS_PER_GROUP):
        c0 = (g * HEADS_PER_GROUP + h) * HEAD_DIM
        o_ref[:, c0:c0 + HEAD_DIM] = out[h * tq:(h + 1) * tq].astype(o_ref.dtype)


def _sel_kernel(q_ref, k_ref, v_ref, tab_ref, neg_ref, o_ref, m0_scr, m1_scr, acc0_scr, acc1_scr, *, tq, tk, nsel):
    i = pl.program_id(1)
    sub = tk // LANE
    pad = jnp.zeros((HEADS_PER_GROUP * tq, HEAD_DIM - nsel), jnp.bfloat16)
    q4 = [jnp.concatenate([_stack_heads(q_ref, g), jnp.concatenate([neg_ref[0, g]] * HEADS_PER_GROUP, axis=0), pad],
                          axis=1) for g in range(KV_HEADS)]
    kpad = jnp.zeros((tk, HEAD_DIM - nsel), jnp.bfloat16)
    m_scr = (m0_scr, m1_scr)
    acc_scr = (acc0_scr, acc1_scr)
    for g in range(KV_HEADS):
        m_scr[g][...] = jnp.full(m_scr[g].shape, NEG_INF, jnp.float32)
        acc_scr[g][...] = jnp.zeros(acc_scr[g].shape, jnp.float32)
    u_min = 1 - tq // tk
    n_far = tab_ref.shape[1] - 1
    ones = jnp.ones((tk, LANE), jnp.bfloat16)

    def body(j, carry):
        tidx = jnp.minimum(i * (tq // tk) - j - u_min, n_far)
        k_i = lax.broadcasted_iota(jnp.int32, (tk, nsel), 0)
        n_i = lax.broadcasted_iota(jnp.int32, (tk, nsel), 1)
        onehot = jnp.where(lax.shift_right_logical(k_i + j * tk, 6) == n_i, 1.0, 0.0).astype(jnp.bfloat16)
        start = pl.multiple_of(j * tk, tk)
        for g in range(KV_HEADS):
            gs = slice(g * HEAD_DIM, (g + 1) * HEAD_DIM)
            kt = jnp.concatenate([k_ref[pl.ds(start, tk), gs], onehot, kpad], axis=1)
            vt = jnp.concatenate([v_ref[pl.ds(start, tk), gs], ones], axis=1)
            s = lax.dot_general(q4[g], kt, (((1,), (1,)), ((), ())), preferred_element_type=jnp.float32)
            s = s + tab_ref[g, tidx]
            m_prev = m_scr[g][...]
            m_new = jnp.maximum(m_prev, jnp.max(s, axis=-1, keepdims=True))
            alpha = jnp.exp2(m_prev - m_new)
            p = jnp.exp2(s - jnp.concatenate([m_new] * sub, axis=1))
            acc_scr[g][...] = (jnp.concatenate([alpha, alpha], axis=1) * acc_scr[g][...]
                               + jnp.dot(p.astype(jnp.bfloat16), vt, preferred_element_type=jnp.float32))
            m_scr[g][...] = m_new
        return carry

    lax.fori_loop(0, (i + 1) * (tq // tk), body, 0)
    for g in range(KV_HEADS):
        _unstack_heads(o_ref, g, acc_scr[g][:, 0:HEAD_DIM] / acc_scr[g][:, HEAD_DIM:2 * HEAD_DIM], tq)


def _win_kernel(q_ref, k_ref, v_ref, tab_ref, o_ref, *, tq):
    i = pl.program_id(1)
    nb = N_PREV + 1
    sb = jnp.maximum(i - N_PREV, 0)
    off = sb - i + N_PREV
    start = pl.multiple_of(sb * WIN_QBLOCK, WIN_QBLOCK)
    ones = jnp.ones((nb * WIN_QBLOCK, LANE), jnp.bfloat16)
    for g in range(KV_HEADS):
        gs = slice(g * HEAD_DIM, (g + 1) * HEAD_DIM)
        kt = k_ref[pl.ds(start, nb * WIN_QBLOCK), gs]
        vt = jnp.concatenate([v_ref[pl.ds(start, nb * WIN_QBLOCK), gs], ones], axis=1)
        s = lax.dot_general(_stack_heads(q_ref, g), kt, (((1,), (1,)), ((), ())),
                            preferred_element_type=jnp.float32)
        s = s + jnp.concatenate([tab_ref[g, jnp.minimum(u + off, nb)] for u in range(nb)], axis=1)
        p = jnp.exp2(s - jnp.max(s, axis=-1, keepdims=True))
        pv = jnp.dot(p.astype(jnp.bfloat16), vt, preferred_element_type=jnp.float32)
        _unstack_heads(o_ref, g, pv[:, 0:HEAD_DIM] / pv[:, HEAD_DIM:2 * HEAD_DIM], tq)


def _flash(z, tab, neg, b, s, *, selected):
    tq = SEL_TQ if selected else TQ
    nq = s // tq
    tk = SEL_TK if selected else WIN_QBLOCK
    k_off, v_off = (OFF_KS, OFF_VS) if selected else (OFF_KW, OFF_VW)
    rows = HEADS_PER_GROUP * tq
    n_tab = tab.shape[1]
    in_specs = [pl.BlockSpec((tq, NSA_WIDTH), lambda bi, i: (bi * nq + i, OFF_Q // NSA_WIDTH)),
                pl.BlockSpec((s, KV_WIDTH), lambda bi, i: (bi, k_off // KV_WIDTH)),
                pl.BlockSpec((s, KV_WIDTH), lambda bi, i: (bi, v_off // KV_WIDTH)),
                pl.BlockSpec((KV_HEADS, n_tab, rows, tk), lambda bi, i: (0, 0, 0, 0),
                             pipeline_mode=pl.Buffered(1))]
    args = [z, z, z, tab]
    scratch = []
    if selected:
        nsel = s // SEL_BLOCK
        in_specs.append(pl.BlockSpec((1, KV_HEADS, tq, nsel), lambda bi, i: (bi, 0, i, 0)))
        args.append(neg)
        kern = functools.partial(_sel_kernel, tq=tq, tk=tk, nsel=nsel)
        scratch = ([pltpu.VMEM((rows, LANE), jnp.float32)] * KV_HEADS
                   + [pltpu.VMEM((rows, 2 * HEAD_DIM), jnp.float32)] * KV_HEADS)
    else:
        kern = functools.partial(_win_kernel, tq=tq)
    return pl.pallas_call(
        kern,
        grid=(b, nq),
        in_specs=in_specs,
        out_specs=pl.BlockSpec((tq, NSA_WIDTH), lambda bi, i: (bi * nq + i, 0)),
        out_shape=jax.ShapeDtypeStruct((b * s, NSA_WIDTH), jnp.bfloat16),
        scratch_shapes=scratch,
        compiler_params=_cparams(("parallel", "parallel")),
        name="flash_sel" if selected else "flash_win",
    )(*args)


HALO = 32
CONV_RC = 64


def _conv_kernel(a_ref, g_ref, ah_ref, gh_ref, bg_ref, cg_ref, xs_ref, cgh_ref, xsh_ref,
                 cw_ref, cb_ref, sw_ref, uo_ref, so_ref, ext_scr, ext2_scr, *, ts):
    first = pl.program_id(1) == 0
    f32 = jnp.float32
    n_ext = HALO + ts
    n_sh = n_ext - SUBLANE
    u = a_ref[...].astype(f32) * jax.nn.sigmoid(g_ref[...].astype(f32))
    uh = ah_ref[...].astype(f32) * jax.nn.sigmoid(gh_ref[...].astype(f32))
    ext_scr[0, 0:HALO, :] = jnp.where(first, 0.0, uh)
    ext_scr[0, HALO:n_ext, :] = u
    v = cg_ref[...].astype(f32) * xs_ref[...].astype(f32)
    vh = cgh_ref[...].astype(f32) * xsh_ref[...].astype(f32)
    ext2_scr[0, 0:HALO, :] = jnp.where(first, 0.0, vh)
    ext2_scr[0, HALO:n_ext, :] = v
    base = HALO - (CONF_CONV_WIDTH - 1)
    base2 = HALO - (SC_CONV_WIDTH - 1)
    for r in range(1, SUBLANE):
        ext_scr[r, 0:n_sh, :] = ext_scr[0, r:r + n_sh, :]
    sc_shifts = sorted({(base2 + k) % SUBLANE for k in range(SC_CONV_WIDTH)} - {0})
    for r in sc_shifts:
        ext2_scr[r, 0:n_sh, :] = ext2_scr[0, r:r + n_sh, :]

    def tap(scr, off, r0):
        r = off % SUBLANE
        return scr[r, r0 + off - r:r0 + off - r + CONV_RC, :]

    for r0 in range(0, ts, CONV_RC):
        acc = jnp.zeros((CONV_RC, a_ref.shape[1]), f32) + cb_ref[...]
        for k in range(CONF_CONV_WIDTH):
            acc = acc + cw_ref[k:k + 1, :] * tap(ext_scr, base + k, r0)
        uo_ref[r0:r0 + CONV_RC, :] = acc.astype(uo_ref.dtype)
        acc2 = jnp.zeros((CONV_RC, a_ref.shape[1]), f32)
        for k in range(SC_CONV_WIDTH):
            acc2 = acc2 + sw_ref[k:k + 1, :] * tap(ext2_scr, base2 + k, r0)
        so_ref[r0:r0 + CONV_RC, :] = (bg_ref[r0:r0 + CONV_RC, :].astype(f32) * acc2).astype(so_ref.dtype)


def _conv(z, cw, cb, sw, b, s, ts, tc):
    ns = s // ts
    t = b * s

    def cur(off):
        return pl.BlockSpec((ts, tc), lambda bi, i, c: (bi * ns + i, off // tc + c))

    def halo(off):
        return pl.BlockSpec((HALO, tc), lambda bi, i, c: (jnp.maximum((bi * s + i * ts) // HALO - 1, 0), off // tc + c))

    return pl.pallas_call(
        functools.partial(_conv_kernel, ts=ts),
        grid=(b, ns, CONF_WIDTH // tc),
        in_specs=[cur(OFF_CONF), cur(OFF_CONF + CONF_WIDTH), halo(OFF_CONF), halo(OFF_CONF + CONF_WIDTH),
                  cur(OFF_SC), cur(OFF_SC + SC_WIDTH), cur(OFF_SC + 2 * SC_WIDTH),
                  halo(OFF_SC + SC_WIDTH), halo(OFF_SC + 2 * SC_WIDTH),
                  pl.BlockSpec((CONF_CONV_WIDTH, tc), lambda bi, i, c: (0, c)),
                  pl.BlockSpec((1, tc), lambda bi, i, c: (0, c)),
                  pl.BlockSpec((SC_CONV_WIDTH, tc), lambda bi, i, c: (0, c))],
        out_specs=[pl.BlockSpec((ts, tc), lambda bi, i, c: (bi * ns + i, c)),
                   pl.BlockSpec((ts, tc), lambda bi, i, c: (bi * ns + i, c))],
        out_shape=[jax.ShapeDtypeStruct((t, CONF_WIDTH), jnp.bfloat16),
                   jax.ShapeDtypeStruct((t, SC_WIDTH), jnp.bfloat16)],
        scratch_shapes=[pltpu.VMEM((SUBLANE, HALO + ts, tc), jnp.float32),
                        pltpu.VMEM((SUBLANE, HALO + ts, tc), jnp.float32)],
        compiler_params=_cparams(("parallel", "parallel", "parallel")),
        name="conv",
    )(z, z, z, z, z, z, z, z, z, cw, cb, sw)


def _merge_kernel(ocmp_ref, oslc_ref, owin_ref, ng_ref, uc_ref, lng_ref, lnb_ref, osc_ref,
                  mg0_ref, mg1_ref, mg2_ref, wn_ref, wc_ref, ws_ref, o_ref, nsa_scr, conf_scr):
    f32 = jnp.float32

    @pl.when(pl.program_id(1) == 0)
    def _():
        gt = jax.nn.sigmoid(ng_ref[...].astype(f32))
        for h in range(N_HEADS):
            sl = slice(h * HEAD_DIM, (h + 1) * HEAD_DIM)
            c = N_NSA_BRANCHES * h
            o = (gt[:, c:c + 1] * ocmp_ref[:, sl].astype(f32) + gt[:, c + 1:c + 2] * oslc_ref[:, sl].astype(f32)
                 + gt[:, c + 2:c + 3] * owin_ref[:, sl].astype(f32))
            nsa_scr[:, sl] = o.astype(nsa_scr.dtype)
        u = uc_ref[...].astype(f32)
        mu = jnp.mean(u, axis=-1, keepdims=True)
        var = jnp.mean(jnp.square(u - mu), axis=-1, keepdims=True)
        y = (u - mu) * lax.rsqrt(var + EPS) * lng_ref[...] + lnb_ref[...]
        conf_scr[...] = (y * jax.nn.sigmoid(y)).astype(conf_scr.dtype)

    a = jnp.dot(nsa_scr[...], wn_ref[...], preferred_element_type=f32)
    b = jnp.dot(conf_scr[...], wc_ref[...], preferred_element_type=f32)
    c = jnp.dot(osc_ref[...], ws_ref[...], preferred_element_type=f32)
    m = (jax.nn.sigmoid(mg0_ref[...].astype(f32)) * a + jax.nn.sigmoid(mg1_ref[...].astype(f32)) * b
         + jax.nn.sigmoid(mg2_ref[...].astype(f32)) * c)
    o_ref[...] = m.astype(o_ref.dtype)


def _merge(z, ocmp, oslc, owin, uconv, lng, lnb, osc, wb, tm, tn):
    t = z.shape[0]
    nj = D_MODEL // tn
    row = lambda w: pl.BlockSpec((tm, w), lambda i, j: (i, 0))
    return pl.pallas_call(
        _merge_kernel,
        grid=(t // tm, nj),
        in_specs=[row(NSA_WIDTH), row(NSA_WIDTH), row(NSA_WIDTH),
                  pl.BlockSpec((tm, LANE), lambda i, j: (i, OFF_NSAG // LANE)),
                  row(CONF_WIDTH),
                  pl.BlockSpec((1, CONF_WIDTH), lambda i, j: (0, 0)),
                  pl.BlockSpec((1, CONF_WIDTH), lambda i, j: (0, 0)),
                  row(SC_WIDTH),
                  pl.BlockSpec((tm, tn), lambda i, j: (i, j)),
                  pl.BlockSpec((tm, tn), lambda i, j: (i, nj + j)),
                  pl.BlockSpec((tm, tn), lambda i, j: (i, 2 * nj + j)),
                  pl.BlockSpec((NSA_WIDTH, tn), lambda i, j: (0, j)),
                  pl.BlockSpec((CONF_WIDTH, tn), lambda i, j: (NSA_WIDTH // CONF_WIDTH, j)),
                  pl.BlockSpec((SC_WIDTH, tn), lambda i, j: ((NSA_WIDTH + CONF_WIDTH) // SC_WIDTH, j))],
        out_specs=pl.BlockSpec((tm, tn), lambda i, j: (i, j)),
        out_shape=jax.ShapeDtypeStruct((t, D_MODEL), jnp.bfloat16),
        scratch_shapes=[pltpu.VMEM((tm, NSA_WIDTH), jnp.bfloat16), pltpu.VMEM((tm, CONF_WIDTH), jnp.bfloat16)],
        compiler_params=_cparams(("parallel", "arbitrary")),
        name="merge",
    )(ocmp, oslc, owin, z, uconv, lng, lnb, osc, z, z, z, wb, wb, wb)


R_E1, R_E2, R_W1, R_W2, R_RANK1, R_RANK2 = range(6)
GROUP_LANE0 = N_EXPERTS


def _out_router_kernel(x_ref, m_ref, wo_ref, fn_ref, wr_ref, br_ref, x1_ref, h2_ref, route_ref, cnt_ref,
                       base_scr, *, tm):
    f32 = jnp.float32

    @pl.when(pl.program_id(0) == 0)
    def _():
        base_scr[...] = jnp.zeros_like(base_scr)

    x1 = x_ref[...] + jnp.dot(m_ref[...], wo_ref[...], preferred_element_type=f32)
    x1_ref[...] = x1
    var = jnp.mean(x1 * x1, axis=-1, keepdims=True)
    h2 = x1 * lax.rsqrt(var + EPS) * fn_ref[...]
    h2_ref[...] = _pack_bf16_pairs(h2)
    h_hi = h2.astype(jnp.bfloat16)
    h_lo = (h2 - h_hi.astype(f32)).astype(jnp.bfloat16)
    l_hi = jnp.dot(h_hi, wr_ref[...], preferred_element_type=f32)
    l_lo = jnp.dot(h_lo, wr_ref[:, 0:ROUTE_LANES], preferred_element_type=f32)
    logits = l_hi[:, 0:ROUTE_LANES] + l_hi[:, ROUTE_LANES:2 * ROUTE_LANES] + l_lo + br_ref[...]
    lane = lax.broadcasted_iota(jnp.int32, (tm, ROUTE_LANES), 1).astype(f32)
    big = float(ROUTE_LANES)
    is_g = jnp.where(lane >= GROUP_LANE0, jnp.where(lane < GROUP_LANE0 + N_GROUPS, 1.0, 0.0), 0.0) > 0.5
    gl = jnp.where(is_g, logits, NEG_INF)
    gmax = jnp.max(gl, axis=-1, keepdims=True)
    glane = jnp.min(jnp.where(gl == gmax, lane, big), axis=-1, keepdims=True)
    gsum = jnp.sum(jnp.where(is_g, jnp.exp(gl - gmax), 0.0), axis=-1, keepdims=True)
    g_w = 1.0 / gsum
    grp = glane - GROUP_LANE0
    in_grp = jnp.floor(lane * (1.0 / EXPERTS_PER_GROUP)) == grp
    el = jnp.where(in_grp, logits, NEG_INF)
    emax = jnp.max(el, axis=-1, keepdims=True)
    ee = jnp.where(in_grp, jnp.exp(el - emax), 0.0)
    ep = ee / jnp.sum(ee, axis=-1, keepdims=True)
    ep = jnp.where(in_grp, ep, -1.0)
    p1 = jnp.max(ep, axis=-1, keepdims=True)
    i1 = jnp.min(jnp.where(ep == p1, lane, big), axis=-1, keepdims=True)
    ep2 = jnp.where(lane == i1, -1.0, ep)
    p2 = jnp.max(ep2, axis=-1, keepdims=True)
    i2 = jnp.min(jnp.where(ep2 == p2, lane, big), axis=-1, keepdims=True)
    psum = p1 + p2
    w1 = g_w * (p1 / psum)
    w2 = g_w * (p2 / psum)
    onehot = jnp.where(lane == i1, 1.0, jnp.where(lane == i2, 1.0, 0.0))
    r_i = lax.broadcasted_iota(jnp.int32, (tm, tm), 0)
    c_i = lax.broadcasted_iota(jnp.int32, (tm, tm), 1)
    tri = jnp.where(c_i < r_i, 1.0, 0.0).astype(jnp.bfloat16)
    cum = jnp.dot(tri, onehot.astype(jnp.bfloat16), preferred_element_type=f32) + base_scr[0:1, :]
    rank1 = jnp.sum(jnp.where(lane == i1, cum, 0.0), axis=-1, keepdims=True)
    rank2 = jnp.sum(jnp.where(lane == i2, cum, 0.0), axis=-1, keepdims=True)
    new_base = base_scr[0:1, :] + jnp.sum(onehot, axis=0, keepdims=True)
    base_scr[...] = jnp.broadcast_to(new_base, base_scr.shape)
    cnt_ref[...] = jnp.broadcast_to(new_base, cnt_ref.shape)
    rec = jnp.zeros((tm, ROUTE_LANES), f32)
    for ln, val in ((R_E1, i1), (R_E2, i2), (R_W1, w1), (R_W2, w2), (R_RANK1, rank1), (R_RANK2, rank2)):
        rec = jnp.where(lane == ln, val, rec)
    route_ref[...] = rec


def _out_router(x, merged, wo, fn, wr, br, tm):
    t = x.shape[0]
    n = t // tm
    return pl.pallas_call(
        functools.partial(_out_router_kernel, tm=tm),
        grid=(n,),
        in_specs=[pl.BlockSpec((tm, D_MODEL), lambda i: (i, 0)),
                  pl.BlockSpec((tm, D_MODEL), lambda i: (i, 0)),
                  pl.BlockSpec((D_MODEL, D_MODEL), lambda i: (0, 0)),
                  pl.BlockSpec((1, D_MODEL), lambda i: (0, 0)),
                  pl.BlockSpec((D_MODEL, 2 * ROUTE_LANES), lambda i: (0, 0)),
                  pl.BlockSpec((1, ROUTE_LANES), lambda i: (0, 0))],
        out_specs=[pl.BlockSpec((tm, D_MODEL), lambda i: (i, 0)),
                   pl.BlockSpec((tm, D_MODEL // 2), lambda i: (i, 0)),
                   pl.BlockSpec((tm, ROUTE_LANES), lambda i: (i, 0)),
                   pl.BlockSpec((8, ROUTE_LANES), lambda i: (i, 0))],
        out_shape=[jax.ShapeDtypeStruct((t, D_MODEL), jnp.float32),
                   jax.ShapeDtypeStruct((t, D_MODEL // 2), jnp.uint32),
                   jax.ShapeDtypeStruct((t, ROUTE_LANES), jnp.float32),
                   jax.ShapeDtypeStruct((n * 8, ROUTE_LANES), jnp.float32)],
        scratch_shapes=[pltpu.VMEM((8, ROUTE_LANES), jnp.float32)],
        compiler_params=_cparams(("arbitrary",)),
        name="out_router",
    )(x, merged, wo, fn, wr, br)


HALF = D_MODEL // 2
DMA_UNROLL = 8


def _pack_bf16_pairs(x):
    lo = pltpu.bitcast(x[:, 0:HALF].astype(jnp.bfloat16).astype(jnp.float32), jnp.uint32)
    hi = pltpu.bitcast(x[:, HALF:D_MODEL].astype(jnp.bfloat16).astype(jnp.float32), jnp.uint32)
    return hi | (lo >> 16)


def _unpack_bf16_pairs(w):
    lo = pltpu.bitcast(w << 16, jnp.float32)
    hi = pltpu.bitcast(w & jnp.uint32(0xFFFF0000), jnp.float32)
    return lo, hi


def _dispatch_kernel(zs_ref, dest_ref, h_ref, xs_ref, zero_scr, sem, zsem, *, tmd):
    @pl.when(pl.program_id(0) == 0)
    def _():
        zero_scr[...] = jnp.zeros_like(zero_scr)
        for e in range(N_EXPERTS):
            @pl.when(zs_ref[e] >= 0)
            def _():
                cp = pltpu.make_async_copy(zero_scr, xs_ref.at[pl.ds(pl.multiple_of(zs_ref[e], TMX), TMX)], zsem)
                cp.start()
                cp.wait()

        def zero_tail(tile, c):
            cp = pltpu.make_async_copy(zero_scr, xs_ref.at[pl.ds(pl.multiple_of(tile * TMX, TMX), TMX)], zsem)
            cp.start()
            cp.wait()
            return c

        lax.fori_loop(zs_ref[N_EXPERTS] // TMX, xs_ref.shape[0] // TMX, zero_tail, 0)

    def row_copy(r, k):
        return pltpu.make_async_copy(h_ref.at[pl.ds(r, 1)], xs_ref.at[pl.ds(dest_ref[0, 0, k * tmd + r], 1)], sem)

    def issue(r, c):
        row_copy(r, 0).start(priority=0)
        row_copy(r, 1).start(priority=1)
        return c

    lax.fori_loop(0, tmd, issue, 0, unroll=DMA_UNROLL)

    def drain(r, c):
        row_copy(r, 0).wait()
        row_copy(r, 1).wait()
        return c

    lax.fori_loop(0, tmd, drain, 0, unroll=DMA_UNROLL)


def _dispatch(zstart, dest3, h2p, p_rows, tmd):
    t = h2p.shape[0]
    grid_spec = pltpu.PrefetchScalarGridSpec(
        num_scalar_prefetch=1,
        grid=(t // tmd,),
        in_specs=[pl.BlockSpec((1, 1, 2 * tmd), lambda i, zs: (i, 0, 0), memory_space=pltpu.SMEM),
                  pl.BlockSpec((tmd, HALF), lambda i, zs: (i, 0))],
        out_specs=pl.BlockSpec(memory_space=pl.ANY),
        scratch_shapes=[pltpu.VMEM((TMX, HALF), jnp.uint32), pltpu.SemaphoreType.DMA(()),
                        pltpu.SemaphoreType.DMA(())],
    )
    return pl.pallas_call(
        functools.partial(_dispatch_kernel, tmd=tmd),
        grid_spec=grid_spec,
        out_shape=jax.ShapeDtypeStruct((p_rows, HALF), jnp.uint32),
        compiler_params=_cparams(("arbitrary",)),
        name="dispatch",
    )(zstart, dest3, h2p)


TM_EXPERT, TM_VALID, TM_BLOCK, TM_FIRST, TM_NEXT, TM_SLOT = range(6)


def _expert_kernel(tm_ref, xs_ref, wg_hbm, wu_hbm, wd_hbm, y_ref, wg_buf, wu_buf, wd_buf, wgu_scr, wd_scr, sems,
                   *, layer):
    i = pl.program_id(0)
    e = tm_ref[TM_EXPERT, i]
    slot = tm_ref[TM_SLOT, i]

    def weight_copies(expert, s):
        return (pltpu.make_async_copy(wg_hbm.at[layer, expert], wg_buf.at[s], sems.at[s, 0]),
                pltpu.make_async_copy(wu_hbm.at[layer, expert], wu_buf.at[s], sems.at[s, 1]),
                pltpu.make_async_copy(wd_hbm.at[layer, expert], wd_buf.at[s], sems.at[s, 2]))

    @pl.when(tm_ref[TM_FIRST, i] == 1)
    def _():
        @pl.when(i == 0)
        def _():
            for cp in weight_copies(e, slot):
                cp.start()

        for cp in weight_copies(e, slot):
            cp.wait()

        @pl.when(tm_ref[TM_NEXT, i] >= 0)
        def _():
            for cp in weight_copies(tm_ref[TM_NEXT, i], 1 - slot):
                cp.start()

        wgu_scr[:, 0:D_EXPERT] = wg_buf[slot].astype(jnp.bfloat16)
        wgu_scr[:, D_EXPERT:2 * D_EXPERT] = wu_buf[slot].astype(jnp.bfloat16)
        wd_scr[...] = wd_buf[slot].astype(jnp.bfloat16)

    @pl.when(tm_ref[TM_VALID, i] == 1)
    def _():
        x_lo, x_hi = _unpack_bf16_pairs(xs_ref[...])
        gu = (jnp.dot(x_lo.astype(jnp.bfloat16), wgu_scr[0:HALF, :], preferred_element_type=jnp.float32)
              + jnp.dot(x_hi.astype(jnp.bfloat16), wgu_scr[HALF:D_MODEL, :], preferred_element_type=jnp.float32))
        gate = gu[:, 0:D_EXPERT]
        he = (gate * jax.nn.sigmoid(gate)) * gu[:, D_EXPERT:2 * D_EXPERT]
        y = jnp.dot(he.astype(jnp.bfloat16), wd_scr[...], preferred_element_type=jnp.float32)
        y_ref[...] = _pack_bf16_pairs(y)

    @pl.when(tm_ref[TM_VALID, i] == 0)
    def _():
        y_ref[...] = jnp.zeros_like(y_ref)


def _experts(tile_meta, xs, wg, wu, wd, layer):
    p_rows = xs.shape[0]
    f32 = jnp.float32
    grid_spec = pltpu.PrefetchScalarGridSpec(
        num_scalar_prefetch=1,
        grid=(p_rows // TMX,),
        in_specs=[pl.BlockSpec((TMX, HALF), lambda i, tm: (tm[TM_BLOCK, i], 0)),
                  pl.BlockSpec(memory_space=pl.ANY),
                  pl.BlockSpec(memory_space=pl.ANY),
                  pl.BlockSpec(memory_space=pl.ANY)],
        out_specs=pl.BlockSpec((TMX, HALF), lambda i, tm: (i, 0)),
        scratch_shapes=[pltpu.VMEM((2, D_MODEL, D_EXPERT), f32), pltpu.VMEM((2, D_MODEL, D_EXPERT), f32),
                        pltpu.VMEM((2, D_EXPERT, D_MODEL), f32),
                        pltpu.VMEM((D_MODEL, 2 * D_EXPERT), jnp.bfloat16),
                        pltpu.VMEM((D_EXPERT, D_MODEL), jnp.bfloat16),
                        pltpu.SemaphoreType.DMA((2, 3))],
    )
    return pl.pallas_call(
        functools.partial(_expert_kernel, layer=layer),
        grid_spec=grid_spec,
        out_shape=jax.ShapeDtypeStruct((p_rows, HALF), jnp.uint32),
        compiler_params=_cparams(("arbitrary",)),
        name="experts",
    )(tile_meta, xs, wg, wu, wd)


def _combine_ple_kernel(dcur_ref, dnext_ref, x1_ref, route_ref, y_ref, p_ref, pn_ref, wpg_ref, wpp_ref, fn_ref,
                        o_ref, buf, sems, *, tmc, final):
    f32 = jnp.float32
    i = pl.program_id(0)
    slot = lax.rem(i, 2)

    def row_copy(d_ref, s, r, k):
        return pltpu.make_async_copy(y_ref.at[pl.ds(d_ref[0, 0, k * tmc + r], 1)], buf.at[s, k, pl.ds(r, 1)],
                                     sems.at[s])

    def issue(d_ref, s):
        def body(r, c):
            row_copy(d_ref, s, r, 0).start(priority=0)
            row_copy(d_ref, s, r, 1).start(priority=1)
            return c
        lax.fori_loop(0, tmc, body, 0, unroll=DMA_UNROLL)

    @pl.when(i == 0)
    def _():
        issue(dcur_ref, 0)

    @pl.when(i + 1 < pl.num_programs(0))
    def _():
        issue(dnext_ref, 1 - slot)

    def drain(r, c):
        row_copy(dcur_ref, slot, r, 0).wait()
        row_copy(dcur_ref, slot, r, 1).wait()
        return c

    lax.fori_loop(0, tmc, drain, 0, unroll=DMA_UNROLL)

    route = route_ref[...]
    w1 = route[:, R_W1:R_W1 + 1]
    w2 = route[:, R_W2:R_W2 + 1]
    y1_lo, y1_hi = _unpack_bf16_pairs(buf[slot, 0])
    y2_lo, y2_hi = _unpack_bf16_pairs(buf[slot, 1])
    x2 = x1_ref[...] + jnp.concatenate([w1 * y1_lo + w2 * y2_lo, w1 * y1_hi + w2 * y2_hi], axis=1)
    var = jnp.mean(x2 * x2, axis=-1, keepdims=True)
    hp = (x2 * lax.rsqrt(var + EPS) * pn_ref[...]).astype(jnp.bfloat16)
    gate = jax.nn.sigmoid(jnp.dot(hp, wpg_ref[...], preferred_element_type=f32))
    pp = jnp.dot(p_ref[...].astype(jnp.bfloat16), wpp_ref[...], preferred_element_type=f32)
    x3 = x2 + gate * pp
    if final:
        var3 = jnp.mean(x3 * x3, axis=-1, keepdims=True)
        x3 = x3 * lax.rsqrt(var3 + EPS) * fn_ref[...]
    o_ref[...] = x3


def _combine_ple(dest3, x1, route, y, p, pn, wpg, wpp, fn, tmc, final):
    t = x1.shape[0]
    n = t // tmc
    return pl.pallas_call(
        functools.partial(_combine_ple_kernel, tmc=tmc, final=final),
        grid=(n,),
        in_specs=[pl.BlockSpec((1, 1, 2 * tmc), lambda i: (i, 0, 0), memory_space=pltpu.SMEM),
                  pl.BlockSpec((1, 1, 2 * tmc), lambda i: (jnp.minimum(i + 1, n - 1), 0, 0), memory_space=pltpu.SMEM),
                  pl.BlockSpec((tmc, D_MODEL), lambda i: (i, 0)),
                  pl.BlockSpec((tmc, ROUTE_LANES), lambda i: (i, 0)),
                  pl.BlockSpec(memory_space=pl.ANY),
                  pl.BlockSpec((tmc, PLE_DIM), lambda i: (i, 0)),
                  pl.BlockSpec((1, D_MODEL), lambda i: (0, 0)),
                  pl.BlockSpec((D_MODEL, D_MODEL), lambda i: (0, 0)),
                  pl.BlockSpec((PLE_DIM, D_MODEL), lambda i: (0, 0)),
                  pl.BlockSpec((1, D_MODEL), lambda i: (0, 0))],
        out_specs=pl.BlockSpec((tmc, D_MODEL), lambda i: (i, 0)),
        out_shape=jax.ShapeDtypeStruct((t, D_MODEL), jnp.float32),
        scratch_shapes=[pltpu.VMEM((2, 2, tmc, HALF), jnp.uint32), pltpu.SemaphoreType.DMA((2,))],
        compiler_params=_cparams(("arbitrary",)),
        name="combine_ple",
    )(dest3, dest3, x1, route, y, p, pn, wpg, wpp, fn)


def _regroup_w_in(w):
    o_ng = NSA_WIDTH + 6 * KV_WIDTH
    o_conf = o_ng + N_HEADS * N_NSA_BRANCHES
    o_sc = o_conf + 2 * CONF_WIDTH
    o_mg = o_sc + 3 * SC_WIDTH
    out = jnp.zeros(w.shape[:2] + (Z_WIDTH,), jnp.bfloat16)
    for lo, hi, dst in ((o_mg, w.shape[2], OFF_MERGE), (o_sc, o_mg, OFF_SC), (o_conf, o_sc, OFF_CONF),
                        (0, o_ng, OFF_Q), (o_ng, o_conf, OFF_NSAG)):
        out = lax.dynamic_update_slice(out, w[:, :, lo:hi].astype(jnp.bfloat16), (0, 0, dst))
    return out


def _route_plan(route, counts, t, tmd):
    cnt = counts[:N_EXPERTS].astype(jnp.int32)
    padded = ((cnt + TMX - 1) // TMX) * TMX
    ends = jnp.cumsum(padded)
    starts = ends - padded
    e_idx = jnp.arange(N_EXPERTS, dtype=jnp.int32)

    def lookup(table, idx):
        return jnp.sum(jnp.where(idx[..., None] == e_idx, table, 0), axis=-1)

    dest = [lookup(starts, route[:, R_E1 + k].astype(jnp.int32).reshape(t // tmd, tmd))
            + route[:, R_RANK1 + k].astype(jnp.int32).reshape(t // tmd, tmd) for k in range(2)]
    dest3 = jnp.stack(dest, axis=1).reshape(t // tmd, 1, 2 * tmd)
    p_rows = 2 * t + N_EXPERTS * TMX
    n_tiles = p_rows // TMX
    tile_start = jnp.arange(n_tiles, dtype=jnp.int32) * TMX
    tile_e = jnp.minimum(jnp.sum((tile_start[:, None] >= ends[None, :]).astype(jnp.int32), axis=1), N_EXPERTS - 1)
    tile_v = (tile_start < ends[-1]).astype(jnp.int32)
    tile_b = jnp.where(tile_v == 1, jnp.arange(n_tiles, dtype=jnp.int32), 0)
    tile_f = tile_v * jnp.concatenate([jnp.ones((1,), jnp.int32), (tile_e[1:] != tile_e[:-1]).astype(jnp.int32)])
    has = padded > 0
    later = jnp.where(has[None, :] & (e_idx[None, :] > e_idx[:, None]), e_idx[None, :], N_EXPERTS)
    next_e = jnp.min(later, axis=1)
    next_e = jnp.where(next_e < N_EXPERTS, next_e, -1)
    seg_slot = (jnp.cumsum(has.astype(jnp.int32)) - 1) % 2
    tile_meta = jnp.stack([tile_e, tile_v, tile_b, tile_f, lookup(next_e, tile_e), lookup(seg_slot, tile_e)])
    zstart = jnp.concatenate([jnp.where(padded > cnt, ends - TMX, -1), ends[-1:]]).astype(jnp.int32)
    return dest3, tile_meta.astype(jnp.int32), zstart, p_rows


def kernel(x, p, rel_bias, attn_norm, w_in, cmp_pe, cmp_w1, cmp_w2, conf_conv_w, conf_conv_b, conf_ln_g, conf_ln_b, sc_conv_w, w_branch, w_out, ffn_norm, router_group_w, router_group_b, router_expert_w, router_expert_b, expert_w_gate, expert_w_up, expert_w_down, ple_norm, ple_gate_w, ple_proj_w, final_norm):
    b, s, d = x.shape
    t = b * s
    depth = w_in.shape[0]
    bf16 = jnp.bfloat16
    ncp = s // CMP_STRIDE
    rows4 = HEADS_PER_GROUP * TQ

    cmp_tab = _bias_tables(rel_bias, 1, s, min(1024, s), ncp, col_mult=CMP_STRIDE, d0_base=-(CMP_BLOCK - 1), d0_step=0,
                           lo=0, hi=1 << 30, out_scale=1.0)
    toe = _bias_tables(rel_bias, 6, TQ, TQ, LANE, col_mult=1, d0_base=-LANE, d0_step=LANE,
                       lo=0, hi=WINDOW, out_scale=LOG2E)
    toe = toe.reshape(KV_HEADS, HEADS_PER_GROUP, 6, TQ, LANE).transpose(0, 2, 1, 3, 4).reshape(KV_HEADS, 6, rows4, LANE)
    win_tab = jnp.stack([toe[:, N_PREV - jj + 1] for jj in range(N_PREV + 2)], axis=1)
    n_sel_tab = SEL_TQ // SEL_TK + 2
    sel_tab = _bias_tables(rel_bias, n_sel_tab, SEL_TQ, SEL_TQ, SEL_TK, col_mult=1,
                           d0_base=(1 - SEL_TQ // SEL_TK) * SEL_TK, d0_step=SEL_TK, lo=0, hi=1 << 30, out_scale=LOG2E)
    sel_tab = sel_tab.reshape(KV_HEADS, HEADS_PER_GROUP, n_sel_tab, SEL_TQ, SEL_TK).transpose(0, 2, 1, 3, 4)
    sel_tab = sel_tab.reshape(KV_HEADS, n_sel_tab, HEADS_PER_GROUP * SEL_TQ, SEL_TK)

    w_z = _regroup_w_in(w_in)
    x2d = x.reshape(t, d)
    tm_in = min(1024, t)
    tm = min(512, t)
    tmd = min(256, t)
    for i in range(depth):
        n_pad = ROUTE_LANES - N_EXPERTS - N_GROUPS
        wr = jnp.concatenate([router_expert_w[i], router_group_w[i], jnp.zeros((d, n_pad), jnp.float32)], axis=1)
        wr_hi = wr.astype(bf16)
        wr = jnp.concatenate([wr_hi, (wr - wr_hi.astype(jnp.float32)).astype(bf16)], axis=1)
        br = jnp.concatenate([router_expert_b[i], router_group_b[i], jnp.zeros((n_pad,), jnp.float32)]).reshape(1, -1)

        z, kcv = _in_proj(x2d, attn_norm[i].reshape(1, d), w_z, i, tm_in, 2048)
        cmp_kv = _compress(kcv, cmp_pe[i], cmp_w1[i].astype(bf16), cmp_w2[i].astype(bf16), b, s)
        o_cmp, neg_t = _cmp_attn(z, cmp_kv, cmp_tab, b, s)
        neg = jnp.swapaxes(neg_t, 2, 3)
        o_slc = _flash(z, sel_tab, neg, b, s, selected=True)
        o_win = _flash(z, win_tab, None, b, s, selected=False)
        uconv, o_sc = _conv(z, conf_conv_w[i], conf_conv_b[i].reshape(1, -1), sc_conv_w[i], b, s, min(512, s), 256)
        merged = _merge(z, o_cmp, o_slc, o_win, uconv, conf_ln_g[i].reshape(1, -1), conf_ln_b[i].reshape(1, -1),
                        o_sc, w_branch[i].astype(bf16), tm, 1024)

        x1, h2, route, cnts = _out_router(x2d, merged, w_out[i].astype(bf16), ffn_norm[i].reshape(1, d), wr, br, tm)
        dest3, tile_meta, zstart, p_rows = _route_plan(route, cnts[-1], t, tmd)
        xs = _dispatch(zstart, dest3, h2, p_rows, tmd)
        y = _experts(tile_meta, xs, expert_w_gate, expert_w_up, expert_w_down, i)

        x2d = _combine_ple(dest3, x1, route, y, p[i].reshape(t, PLE_DIM), ple_norm[i].reshape(1, d),
                           ple_gate_w[i].astype(bf16), ple_proj_w[i].astype(bf16), final_norm.reshape(1, d),
                           tmd, i == depth - 1)
    return x2d.reshape(b, s, d)
```

```python
import functools
import math

import jax
import jax.numpy as jnp
from jax import lax
from jax.experimental import pallas as pl
from jax.experimental.pallas import tpu as pltpu

D_MODEL = 2048
N_HEADS = 8
HEAD_DIM = 128
KV_HEADS = 2
HEADS_PER_GROUP = N_HEADS // KV_HEADS
NSA_WIDTH = N_HEADS * HEAD_DIM
KV_WIDTH = KV_HEADS * HEAD_DIM
N_NSA_BRANCHES = 3
CMP_BLOCK = 32
CMP_STRIDE = 16
CMP_HIDDEN = 512
SEL_BLOCK = 64
SEL_TOPN = 16
WINDOW = 512
WIN_QBLOCK = 128
CONF_WIDTH = 1024
CONF_CONV_WIDTH = 31
SC_WIDTH = 1024
SC_CONV_WIDTH = 3
REL_BUCKETS = 32
REL_MAX_DIST = 128
N_GROUPS = 4
EXPERTS_PER_GROUP = 8
N_EXPERTS = N_GROUPS * EXPERTS_PER_GROUP
D_EXPERT = 512
PLE_DIM = 256
EPS = 1e-6
NEG_INF = -1e30
FORCE_SCORE = 1e9
SCALE = HEAD_DIM ** -0.5
LOG2E = math.log2(math.e)

LANE = 128
SUBLANE = 8
VMEM_LIMIT = 56 * 1024 * 1024

OFF_MERGE = 0
OFF_SC = OFF_MERGE + 3 * D_MODEL
OFF_CONF = OFF_SC + 3 * SC_WIDTH
OFF_Q = OFF_CONF + 2 * CONF_WIDTH
OFF_KC = OFF_Q + NSA_WIDTH
OFF_VC = OFF_KC + KV_WIDTH
OFF_KS = OFF_VC + KV_WIDTH
OFF_VS = OFF_KS + KV_WIDTH
OFF_KW = OFF_VS + KV_WIDTH
OFF_VW = OFF_KW + KV_WIDTH
OFF_NSAG = OFF_VW + KV_WIDTH
Z_WIDTH = OFF_NSAG + 512

QW = HEADS_PER_GROUP * HEAD_DIM
TQ = 128
SEL_TQ = 512
SEL_TK = 256
N_PREV = WINDOW // WIN_QBLOCK
ROUTE_LANES = LANE
TMX = 256


def _cparams(sem, vmem=VMEM_LIMIT):
    return pltpu.CompilerParams(dimension_semantics=sem, vmem_limit_bytes=vmem)


def _t5_bucket(dist):
    n = jnp.maximum(dist, 0)
    max_exact = REL_BUCKETS // 2
    nf = jnp.maximum(n, 1).astype(jnp.float32)
    large = max_exact + (jnp.log(nf / max_exact) / math.log(REL_MAX_DIST / max_exact)
                         * (REL_BUCKETS - max_exact)).astype(jnp.int32)
    large = jnp.minimum(large, REL_BUCKETS - 1)
    return jnp.where(n < max_exact, n, large)


def _bias_table_kernel(rel_ref, o_ref, *, col_mult, d0_base, d0_step, lo, hi, rows, cols, out_scale):
    h = pl.program_id(0)
    k = pl.program_id(1)
    rb = pl.program_id(2)
    cb = pl.program_id(3)
    d0 = d0_base + d0_step * k
    d_min = rb * rows - col_mult * (cb * cols + cols - 1) + d0
    d_max = rb * rows + rows - 1 - col_mult * cb * cols + d0
    masked = jnp.logical_or(d_max < lo, d_min >= hi)
    far = jnp.logical_and(d_min >= REL_MAX_DIST, d_max < hi)

    @pl.when(masked)
    def _():
        o_ref[0, 0] = jnp.full((rows, cols), NEG_INF * out_scale, jnp.float32)

    @pl.when(far)
    def _():
        o_ref[0, 0] = jnp.full((rows, cols), rel_ref[REL_BUCKETS - 1, h] * out_scale, jnp.float32)

    @pl.when(jnp.logical_not(jnp.logical_or(masked, far)))
    def _():
        r = lax.broadcasted_iota(jnp.int32, (rows, cols), 0) + rb * rows
        c = lax.broadcasted_iota(jnp.int32, (rows, cols), 1) + cb * cols
        dist = r - col_mult * c + d0
        bucket = _t5_bucket(dist)
        val = jnp.zeros((rows, cols), jnp.float32)
        for b in range(REL_BUCKETS):
            val = jnp.where(bucket == b, rel_ref[b, h], val)
        ok = jnp.where(dist >= lo, jnp.where(dist < hi, 1.0, 0.0), 0.0)
        o_ref[0, 0] = jnp.where(ok > 0.5, val, NEG_INF) * out_scale


def _bias_tables(rel_bias, n_k, n_rows, rows, n_cols, **kw):
    cols = min(LANE, n_cols)
    kern = functools.partial(_bias_table_kernel, rows=rows, cols=cols, **kw)
    return pl.pallas_call(
        kern,
        grid=(N_HEADS, n_k, n_rows // rows, n_cols // cols),
        in_specs=[pl.BlockSpec(memory_space=pltpu.SMEM)],
        out_specs=pl.BlockSpec((1, 1, rows, cols), lambda h, k, r, c: (h, k, r, c)),
        out_shape=jax.ShapeDtypeStruct((N_HEADS, n_k, n_rows, n_cols), jnp.float32),
        compiler_params=_cparams(("parallel", "parallel", "parallel", "parallel")),
        name="bias_tables",
    )(rel_bias)


def _inproj_kernel(x_ref, g_ref, w_ref, o_ref, kcv_ref, h_scr, *, kcv_tile, kcv_col):
    @pl.when(pl.program_id(1) == 0)
    def _():
        x = x_ref[...]
        var = jnp.mean(x * x, axis=-1, keepdims=True)
        h_scr[...] = (x * lax.rsqrt(var + EPS) * g_ref[...]).astype(jnp.bfloat16)

    acc = jnp.dot(h_scr[...], w_ref[0], preferred_element_type=jnp.float32)
    o_ref[...] = acc.astype(o_ref.dtype)

    @pl.when(pl.program_id(1) == kcv_tile)
    def _():
        kcv_ref[...] = acc[:, kcv_col:kcv_col + 2 * KV_WIDTH]


def _in_proj(x, g, w, layer, tm, tn):
    t = x.shape[0]
    assert OFF_KC % tn + 2 * KV_WIDTH <= tn
    return pl.pallas_call(
        functools.partial(_inproj_kernel, kcv_tile=OFF_KC // tn, kcv_col=OFF_KC % tn),
        grid=(t // tm, Z_WIDTH // tn),
        in_specs=[pl.BlockSpec((tm, D_MODEL), lambda i, j: (i, 0)),
                  pl.BlockSpec((1, D_MODEL), lambda i, j: (0, 0)),
                  pl.BlockSpec((1, D_MODEL, tn), lambda i, j: (layer, 0, j))],
        out_specs=[pl.BlockSpec((tm, tn), lambda i, j: (i, j)),
                   pl.BlockSpec((tm, 2 * KV_WIDTH), lambda i, j: (i, 0))],
        out_shape=[jax.ShapeDtypeStruct((t, Z_WIDTH), jnp.bfloat16),
                   jax.ShapeDtypeStruct((t, 2 * KV_WIDTH), jnp.float32)],
        scratch_shapes=[pltpu.VMEM((tm, D_MODEL), jnp.bfloat16)],
        compiler_params=_cparams(("parallel", "arbitrary")),
        name="in_proj",
    )(x, g, w)


def _compress_kernel(a_ref, pe_ref, w1_ref, w2_ref, o_ref, *, ncp):
    lo = jnp.zeros((ncp, CMP_HIDDEN), jnp.float32)
    hi = jnp.zeros((ncp, CMP_HIDDEN), jnp.float32)
    for l in range(CMP_STRIDE):
        a = a_ref[pl.ds(l, ncp, stride=CMP_STRIDE), :]
        a_lo = (a + pe_ref[0, l:l + 1, :]).astype(jnp.bfloat16)
        a_hi = (a + pe_ref[0, CMP_STRIDE + l:CMP_STRIDE + l + 1, :]).astype(jnp.bfloat16)
        lo = lo + jnp.dot(a_lo, w1_ref[0, l * HEAD_DIM:(l + 1) * HEAD_DIM, :],
                          preferred_element_type=jnp.float32)
        hi = hi + jnp.dot(a_hi, w1_ref[0, (CMP_STRIDE + l) * HEAD_DIM:(CMP_STRIDE + l + 1) * HEAD_DIM, :],
                          preferred_element_type=jnp.float32)
    hidden = lo + pltpu.roll(hi, ncp - 1, 0)
    act = jax.nn.gelu(hidden).astype(jnp.bfloat16)
    out = jnp.dot(act, w2_ref[0], preferred_element_type=jnp.float32)
    row = lax.broadcasted_iota(jnp.int32, (ncp, HEAD_DIM), 0)
    o_ref[0, 0] = jnp.where(row < ncp - 1, out, 0.0).astype(o_ref.dtype)


def _compress(a, pe, w1, w2, b, s):
    ncp = s // CMP_STRIDE
    return pl.pallas_call(
        functools.partial(_compress_kernel, ncp=ncp),
        grid=(b, 2 * KV_HEADS),
        in_specs=[pl.BlockSpec((s, HEAD_DIM), lambda i, j: (i, j)),
                  pl.BlockSpec((1, CMP_BLOCK, HEAD_DIM), lambda i, j: (j // KV_HEADS, 0, 0)),
                  pl.BlockSpec((1, CMP_BLOCK * HEAD_DIM, CMP_HIDDEN), lambda i, j: (j // KV_HEADS, 0, 0)),
                  pl.BlockSpec((1, CMP_HIDDEN, HEAD_DIM), lambda i, j: (j // KV_HEADS, 0, 0))],
        out_specs=pl.BlockSpec((1, 1, ncp, HEAD_DIM), lambda i, j: (i, j, 0, 0)),
        out_shape=jax.ShapeDtypeStruct((b, 2 * KV_HEADS, ncp, HEAD_DIM), jnp.bfloat16),
        compiler_params=_cparams(("parallel", "parallel")),
        name="compress",
    )(a, pe, w1, w2)


def _cmp_attn_kernel(q_ref, kc_ref, vc_ref, tab_ref, o_ref, neg_ref, *, tq, ncp, nsel):
    i = pl.program_id(2)
    kc = kc_ref[0, 0]
    vc = vc_ref[0, 0]
    q4 = jnp.concatenate([q_ref[:, h * HEAD_DIM:(h + 1) * HEAD_DIM] for h in range(HEADS_PER_GROUP)], axis=0)
    tab = jnp.concatenate([tab_ref[h, 0] for h in range(HEADS_PER_GROUP)], axis=0)
    s = lax.dot_general(q4, kc, (((1,), (1,)), ((), ())), preferred_element_type=jnp.float32) * SCALE + tab
    valid = tab > 0.5 * NEG_INF
    m = jnp.max(s, axis=-1, keepdims=True)
    e = jnp.where(valid, jnp.exp(s - m), 0.0)
    den = jnp.sum(e, axis=-1, keepdims=True)
    p = jnp.where(valid, e / den, 0.0)
    o = jnp.dot(p.astype(jnp.bfloat16), vc, preferred_element_type=jnp.float32)
    psum = p[0:tq]
    for h in range(HEADS_PER_GROUP):
        o_ref[:, h * HEAD_DIM:(h + 1) * HEAD_DIM] = o[h * tq:(h + 1) * tq].astype(o_ref.dtype)
        if h:
            psum = psum + p[h * tq:(h + 1) * tq]
    blk_n = lax.broadcasted_iota(jnp.int32, (nsel, ncp), 0)
    c_start = lax.broadcasted_iota(jnp.int32, (nsel, ncp), 1) * CMP_STRIDE
    ov = jnp.where(c_start < (blk_n + 1) * SEL_BLOCK,
                   jnp.where(c_start + CMP_BLOCK > blk_n * SEL_BLOCK, 1.0, 0.0), 0.0)
    imp = lax.dot_general(ov, psum, (((1,), (1,)), ((), ())), precision=lax.Precision.HIGHEST,
                          preferred_element_type=jnp.float32)
    blk = lax.broadcasted_iota(jnp.int32, (nsel, tq), 0)
    t = lax.broadcasted_iota(jnp.int32, (nsel, tq), 1) + i * tq
    cur = lax.shift_right_logical(t, 6)
    forced = jnp.where(blk == 0, 1.0, jnp.where(blk == cur, 1.0, jnp.where(blk == cur - 1, 1.0, 0.0)))
    causal = blk * SEL_BLOCK <= t
    score = jnp.where(forced > 0.5, FORCE_SCORE, jnp.where(causal, imp, NEG_INF))
    rank = jnp.zeros((nsel, tq), jnp.float32)
    for k in range(nsel):
        row = score[k:k + 1, :]
        rank = rank + jnp.where(blk > k, jnp.where(row >= score, 1.0, 0.0), jnp.where(row > score, 1.0, 0.0))
    n_top = min(SEL_TOPN, nsel)
    neg = jnp.where(rank < n_top, jnp.where(causal, 0.0, NEG_INF), NEG_INF)
    neg_ref[0, 0] = neg.astype(neg_ref.dtype)


def _cmp_attn(z, cmp_kv, tab, b, s):
    ncp = s // CMP_STRIDE
    nsel = s // SEL_BLOCK
    nq = s // TQ
    kern = functools.partial(_cmp_attn_kernel, tq=TQ, ncp=ncp, nsel=nsel)
    return pl.pallas_call(
        kern,
        grid=(b, KV_HEADS, nq),
        in_specs=[pl.BlockSpec((TQ, QW), lambda bi, g, i: (bi * nq + i, OFF_Q // QW + g)),
                  pl.BlockSpec((1, 1, ncp, HEAD_DIM), lambda bi, g, i: (bi, g, 0, 0)),
                  pl.BlockSpec((1, 1, ncp, HEAD_DIM), lambda bi, g, i: (bi, KV_HEADS + g, 0, 0)),
                  pl.BlockSpec((HEADS_PER_GROUP, 1, TQ, ncp), lambda bi, g, i: (g, 0, i, 0))],
        out_specs=[pl.BlockSpec((TQ, QW), lambda bi, g, i: (bi * nq + i, g)),
                   pl.BlockSpec((1, 1, nsel, TQ), lambda bi, g, i: (bi, g, 0, i))],
        out_shape=[jax.ShapeDtypeStruct((b * s, NSA_WIDTH), jnp.bfloat16),
                   jax.ShapeDtypeStruct((b, KV_HEADS, nsel, s), jnp.bfloat16)],
        compiler_params=_cparams(("parallel", "parallel", "parallel")),
        name="cmp_attn",
    )(z, cmp_kv, cmp_kv, tab)


def _stack_heads(q_ref, g):
    q = jnp.concatenate(
        [q_ref[:, (g * HEADS_PER_GROUP + h) * HEAD_DIM:(g * HEADS_PER_GROUP + h + 1) * HEAD_DIM]
         for h in range(HEADS_PER_GROUP)], axis=0)
    return (q.astype(jnp.float32) * (SCALE * LOG2E)).astype(jnp.bfloat16)


def _unstack_heads(o_ref, g, out, tq):
    for h in range(HEADS_PER_GROUP):
        c0 = (g * HEADS_PER_GROUP + h) * HEAD_DIM
        o_ref[:, c0:c0 + HEAD_DIM] = out[h * tq:(h + 1) * tq].astype(o_ref.dtype)


def _sel_kernel(q_ref, k_ref, v_ref, tab_ref, neg_ref, o_ref, m0_scr, m1_scr, acc0_scr, acc1_scr, *, tq, tk, nsel):
    i = pl.program_id(1)
    sub = tk // LANE
    pad = jnp.zeros((HEADS_PER_GROUP * tq, HEAD_DIM - nsel), jnp.bfloat16)
    q4 = [jnp.concatenate([_stack_heads(q_ref, g), jnp.concatenate([neg_ref[0, g]] * HEADS_PER_GROUP, axis=0), pad],
                          axis=1) for g in range(KV_HEADS)]
    kpad = jnp.zeros((tk, HEAD_DIM - nsel), jnp.bfloat16)
    m_scr = (m0_scr, m1_scr)
    acc_scr = (acc0_scr, acc1_scr)
    for g in range(KV_HEADS):
        m_scr[g][...] = jnp.full(m_scr[g].shape, NEG_INF, jnp.float32)
        acc_scr[g][...] = jnp.zeros(acc_scr[g].shape, jnp.float32)
    u_min = 1 - tq // tk
    n_far = tab_ref.shape[1] - 1
    ones = jnp.ones((tk, LANE), jnp.bfloat16)

    def key_tile(j):
        tidx = jnp.minimum(i * (tq // tk) - j - u_min, n_far)
        k_i = lax.broadcasted_iota(jnp.int32, (tk, nsel), 0)
        n_i = lax.broadcasted_iota(jnp.int32, (tk, nsel), 1)
        onehot = jnp.where(lax.shift_right_logical(k_i + j * tk, 6) == n_i, 1.0, 0.0).astype(jnp.bfloat16)
        start = pl.multiple_of(j * tk, tk)
        for g in range(KV_HEADS):
            gs = slice(g * HEAD_DIM, (g + 1) * HEAD_DIM)
            kt = jnp.concatenate([k_ref[pl.ds(start, tk), gs], onehot, kpad], axis=1)
            vt = jnp.concatenate([v_ref[pl.ds(start, tk), gs], ones], axis=1)
            s = lax.dot_general(q4[g], kt, (((1,), (1,)), ((), ())), preferred_element_type=jnp.float32)
            s = s + tab_ref[g, tidx]
            m_prev = m_scr[g][...]
            m_new = jnp.maximum(m_prev, jnp.max(s, axis=-1, keepdims=True))
            alpha = jnp.exp2(m_prev - m_new)
            p = jnp.exp2(s - jnp.concatenate([m_new] * sub, axis=1))
            acc_scr[g][...] = (jnp.concatenate([alpha, alpha], axis=1) * acc_scr[g][...]
                               + jnp.dot(p.astype(jnp.bfloat16), vt, preferred_element_type=jnp.float32))
            m_scr[g][...] = m_new

    per_trip = tq // tk

    def body(jj, carry):
        for jo in range(per_trip):
            key_tile(jj * per_trip + jo)
        return carry

    lax.fori_loop(0, i + 1, body, 0)
    for g in range(KV_HEADS):
        _unstack_heads(o_ref, g, acc_scr[g][:, 0:HEAD_DIM] / acc_scr[g][:, HEAD_DIM:2 * HEAD_DIM], tq)


def _win_kernel(q_ref, k_ref, v_ref, tab_ref, o_ref, *, tq):
    i = pl.program_id(1)
    nb = N_PREV + 1
    sb = jnp.maximum(i - N_PREV, 0)
    off = sb - i + N_PREV
    start = pl.multiple_of(sb * WIN_QBLOCK, WIN_QBLOCK)
    ones = jnp.ones((nb * WIN_QBLOCK, LANE), jnp.bfloat16)
    for g in range(KV_HEADS):
        gs = slice(g * HEAD_DIM, (g + 1) * HEAD_DIM)
        kt = k_ref[pl.ds(start, nb * WIN_QBLOCK), gs]
        vt = jnp.concatenate([v_ref[pl.ds(start, nb * WIN_QBLOCK), gs], ones], axis=1)
        s = lax.dot_general(_stack_heads(q_ref, g), kt, (((1,), (1,)), ((), ())),
                            preferred_element_type=jnp.float32)
        s = s + jnp.concatenate([tab_ref[g, jnp.minimum(u + off, nb)] for u in range(nb)], axis=1)
        p = jnp.exp2(s - jnp.max(s, axis=-1, keepdims=True))
        pv = jnp.dot(p.astype(jnp.bfloat16), vt, preferred_element_type=jnp.float32)
        _unstack_heads(o_ref, g, pv[:, 0:HEAD_DIM] / pv[:, HEAD_DIM:2 * HEAD_DIM], tq)


def _flash(z, tab, neg, b, s, *, selected):
    tq = SEL_TQ if selected else TQ
    nq = s // tq
    tk = SEL_TK if selected else WIN_QBLOCK
    k_off, v_off = (OFF_KS, OFF_VS) if selected else (OFF_KW, OFF_VW)
    rows = HEADS_PER_GROUP * tq
    n_tab = tab.shape[1]
    in_specs = [pl.BlockSpec((tq, NSA_WIDTH), lambda bi, i: (bi * nq + i, OFF_Q // NSA_WIDTH)),
                pl.BlockSpec((s, KV_WIDTH), lambda bi, i: (bi, k_off // KV_WIDTH)),
                pl.BlockSpec((s, KV_WIDTH), lambda bi, i: (bi, v_off // KV_WIDTH)),
                pl.BlockSpec((KV_HEADS, n_tab, rows, tk), lambda bi, i: (0, 0, 0, 0),
                             pipeline_mode=pl.Buffered(1))]
    args = [z, z, z, tab]
    scratch = []
    if selected:
        nsel = s // SEL_BLOCK
        in_specs.append(pl.BlockSpec((1, KV_HEADS, tq, nsel), lambda bi, i: (bi, 0, i, 0)))
        args.append(neg)
        kern = functools.partial(_sel_kernel, tq=tq, tk=tk, nsel=nsel)
        scratch = ([pltpu.VMEM((rows, LANE), jnp.float32)] * KV_HEADS
                   + [pltpu.VMEM((rows, 2 * HEAD_DIM), jnp.float32)] * KV_HEADS)
    else:
        kern = functools.partial(_win_kernel, tq=tq)
    return pl.pallas_call(
        kern,
        grid=(b, nq),
        in_specs=in_specs,
        out_specs=pl.BlockSpec((tq, NSA_WIDTH), lambda bi, i: (bi * nq + i, 0)),
        out_shape=jax.ShapeDtypeStruct((b * s, NSA_WIDTH), jnp.bfloat16),
        scratch_shapes=scratch,
        compiler_params=_cparams(("parallel", "parallel")),
        name="flash_sel" if selected else "flash_win",
    )(*args)


HALO = 32
CONV_RC = 64


def _conv_kernel(a_ref, g_ref, ah_ref, gh_ref, bg_ref, cg_ref, xs_ref, cgh_ref, xsh_ref,
                 cw_ref, cb_ref, sw_ref, uo_ref, so_ref, ext_scr, ext2_scr, *, ts):
    first = pl.program_id(1) == 0
    f32 = jnp.float32
    n_ext = HALO + ts
    n_sh = n_ext - SUBLANE
    u = a_ref[...].astype(f32) * jax.nn.sigmoid(g_ref[...].astype(f32))
    uh = ah_ref[...].astype(f32) * jax.nn.sigmoid(gh_ref[...].astype(f32))
    ext_scr[0, 0:HALO, :] = jnp.where(first, 0.0, uh)
    ext_scr[0, HALO:n_ext, :] = u
    v = cg_ref[...].astype(f32) * xs_ref[...].astype(f32)
    vh = cgh_ref[...].astype(f32) * xsh_ref[...].astype(f32)
    ext2_scr[0, 0:HALO, :] = jnp.where(first, 0.0, vh)
    ext2_scr[0, HALO:n_ext, :] = v
    base = HALO - (CONF_CONV_WIDTH - 1)
    base2 = HALO - (SC_CONV_WIDTH - 1)
    for r in range(1, SUBLANE):
        ext_scr[r, 0:n_sh, :] = ext_scr[0, r:r + n_sh, :]
    sc_shifts = sorted({(base2 + k) % SUBLANE for k in range(SC_CONV_WIDTH)} - {0})
    for r in sc_shifts:
        ext2_scr[r, 0:n_sh, :] = ext2_scr[0, r:r + n_sh, :]

    def tap(scr, off, r0):
        r = off % SUBLANE
        return scr[r, r0 + off - r:r0 + off - r + CONV_RC, :]

    for r0 in range(0, ts, CONV_RC):
        acc = jnp.zeros((CONV_RC, a_ref.shape[1]), f32) + cb_ref[...]
        for k in range(CONF_CONV_WIDTH):
            acc = acc + cw_ref[k:k + 1, :] * tap(ext_scr, base + k, r0)
        uo_ref[r0:r0 + CONV_RC, :] = acc.astype(uo_ref.dtype)
        acc2 = jnp.zeros((CONV_RC, a_ref.shape[1]), f32)
        for k in range(SC_CONV_WIDTH):
            acc2 = acc2 + sw_ref[k:k + 1, :] * tap(ext2_scr, base2 + k, r0)
        so_ref[r0:r0 + CONV_RC, :] = (bg_ref[r0:r0 + CONV_RC, :].astype(f32) * acc2).astype(so_ref.dtype)


def _conv(z, cw, cb, sw, b, s, ts, tc):
    ns = s // ts
    t = b * s

    def cur(off):
        return pl.BlockSpec((ts, tc), lambda bi, i, c: (bi * ns + i, off // tc + c))

    def halo(off):
        return pl.BlockSpec((HALO, tc), lambda bi, i, c: (jnp.maximum((bi * s + i * ts) // HALO - 1, 0), off // tc + c))

    return pl.pallas_call(
        functools.partial(_conv_kernel, ts=ts),
        grid=(b, ns, CONF_WIDTH // tc),
        in_specs=[cur(OFF_CONF), cur(OFF_CONF + CONF_WIDTH), halo(OFF_CONF), halo(OFF_CONF + CONF_WIDTH),
                  cur(OFF_SC), cur(OFF_SC + SC_WIDTH), cur(OFF_SC + 2 * SC_WIDTH),
                  halo(OFF_SC + SC_WIDTH), halo(OFF_SC + 2 * SC_WIDTH),
                  pl.BlockSpec((CONF_CONV_WIDTH, tc), lambda bi, i, c: (0, c)),
                  pl.BlockSpec((1, tc), lambda bi, i, c: (0, c)),
                  pl.BlockSpec((SC_CONV_WIDTH, tc), lambda bi, i, c: (0, c))],
        out_specs=[pl.BlockSpec((ts, tc), lambda bi, i, c: (bi * ns + i, c)),
                   pl.BlockSpec((ts, tc), lambda bi, i, c: (bi * ns + i, c))],
        out_shape=[jax.ShapeDtypeStruct((t, CONF_WIDTH), jnp.bfloat16),
                   jax.ShapeDtypeStruct((t, SC_WIDTH), jnp.bfloat16)],
        scratch_shapes=[pltpu.VMEM((SUBLANE, HALO + ts, tc), jnp.float32),
                        pltpu.VMEM((SUBLANE, HALO + ts, tc), jnp.float32)],
        compiler_params=_cparams(("parallel", "parallel", "parallel")),
        name="conv",
    )(z, z, z, z, z, z, z, z, z, cw, cb, sw)


def _merge_kernel(ocmp_ref, oslc_ref, owin_ref, ng_ref, uc_ref, lng_ref, lnb_ref, osc_ref,
                  mg0_ref, mg1_ref, mg2_ref, wn_ref, wc_ref, ws_ref, o_ref, nsa_scr, conf_scr):
    f32 = jnp.float32

    @pl.when(pl.program_id(1) == 0)
    def _():
        gt = jax.nn.sigmoid(ng_ref[...].astype(f32))
        for h in range(N_HEADS):
            sl = slice(h * HEAD_DIM, (h + 1) * HEAD_DIM)
            c = N_NSA_BRANCHES * h
            o = (gt[:, c:c + 1] * ocmp_ref[:, sl].astype(f32) + gt[:, c + 1:c + 2] * oslc_ref[:, sl].astype(f32)
                 + gt[:, c + 2:c + 3] * owin_ref[:, sl].astype(f32))
            nsa_scr[:, sl] = o.astype(nsa_scr.dtype)
        u = uc_ref[...].astype(f32)
        mu = jnp.mean(u, axis=-1, keepdims=True)
        var = jnp.mean(jnp.square(u - mu), axis=-1, keepdims=True)
        y = (u - mu) * lax.rsqrt(var + EPS) * lng_ref[...] + lnb_ref[...]
        conf_scr[...] = (y * jax.nn.sigmoid(y)).astype(conf_scr.dtype)

    a = jnp.dot(nsa_scr[...], wn_ref[...], preferred_element_type=f32)
    b = jnp.dot(conf_scr[...], wc_ref[...], preferred_element_type=f32)
    c = jnp.dot(osc_ref[...], ws_ref[...], preferred_element_type=f32)
    m = (jax.nn.sigmoid(mg0_ref[...].astype(f32)) * a + jax.nn.sigmoid(mg1_ref[...].astype(f32)) * b
         + jax.nn.sigmoid(mg2_ref[...].astype(f32)) * c)
    o_ref[...] = m.astype(o_ref.dtype)


def _merge(z, ocmp, oslc, owin, uconv, lng, lnb, osc, wb, tm, tn):
    t = z.shape[0]
    nj = D_MODEL // tn
    row = lambda w: pl.BlockSpec((tm, w), lambda i, j: (i, 0))
    return pl.pallas_call(
        _merge_kernel,
        grid=(t // tm, nj),
        in_specs=[row(NSA_WIDTH), row(NSA_WIDTH), row(NSA_WIDTH),
                  pl.BlockSpec((tm, LANE), lambda i, j: (i, OFF_NSAG // LANE)),
                  row(CONF_WIDTH),
                  pl.BlockSpec((1, CONF_WIDTH), lambda i, j: (0, 0)),
                  pl.BlockSpec((1, CONF_WIDTH), lambda i, j: (0, 0)),
                  row(SC_WIDTH),
                  pl.BlockSpec((tm, tn), lambda i, j: (i, j)),
                  pl.BlockSpec((tm, tn), lambda i, j: (i, nj + j)),
                  pl.BlockSpec((tm, tn), lambda i, j: (i, 2 * nj + j)),
                  pl.BlockSpec((NSA_WIDTH, tn), lambda i, j: (0, j)),
                  pl.BlockSpec((CONF_WIDTH, tn), lambda i, j: (NSA_WIDTH // CONF_WIDTH, j)),
                  pl.BlockSpec((SC_WIDTH, tn), lambda i, j: ((NSA_WIDTH + CONF_WIDTH) // SC_WIDTH, j))],
        out_specs=pl.BlockSpec((tm, tn), lambda i, j: (i, j)),
        out_shape=jax.ShapeDtypeStruct((t, D_MODEL), jnp.bfloat16),
        scratch_shapes=[pltpu.VMEM((tm, NSA_WIDTH), jnp.bfloat16), pltpu.VMEM((tm, CONF_WIDTH), jnp.bfloat16)],
        compiler_params=_cparams(("parallel", "arbitrary")),
        name="merge",
    )(ocmp, oslc, owin, z, uconv, lng, lnb, osc, z, z, z, wb, wb, wb)


R_E1, R_E2, R_W1, R_W2, R_RANK1, R_RANK2 = range(6)
GROUP_LANE0 = N_EXPERTS


def _out_router_kernel(x_ref, m_ref, wo_ref, fn_ref, wr_ref, br_ref, x1_ref, h2_ref, route_ref, cnt_ref,
                       base_scr, *, tm):
    f32 = jnp.float32

    @pl.when(pl.program_id(0) == 0)
    def _():
        base_scr[...] = jnp.zeros_like(base_scr)

    x1 = x_ref[...] + jnp.dot(m_ref[...], wo_ref[...], preferred_element_type=f32)
    x1_ref[...] = x1
    var = jnp.mean(x1 * x1, axis=-1, keepdims=True)
    h2 = x1 * lax.rsqrt(var + EPS) * fn_ref[...]
    h2_ref[...] = _pack_bf16_pairs(h2)
    h_hi = h2.astype(jnp.bfloat16)
    h_lo = (h2 - h_hi.astype(f32)).astype(jnp.bfloat16)
    l_hi = jnp.dot(h_hi, wr_ref[...], preferred_element_type=f32)
    l_lo = jnp.dot(h_lo, wr_ref[:, 0:ROUTE_LANES], preferred_element_type=f32)
    logits = l_hi[:, 0:ROUTE_LANES] + l_hi[:, ROUTE_LANES:2 * ROUTE_LANES] + l_lo + br_ref[...]
    lane = lax.broadcasted_iota(jnp.int32, (tm, ROUTE_LANES), 1).astype(f32)
    big = float(ROUTE_LANES)
    is_g = jnp.where(lane >= GROUP_LANE0, jnp.where(lane < GROUP_LANE0 + N_GROUPS, 1.0, 0.0), 0.0) > 0.5
    gl = jnp.where(is_g, logits, NEG_INF)
    gmax = jnp.max(gl, axis=-1, keepdims=True)
    glane = jnp.min(jnp.where(gl == gmax, lane, big), axis=-1, keepdims=True)
    gsum = jnp.sum(jnp.where(is_g, jnp.exp(gl - gmax), 0.0), axis=-1, keepdims=True)
    g_w = 1.0 / gsum
    grp = glane - GROUP_LANE0
    in_grp = jnp.floor(lane * (1.0 / EXPERTS_PER_GROUP)) == grp
    el = jnp.where(in_grp, logits, NEG_INF)
    emax = jnp.max(el, axis=-1, keepdims=True)
    ee = jnp.where(in_grp, jnp.exp(el - emax), 0.0)
    ep = ee / jnp.sum(ee, axis=-1, keepdims=True)
    ep = jnp.where(in_grp, ep, -1.0)
    p1 = jnp.max(ep, axis=-1, keepdims=True)
    i1 = jnp.min(jnp.where(ep == p1, lane, big), axis=-1, keepdims=True)
    ep2 = jnp.where(lane == i1, -1.0, ep)
    p2 = jnp.max(ep2, axis=-1, keepdims=True)
    i2 = jnp.min(jnp.where(ep2 == p2, lane, big), axis=-1, keepdims=True)
    psum = p1 + p2
    w1 = g_w * (p1 / psum)
    w2 = g_w * (p2 / psum)
    onehot = jnp.where(lane == i1, 1.0, jnp.where(lane == i2, 1.0, 0.0))
    r_i = lax.broadcasted_iota(jnp.int32, (tm, tm), 0)
    c_i = lax.broadcasted_iota(jnp.int32, (tm, tm), 1)
    tri = jnp.where(c_i < r_i, 1.0, 0.0).astype(jnp.bfloat16)
    cum = jnp.dot(tri, onehot.astype(jnp.bfloat16), preferred_element_type=f32) + base_scr[0:1, :]
    rank1 = jnp.sum(jnp.where(lane == i1, cum, 0.0), axis=-1, keepdims=True)
    rank2 = jnp.sum(jnp.where(lane == i2, cum, 0.0), axis=-1, keepdims=True)
    new_base = base_scr[0:1, :] + jnp.sum(onehot, axis=0, keepdims=True)
    base_scr[...] = jnp.broadcast_to(new_base, base_scr.shape)
    cnt_ref[...] = jnp.broadcast_to(new_base, cnt_ref.shape)
    rec = jnp.zeros((tm, ROUTE_LANES), f32)
    for ln, val in ((R_E1, i1), (R_E2, i2), (R_W1, w1), (R_W2, w2), (R_RANK1, rank1), (R_RANK2, rank2)):
        rec = jnp.where(lane == ln, val, rec)
    route_ref[...] = rec


def _out_router(x, merged, wo, fn, wr, br, tm):
    t = x.shape[0]
    n = t // tm
    return pl.pallas_call(
        functools.partial(_out_router_kernel, tm=tm),
        grid=(n,),
        in_specs=[pl.BlockSpec((tm, D_MODEL), lambda i: (i, 0)),
                  pl.BlockSpec((tm, D_MODEL), lambda i: (i, 0)),
                  pl.BlockSpec((D_MODEL, D_MODEL), lambda i: (0, 0)),
                  pl.BlockSpec((1, D_MODEL), lambda i: (0, 0)),
                  pl.BlockSpec((D_MODEL, 2 * ROUTE_LANES), lambda i: (0, 0)),
                  pl.BlockSpec((1, ROUTE_LANES), lambda i: (0, 0))],
        out_specs=[pl.BlockSpec((tm, D_MODEL), lambda i: (i, 0)),
                   pl.BlockSpec((tm, D_MODEL // 2), lambda i: (i, 0)),
                   pl.BlockSpec((tm, ROUTE_LANES), lambda i: (i, 0)),
                   pl.BlockSpec((8, ROUTE_LANES), lambda i: (i, 0))],
        out_shape=[jax.ShapeDtypeStruct((t, D_MODEL), jnp.float32),
                   jax.ShapeDtypeStruct((t, D_MODEL // 2), jnp.uint32),
                   jax.ShapeDtypeStruct((t, ROUTE_LANES), jnp.float32),
                   jax.ShapeDtypeStruct((n * 8, ROUTE_LANES), jnp.float32)],
        scratch_shapes=[pltpu.VMEM((8, ROUTE_LANES), jnp.float32)],
        compiler_params=_cparams(("arbitrary",)),
        name="out_router",
    )(x, merged, wo, fn, wr, br)


HALF = D_MODEL // 2
DMA_UNROLL = 8


def _pack_bf16_pairs(x):
    lo = pltpu.bitcast(x[:, 0:HALF].astype(jnp.bfloat16).astype(jnp.float32), jnp.uint32)
    hi = pltpu.bitcast(x[:, HALF:D_MODEL].astype(jnp.bfloat16).astype(jnp.float32), jnp.uint32)
    return hi | (lo >> 16)


def _unpack_bf16_pairs(w):
    lo = pltpu.bitcast(w << 16, jnp.float32)
    hi = pltpu.bitcast(w & jnp.uint32(0xFFFF0000), jnp.float32)
    return lo, hi


def _dispatch_kernel(zs_ref, dest_ref, h_ref, xs_ref, zero_scr, sem, zsem, *, tmd):
    @pl.when(pl.program_id(0) == 0)
    def _():
        zero_scr[...] = jnp.zeros_like(zero_scr)
        for e in range(N_EXPERTS):
            @pl.when(zs_ref[e] >= 0)
            def _():
                cp = pltpu.make_async_copy(zero_scr, xs_ref.at[pl.ds(pl.multiple_of(zs_ref[e], TMX), TMX)], zsem)
                cp.start()
                cp.wait()

        def zero_tail(tile, c):
            cp = pltpu.make_async_copy(zero_scr, xs_ref.at[pl.ds(pl.multiple_of(tile * TMX, TMX), TMX)], zsem)
            cp.start()
            cp.wait()
            return c

        lax.fori_loop(zs_ref[N_EXPERTS] // TMX, xs_ref.shape[0] // TMX, zero_tail, 0)

    def row_copy(r, k):
        return pltpu.make_async_copy(h_ref.at[pl.ds(r, 1)], xs_ref.at[pl.ds(dest_ref[0, 0, k * tmd + r], 1)], sem)

    def issue(r, c):
        row_copy(r, 0).start(priority=0)
        row_copy(r, 1).start(priority=1)
        return c

    lax.fori_loop(0, tmd, issue, 0, unroll=DMA_UNROLL)

    def drain(r, c):
        row_copy(r, 0).wait()
        row_copy(r, 1).wait()
        return c

    lax.fori_loop(0, tmd, drain, 0, unroll=DMA_UNROLL)


def _dispatch(zstart, dest3, h2p, p_rows, tmd):
    t = h2p.shape[0]
    grid_spec = pltpu.PrefetchScalarGridSpec(
        num_scalar_prefetch=1,
        grid=(t // tmd,),
        in_specs=[pl.BlockSpec((1, 1, 2 * tmd), lambda i, zs: (i, 0, 0), memory_space=pltpu.SMEM),
                  pl.BlockSpec((tmd, HALF), lambda i, zs: (i, 0))],
        out_specs=pl.BlockSpec(memory_space=pl.ANY),
        scratch_shapes=[pltpu.VMEM((TMX, HALF), jnp.uint32), pltpu.SemaphoreType.DMA(()),
                        pltpu.SemaphoreType.DMA(())],
    )
    return pl.pallas_call(
        functools.partial(_dispatch_kernel, tmd=tmd),
        grid_spec=grid_spec,
        out_shape=jax.ShapeDtypeStruct((p_rows, HALF), jnp.uint32),
        compiler_params=_cparams(("arbitrary",)),
        name="dispatch",
    )(zstart, dest3, h2p)


TM_EXPERT, TM_VALID, TM_BLOCK, TM_FIRST, TM_NEXT, TM_SLOT = range(6)


def _expert_kernel(tm_ref, xs_ref, wg_hbm, wu_hbm, wd_hbm, y_ref, wg_buf, wu_buf, wd_buf, wgu_scr, wd_scr, sems,
                   *, layer):
    i = pl.program_id(0)
    e = tm_ref[TM_EXPERT, i]
    slot = tm_ref[TM_SLOT, i]

    def weight_copies(expert, s):
        return (pltpu.make_async_copy(wg_hbm.at[layer, expert], wg_buf.at[s], sems.at[s, 0]),
                pltpu.make_async_copy(wu_hbm.at[layer, expert], wu_buf.at[s], sems.at[s, 1]),
                pltpu.make_async_copy(wd_hbm.at[layer, expert], wd_buf.at[s], sems.at[s, 2]))

    @pl.when(tm_ref[TM_FIRST, i] == 1)
    def _():
        @pl.when(i == 0)
        def _():
            for cp in weight_copies(e, slot):
                cp.start()

        for cp in weight_copies(e, slot):
            cp.wait()

        @pl.when(tm_ref[TM_NEXT, i] >= 0)
        def _():
            for cp in weight_copies(tm_ref[TM_NEXT, i], 1 - slot):
                cp.start()

        wgu_scr[:, 0:D_EXPERT] = wg_buf[slot].astype(jnp.bfloat16)
        wgu_scr[:, D_EXPERT:2 * D_EXPERT] = wu_buf[slot].astype(jnp.bfloat16)
        wd_scr[...] = wd_buf[slot].astype(jnp.bfloat16)

    @pl.when(tm_ref[TM_VALID, i] == 1)
    def _():
        x_lo, x_hi = _unpack_bf16_pairs(xs_ref[...])
        gu = (jnp.dot(x_lo.astype(jnp.bfloat16), wgu_scr[0:HALF, :], preferred_element_type=jnp.float32)
              + jnp.dot(x_hi.astype(jnp.bfloat16), wgu_scr[HALF:D_MODEL, :], preferred_element_type=jnp.float32))
        gate = gu[:, 0:D_EXPERT]
        he = (gate * jax.nn.sigmoid(gate)) * gu[:, D_EXPERT:2 * D_EXPERT]
        y = jnp.dot(he.astype(jnp.bfloat16), wd_scr[...], preferred_element_type=jnp.float32)
        y_ref[...] = _pack_bf16_pairs(y)

    @pl.when(tm_ref[TM_VALID, i] == 0)
    def _():
        y_ref[...] = jnp.zeros_like(y_ref)


def _experts(tile_meta, xs, wg, wu, wd, layer):
    p_rows = xs.shape[0]
    f32 = jnp.float32
    grid_spec = pltpu.PrefetchScalarGridSpec(
        num_scalar_prefetch=1,
        grid=(p_rows // TMX,),
        in_specs=[pl.BlockSpec((TMX, HALF), lambda i, tm: (tm[TM_BLOCK, i], 0)),
                  pl.BlockSpec(memory_space=pl.ANY),
                  pl.BlockSpec(memory_space=pl.ANY),
                  pl.BlockSpec(memory_space=pl.ANY)],
        out_specs=pl.BlockSpec((TMX, HALF), lambda i, tm: (i, 0)),
        scratch_shapes=[pltpu.VMEM((2, D_MODEL, D_EXPERT), f32), pltpu.VMEM((2, D_MODEL, D_EXPERT), f32),
                        pltpu.VMEM((2, D_EXPERT, D_MODEL), f32),
                        pltpu.VMEM((D_MODEL, 2 * D_EXPERT), jnp.bfloat16),
                        pltpu.VMEM((D_EXPERT, D_MODEL), jnp.bfloat16),
                        pltpu.SemaphoreType.DMA((2, 3))],
    )
    return pl.pallas_call(
        functools.partial(_expert_kernel, layer=layer),
        grid_spec=grid_spec,
        out_shape=jax.ShapeDtypeStruct((p_rows, HALF), jnp.uint32),
        compiler_params=_cparams(("arbitrary",)),
        name="experts",
    )(tile_meta, xs, wg, wu, wd)


def _combine_ple_kernel(dcur_ref, dnext_ref, x1_ref, route_ref, y_ref, p_ref, pn_ref, wpg_ref, wpp_ref, fn_ref,
                        o_ref, buf, sems, *, tmc, final):
    f32 = jnp.float32
    i = pl.program_id(0)
    slot = lax.rem(i, 2)

    def row_copy(d_ref, s, r, k):
        return pltpu.make_async_copy(y_ref.at[pl.ds(d_ref[0, 0, k * tmc + r], 1)], buf.at[s, k, pl.ds(r, 1)],
                                     sems.at[s])

    def issue(d_ref, s):
        def body(r, c):
            row_copy(d_ref, s, r, 0).start(priority=0)
            row_copy(d_ref, s, r, 1).start(priority=1)
            return c
        lax.fori_loop(0, tmc, body, 0, unroll=DMA_UNROLL)

    @pl.when(i == 0)
    def _():
        issue(dcur_ref, 0)

    @pl.when(i + 1 < pl.num_programs(0))
    def _():
        issue(dnext_ref, 1 - slot)

    def drain(r, c):
        row_copy(dcur_ref, slot, r, 0).wait()
        row_copy(dcur_ref, slot, r, 1).wait()
        return c

    lax.fori_loop(0, tmc, drain, 0, unroll=DMA_UNROLL)

    route = route_ref[...]
    w1 = route[:, R_W1:R_W1 + 1]
    w2 = route[:, R_W2:R_W2 + 1]
    y1_lo, y1_hi = _unpack_bf16_pairs(buf[slot, 0])
    y2_lo, y2_hi = _unpack_bf16_pairs(buf[slot, 1])
    x2 = x1_ref[...] + jnp.concatenate([w1 * y1_lo + w2 * y2_lo, w1 * y1_hi + w2 * y2_hi], axis=1)
    var = jnp.mean(x2 * x2, axis=-1, keepdims=True)
    hp = (x2 * lax.rsqrt(var + EPS) * pn_ref[...]).astype(jnp.bfloat16)
    gate = jax.nn.sigmoid(jnp.dot(hp, wpg_ref[...], preferred_element_type=f32))
    pp = jnp.dot(p_ref[...].astype(jnp.bfloat16), wpp_ref[...], preferred_element_type=f32)
    x3 = x2 + gate * pp
    if final:
        var3 = jnp.mean(x3 * x3, axis=-1, keepdims=True)
        x3 = x3 * lax.rsqrt(var3 + EPS) * fn_ref[...]
    o_ref[...] = x3


def _combine_ple(dest3, x1, route, y, p, pn, wpg, wpp, fn, tmc, final):
    t = x1.shape[0]
    n = t // tmc
    return pl.pallas_call(
        functools.partial(_combine_ple_kernel, tmc=tmc, final=final),
        grid=(n,),
        in_specs=[pl.BlockSpec((1, 1, 2 * tmc), lambda i: (i, 0, 0), memory_space=pltpu.SMEM),
                  pl.BlockSpec((1, 1, 2 * tmc), lambda i: (jnp.minimum(i + 1, n - 1), 0, 0), memory_space=pltpu.SMEM),
                  pl.BlockSpec((tmc, D_MODEL), lambda i: (i, 0)),
                  pl.BlockSpec((tmc, ROUTE_LANES), lambda i: (i, 0)),
                  pl.BlockSpec(memory_space=pl.ANY),
                  pl.BlockSpec((tmc, PLE_DIM), lambda i: (i, 0)),
                  pl.BlockSpec((1, D_MODEL), lambda i: (0, 0)),
                  pl.BlockSpec((D_MODEL, D_MODEL), lambda i: (0, 0)),
                  pl.BlockSpec((PLE_DIM, D_MODEL), lambda i: (0, 0)),
                  pl.BlockSpec((1, D_MODEL), lambda i: (0, 0))],
        out_specs=pl.BlockSpec((tmc, D_MODEL), lambda i: (i, 0)),
        out_shape=jax.ShapeDtypeStruct((t, D_MODEL), jnp.float32),
        scratch_shapes=[pltpu.VMEM((2, 2, tmc, HALF), jnp.uint32), pltpu.SemaphoreType.DMA((2,))],
        compiler_params=_cparams(("arbitrary",)),
        name="combine_ple",
    )(dest3, dest3, x1, route, y, p, pn, wpg, wpp, fn)


def _regroup_w_in(w):
    n_layers, d, n_in = w.shape
    tr = 256
    return pl.pallas_call(
        _regroup_kernel,
        grid=(n_layers, d // tr),
        in_specs=[pl.BlockSpec((1, tr, n_in), lambda l, r: (l, r, 0))],
        out_specs=pl.BlockSpec((1, tr, Z_WIDTH), lambda l, r: (l, r, 0)),
        out_shape=jax.ShapeDtypeStruct((n_layers, d, Z_WIDTH), jnp.bfloat16),
        compiler_params=_cparams(("parallel", "parallel")),
        name="regroup_w_in",
    )(w)


def _regroup_kernel(w_ref, o_ref):
    o_ng = NSA_WIDTH + 6 * KV_WIDTH
    o_conf = o_ng + N_HEADS * N_NSA_BRANCHES
    o_sc = o_conf + 2 * CONF_WIDTH
    o_mg = o_sc + 3 * SC_WIDTH
    n_in = w_ref.shape[2]
    bf16 = jnp.bfloat16
    o_ref[0, :, OFF_MERGE:OFF_MERGE + n_in - o_mg] = w_ref[0, :, o_mg:n_in].astype(bf16)
    o_ref[0, :, OFF_SC:OFF_SC + o_mg - o_sc] = w_ref[0, :, o_sc:o_mg].astype(bf16)
    o_ref[0, :, OFF_CONF:OFF_CONF + o_sc - o_conf] = w_ref[0, :, o_conf:o_sc].astype(bf16)
    o_ref[0, :, OFF_Q:OFF_Q + o_ng] = w_ref[0, :, 0:o_ng].astype(bf16)
    n_gate = o_conf - o_ng
    o_ref[0, :, OFF_NSAG:Z_WIDTH] = jnp.concatenate(
        [w_ref[0, :, o_ng:o_conf].astype(bf16), jnp.zeros((w_ref.shape[1], Z_WIDTH - OFF_NSAG - n_gate), bf16)],
        axis=1)


def _route_plan(route, counts, t, tmd):
    cnt = counts[:N_EXPERTS].astype(jnp.int32)
    padded = ((cnt + TMX - 1) // TMX) * TMX
    ends = jnp.cumsum(padded)
    starts = ends - padded
    e_idx = jnp.arange(N_EXPERTS, dtype=jnp.int32)

    def lookup(table, idx):
        return jnp.sum(jnp.where(idx[..., None] == e_idx, table, 0), axis=-1)

    dest = [lookup(starts, route[:, R_E1 + k].astype(jnp.int32).reshape(t // tmd, tmd))
            + route[:, R_RANK1 + k].astype(jnp.int32).reshape(t // tmd, tmd) for k in range(2)]
    dest3 = jnp.stack(dest, axis=1).reshape(t // tmd, 1, 2 * tmd)
    p_rows = 2 * t + N_EXPERTS * TMX
    n_tiles = p_rows // TMX
    tile_start = jnp.arange(n_tiles, dtype=jnp.int32) * TMX
    tile_e = jnp.minimum(jnp.sum((tile_start[:, None] >= ends[None, :]).astype(jnp.int32), axis=1), N_EXPERTS - 1)
    tile_v = (tile_start < ends[-1]).astype(jnp.int32)
    tile_b = jnp.where(tile_v == 1, jnp.arange(n_tiles, dtype=jnp.int32), 0)
    tile_f = tile_v * jnp.concatenate([jnp.ones((1,), jnp.int32), (tile_e[1:] != tile_e[:-1]).astype(jnp.int32)])
    has = padded > 0
    later = jnp.where(has[None, :] & (e_idx[None, :] > e_idx[:, None]), e_idx[None, :], N_EXPERTS)
    next_e = jnp.min(later, axis=1)
    next_e = jnp.where(next_e < N_EXPERTS, next_e, -1)
    seg_slot = (jnp.cumsum(has.astype(jnp.int32)) - 1) % 2
    tile_meta = jnp.stack([tile_e, tile_v, tile_b, tile_f, lookup(next_e, tile_e), lookup(seg_slot, tile_e)])
    zstart = jnp.concatenate([jnp.where(padded > cnt, ends - TMX, -1), ends[-1:]]).astype(jnp.int32)
    return dest3, tile_meta.astype(jnp.int32), zstart, p_rows


def kernel(x, p, rel_bias, attn_norm, w_in, cmp_pe, cmp_w1, cmp_w2, conf_conv_w, conf_conv_b, conf_ln_g, conf_ln_b, sc_conv_w, w_branch, w_out, ffn_norm, router_group_w, router_group_b, router_expert_w, router_expert_b, expert_w_gate, expert_w_up, expert_w_down, ple_norm, ple_gate_w, ple_proj_w, final_norm):
    b, s, d = x.shape
    t = b * s
    depth = w_in.shape[0]
    bf16 = jnp.bfloat16
    ncp = s // CMP_STRIDE
    rows4 = HEADS_PER_GROUP * TQ

    cmp_tab = _bias_tables(rel_bias, 1, s, min(1024, s), ncp, col_mult=CMP_STRIDE, d0_base=-(CMP_BLOCK - 1), d0_step=0,
                           lo=0, hi=1 << 30, out_scale=1.0)
    toe = _bias_tables(rel_bias, 6, TQ, TQ, LANE, col_mult=1, d0_base=-LANE, d0_step=LANE,
                       lo=0, hi=WINDOW, out_scale=LOG2E)
    toe = toe.reshape(KV_HEADS, HEADS_PER_GROUP, 6, TQ, LANE).transpose(0, 2, 1, 3, 4).reshape(KV_HEADS, 6, rows4, LANE)
    win_tab = jnp.stack([toe[:, N_PREV - jj + 1] for jj in range(N_PREV + 2)], axis=1)
    n_sel_tab = SEL_TQ // SEL_TK + 2
    sel_tab = _bias_tables(rel_bias, n_sel_tab, SEL_TQ, SEL_TQ, SEL_TK, col_mult=1,
                           d0_base=(1 - SEL_TQ // SEL_TK) * SEL_TK, d0_step=SEL_TK, lo=0, hi=1 << 30, out_scale=LOG2E)
    sel_tab = sel_tab.reshape(KV_HEADS, HEADS_PER_GROUP, n_sel_tab, SEL_TQ, SEL_TK).transpose(0, 2, 1, 3, 4)
    sel_tab = sel_tab.reshape(KV_HEADS, n_sel_tab, HEADS_PER_GROUP * SEL_TQ, SEL_TK)

    w_z = _regroup_w_in(w_in)
    x2d = x.reshape(t, d)
    tm_in = min(1024, t)
    tm = min(512, t)
    tmd = min(256, t)
    for i in range(depth):
        n_pad = ROUTE_LANES - N_EXPERTS - N_GROUPS
        wr = jnp.concatenate([router_expert_w[i], router_group_w[i], jnp.zeros((d, n_pad), jnp.float32)], axis=1)
        wr_hi = wr.astype(bf16)
        wr = jnp.concatenate([wr_hi, (wr - wr_hi.astype(jnp.float32)).astype(bf16)], axis=1)
        br = jnp.concatenate([router_expert_b[i], router_group_b[i], jnp.zeros((n_pad,), jnp.float32)]).reshape(1, -1)

        z, kcv = _in_proj(x2d, attn_norm[i].reshape(1, d), w_z, i, tm_in, 2048)
        cmp_kv = _compress(kcv, cmp_pe[i], cmp_w1[i].astype(bf16), cmp_w2[i].astype(bf16), b, s)
        o_cmp, neg_t = _cmp_attn(z, cmp_kv, cmp_tab, b, s)
        neg = jnp.swapaxes(neg_t, 2, 3)
        o_slc = _flash(z, sel_tab, neg, b, s, selected=True)
        o_win = _flash(z, win_tab, None, b, s, selected=False)
        uconv, o_sc = _conv(z, conf_conv_w[i], conf_conv_b[i].reshape(1, -1), sc_conv_w[i], b, s, min(512, s), 256)
        merged = _merge(z, o_cmp, o_slc, o_win, uconv, conf_ln_g[i].reshape(1, -1), conf_ln_b[i].reshape(1, -1),
                        o_sc, w_branch[i].astype(bf16), tm, 1024)

        x1, h2, route, cnts = _out_router(x2d, merged, w_out[i].astype(bf16), ffn_norm[i].reshape(1, d), wr, br, tm)
        dest3, tile_meta, zstart, p_rows = _route_plan(route, cnts[-1], t, tmd)
        xs = _dispatch(zstart, dest3, h2, p_rows, tmd)
        y = _experts(tile_meta, xs, expert_w_gate, expert_w_up, expert_w_down, i)

        x2d = _combine_ple(dest3, x1, route, y, p[i].reshape(t, PLE_DIM), ple_norm[i].reshape(1, d),
                           ple_gate_w[i].astype(bf16), ple_proj_w[i].astype(bf16), final_norm.reshape(1, d),
                           tmd, i == depth - 1)
    return x2d.reshape(b, s, d)
```

```python
import functools
import math

import jax
import jax.numpy as jnp
from jax import lax
from jax.experimental import pallas as pl
from jax.experimental.pallas import tpu as pltpu

D_MODEL = 2048
N_HEADS = 8
HEAD_DIM = 128
KV_HEADS = 2
HEADS_PER_GROUP = N_HEADS // KV_HEADS
NSA_WIDTH = N_HEADS * HEAD_DIM
KV_WIDTH = KV_HEADS * HEAD_DIM
N_NSA_BRANCHES = 3
CMP_BLOCK = 32
CMP_STRIDE = 16
CMP_HIDDEN = 512
SEL_BLOCK = 64
SEL_TOPN = 16
WINDOW = 512
WIN_QBLOCK = 128
CONF_WIDTH = 1024
CONF_CONV_WIDTH = 31
SC_WIDTH = 1024
SC_CONV_WIDTH = 3
REL_BUCKETS = 32
REL_MAX_DIST = 128
N_GROUPS = 4
EXPERTS_PER_GROUP = 8
N_EXPERTS = N_GROUPS * EXPERTS_PER_GROUP
D_EXPERT = 512
PLE_DIM = 256
EPS = 1e-6
NEG_INF = -1e30
FORCE_SCORE = 1e9
SCALE = HEAD_DIM ** -0.5
LOG2E = math.log2(math.e)

LANE = 128
SUBLANE = 8
VMEM_LIMIT = 56 * 1024 * 1024

OFF_MERGE = 0
OFF_SC = OFF_MERGE + 3 * D_MODEL
OFF_CONF = OFF_SC + 3 * SC_WIDTH
OFF_Q = OFF_CONF + 2 * CONF_WIDTH
OFF_KC = OFF_Q + NSA_WIDTH
OFF_VC = OFF_KC + KV_WIDTH
OFF_KS = OFF_VC + KV_WIDTH
OFF_VS = OFF_KS + KV_WIDTH
OFF_KW = OFF_VS + KV_WIDTH
OFF_VW = OFF_KW + KV_WIDTH
OFF_NSAG = OFF_VW + KV_WIDTH
Z_WIDTH = OFF_NSAG + 512

QW = HEADS_PER_GROUP * HEAD_DIM
TQ = 128
SEL_TQ = 512
SEL_TK = 256
N_PREV = WINDOW // WIN_QBLOCK
ROUTE_LANES = LANE
TMX = 256


def _cparams(sem, vmem=VMEM_LIMIT):
    return pltpu.CompilerParams(dimension_semantics=sem, vmem_limit_bytes=vmem)


def _t5_bucket(dist):
    n = jnp.maximum(dist, 0)
    max_exact = REL_BUCKETS // 2
    nf = jnp.maximum(n, 1).astype(jnp.float32)
    large = max_exact + (jnp.log(nf / max_exact) / math.log(REL_MAX_DIST / max_exact)
                         * (REL_BUCKETS - max_exact)).astype(jnp.int32)
    large = jnp.minimum(large, REL_BUCKETS - 1)
    return jnp.where(n < max_exact, n, large)


def _bias_table_kernel(rel_ref, o_ref, *, col_mult, d0_base, d0_step, lo, hi, rows, cols, out_scale):
    h = pl.program_id(0)
    k = pl.program_id(1)
    rb = pl.program_id(2)
    cb = pl.program_id(3)
    d0 = d0_base + d0_step * k
    d_min = rb * rows - col_mult * (cb * cols + cols - 1) + d0
    d_max = rb * rows + rows - 1 - col_mult * cb * cols + d0
    masked = jnp.logical_or(d_max < lo, d_min >= hi)
    far = jnp.logical_and(d_min >= REL_MAX_DIST, d_max < hi)

    @pl.when(masked)
    def _():
        o_ref[0, 0] = jnp.full((rows, cols), NEG_INF * out_scale, jnp.float32)

    @pl.when(far)
    def _():
        o_ref[0, 0] = jnp.full((rows, cols), rel_ref[REL_BUCKETS - 1, h] * out_scale, jnp.float32)

    @pl.when(jnp.logical_not(jnp.logical_or(masked, far)))
    def _():
        r = lax.broadcasted_iota(jnp.int32, (rows, cols), 0) + rb * rows
        c = lax.broadcasted_iota(jnp.int32, (rows, cols), 1) + cb * cols
        dist = r - col_mult * c + d0
        bucket = _t5_bucket(dist)
        val = jnp.zeros((rows, cols), jnp.float32)
        for b in range(REL_BUCKETS):
            val = jnp.where(bucket == b, rel_ref[b, h], val)
        ok = jnp.where(dist >= lo, jnp.where(dist < hi, 1.0, 0.0), 0.0)
        o_ref[0, 0] = jnp.where(ok > 0.5, val, NEG_INF) * out_scale


def _bias_tables(rel_bias, n_k, n_rows, rows, n_cols, **kw):
    cols = min(LANE, n_cols)
    kern = functools.partial(_bias_table_kernel, rows=rows, cols=cols, **kw)
    return pl.pallas_call(
        kern,
        grid=(N_HEADS, n_k, n_rows // rows, n_cols // cols),
        in_specs=[pl.BlockSpec(memory_space=pltpu.SMEM)],
        out_specs=pl.BlockSpec((1, 1, rows, cols), lambda h, k, r, c: (h, k, r, c)),
        out_shape=jax.ShapeDtypeStruct((N_HEADS, n_k, n_rows, n_cols), jnp.float32),
        compiler_params=_cparams(("parallel", "parallel", "parallel", "parallel")),
        name="bias_tables",
    )(rel_bias)


def _inproj_kernel(x_ref, g_ref, w_ref, o_ref, kcv_ref, h_scr, *, kcv_tile, kcv_col):
    @pl.when(pl.program_id(1) == 0)
    def _():
        x = x_ref[...]
        var = jnp.mean(x * x, axis=-1, keepdims=True)
        h_scr[...] = (x * lax.rsqrt(var + EPS) * g_ref[...]).astype(jnp.bfloat16)

    acc = lax.dot_general(h_scr[...], w_ref[0], (((1,), (1,)), ((), ())), preferred_element_type=jnp.float32)
    o_ref[...] = acc.astype(o_ref.dtype)

    @pl.when(pl.program_id(1) == kcv_tile)
    def _():
        kcv_ref[...] = acc[:, kcv_col:kcv_col + 2 * KV_WIDTH]


def _in_proj(x, g, w, layer, tm, tn):
    t = x.shape[0]
    assert OFF_KC % tn + 2 * KV_WIDTH <= tn
    return pl.pallas_call(
        functools.partial(_inproj_kernel, kcv_tile=OFF_KC // tn, kcv_col=OFF_KC % tn),
        grid=(t // tm, Z_WIDTH // tn),
        in_specs=[pl.BlockSpec((tm, D_MODEL), lambda i, j: (i, 0)),
                  pl.BlockSpec((1, D_MODEL), lambda i, j: (0, 0)),
                  pl.BlockSpec((1, tn, D_MODEL), lambda i, j: (layer, j, 0))],
        out_specs=[pl.BlockSpec((tm, tn), lambda i, j: (i, j)),
                   pl.BlockSpec((tm, 2 * KV_WIDTH), lambda i, j: (i, 0))],
        out_shape=[jax.ShapeDtypeStruct((t, Z_WIDTH), jnp.bfloat16),
                   jax.ShapeDtypeStruct((t, 2 * KV_WIDTH), jnp.float32)],
        scratch_shapes=[pltpu.VMEM((tm, D_MODEL), jnp.bfloat16)],
        compiler_params=_cparams(("parallel", "arbitrary")),
        name="in_proj",
    )(x, g, w)


def _compress_kernel(a_ref, pe_ref, w1_ref, w2_ref, o_ref, *, ncp):
    lo = jnp.zeros((ncp, CMP_HIDDEN), jnp.float32)
    hi = jnp.zeros((ncp, CMP_HIDDEN), jnp.float32)
    for l in range(CMP_STRIDE):
        a = a_ref[pl.ds(l, ncp, stride=CMP_STRIDE), :]
        a_lo = (a + pe_ref[0, l:l + 1, :]).astype(jnp.bfloat16)
        a_hi = (a + pe_ref[0, CMP_STRIDE + l:CMP_STRIDE + l + 1, :]).astype(jnp.bfloat16)
        lo = lo + jnp.dot(a_lo, w1_ref[0, l * HEAD_DIM:(l + 1) * HEAD_DIM, :],
                          preferred_element_type=jnp.float32)
        hi = hi + jnp.dot(a_hi, w1_ref[0, (CMP_STRIDE + l) * HEAD_DIM:(CMP_STRIDE + l + 1) * HEAD_DIM, :],
                          preferred_element_type=jnp.float32)
    hidden = lo + pltpu.roll(hi, ncp - 1, 0)
    act = jax.nn.gelu(hidden).astype(jnp.bfloat16)
    out = jnp.dot(act, w2_ref[0], preferred_element_type=jnp.float32)
    row = lax.broadcasted_iota(jnp.int32, (ncp, HEAD_DIM), 0)
    o_ref[0, 0] = jnp.where(row < ncp - 1, out, 0.0).astype(o_ref.dtype)


def _compress(a, pe, w1, w2, b, s):
    ncp = s // CMP_STRIDE
    return pl.pallas_call(
        functools.partial(_compress_kernel, ncp=ncp),
        grid=(b, 2 * KV_HEADS),
        in_specs=[pl.BlockSpec((s, HEAD_DIM), lambda i, j: (i, j)),
                  pl.BlockSpec((1, CMP_BLOCK, HEAD_DIM), lambda i, j: (j // KV_HEADS, 0, 0)),
                  pl.BlockSpec((1, CMP_BLOCK * HEAD_DIM, CMP_HIDDEN), lambda i, j: (j // KV_HEADS, 0, 0)),
                  pl.BlockSpec((1, CMP_HIDDEN, HEAD_DIM), lambda i, j: (j // KV_HEADS, 0, 0))],
        out_specs=pl.BlockSpec((1, 1, ncp, HEAD_DIM), lambda i, j: (i, j, 0, 0)),
        out_shape=jax.ShapeDtypeStruct((b, 2 * KV_HEADS, ncp, HEAD_DIM), jnp.bfloat16),
        compiler_params=_cparams(("parallel", "parallel")),
        name="compress",
    )(a, pe, w1, w2)


def _cmp_attn_kernel(q_ref, kc_ref, vc_ref, tab_ref, o_ref, neg_ref, *, tq, ncp, nsel):
    i = pl.program_id(2)
    kc = kc_ref[0, 0]
    vc = vc_ref[0, 0]
    q4 = jnp.concatenate([q_ref[:, h * HEAD_DIM:(h + 1) * HEAD_DIM] for h in range(HEADS_PER_GROUP)], axis=0)
    tab = jnp.concatenate([tab_ref[h, 0] for h in range(HEADS_PER_GROUP)], axis=0)
    s = lax.dot_general(q4, kc, (((1,), (1,)), ((), ())), preferred_element_type=jnp.float32) * SCALE + tab
    valid = tab > 0.5 * NEG_INF
    m = jnp.max(s, axis=-1, keepdims=True)
    e = jnp.where(valid, jnp.exp(s - m), 0.0)
    den = jnp.sum(e, axis=-1, keepdims=True)
    p = jnp.where(valid, e / den, 0.0)
    o = jnp.dot(p.astype(jnp.bfloat16), vc, preferred_element_type=jnp.float32)
    psum = p[0:tq]
    for h in range(HEADS_PER_GROUP):
        o_ref[:, h * HEAD_DIM:(h + 1) * HEAD_DIM] = o[h * tq:(h + 1) * tq].astype(o_ref.dtype)
        if h:
            psum = psum + p[h * tq:(h + 1) * tq]
    blk_n = lax.broadcasted_iota(jnp.int32, (nsel, ncp), 0)
    c_start = lax.broadcasted_iota(jnp.int32, (nsel, ncp), 1) * CMP_STRIDE
    ov = jnp.where(c_start < (blk_n + 1) * SEL_BLOCK,
                   jnp.where(c_start + CMP_BLOCK > blk_n * SEL_BLOCK, 1.0, 0.0), 0.0)
    imp = lax.dot_general(ov, psum, (((1,), (1,)), ((), ())), precision=lax.Precision.HIGHEST,
                          preferred_element_type=jnp.float32)
    blk = lax.broadcasted_iota(jnp.int32, (nsel, tq), 0)
    t = lax.broadcasted_iota(jnp.int32, (nsel, tq), 1) + i * tq
    cur = lax.shift_right_logical(t, 6)
    forced = jnp.where(blk == 0, 1.0, jnp.where(blk == cur, 1.0, jnp.where(blk == cur - 1, 1.0, 0.0)))
    causal = blk * SEL_BLOCK <= t
    score = jnp.where(forced > 0.5, FORCE_SCORE, jnp.where(causal, imp, NEG_INF))
    rank = jnp.zeros((nsel, tq), jnp.float32)
    for k in range(nsel):
        row = score[k:k + 1, :]
        rank = rank + jnp.where(blk > k, jnp.where(row >= score, 1.0, 0.0), jnp.where(row > score, 1.0, 0.0))
    n_top = min(SEL_TOPN, nsel)
    neg = jnp.where(rank < n_top, jnp.where(causal, 0.0, NEG_INF), NEG_INF)
    neg_ref[0, 0] = neg.astype(neg_ref.dtype)


def _cmp_attn(z, cmp_kv, tab, b, s):
    ncp = s // CMP_STRIDE
    nsel = s // SEL_BLOCK
    nq = s // TQ
    kern = functools.partial(_cmp_attn_kernel, tq=TQ, ncp=ncp, nsel=nsel)
    return pl.pallas_call(
        kern,
        grid=(b, KV_HEADS, nq),
        in_specs=[pl.BlockSpec((TQ, QW), lambda bi, g, i: (bi * nq + i, OFF_Q // QW + g)),
                  pl.BlockSpec((1, 1, ncp, HEAD_DIM), lambda bi, g, i: (bi, g, 0, 0)),
                  pl.BlockSpec((1, 1, ncp, HEAD_DIM), lambda bi, g, i: (bi, KV_HEADS + g, 0, 0)),
                  pl.BlockSpec((HEADS_PER_GROUP, 1, TQ, ncp), lambda bi, g, i: (g, 0, i, 0))],
        out_specs=[pl.BlockSpec((TQ, QW), lambda bi, g, i: (bi * nq + i, g)),
                   pl.BlockSpec((1, 1, nsel, TQ), lambda bi, g, i: (bi, g, 0, i))],
        out_shape=[jax.ShapeDtypeStruct((b * s, NSA_WIDTH), jnp.bfloat16),
                   jax.ShapeDtypeStruct((b, KV_HEADS, nsel, s), jnp.bfloat16)],
        compiler_params=_cparams(("parallel", "parallel", "parallel")),
        name="cmp_attn",
    )(z, cmp_kv, cmp_kv, tab)


def _stack_heads(q_ref, g):
    q = jnp.concatenate(
        [q_ref[:, (g * HEADS_PER_GROUP + h) * HEAD_DIM:(g * HEADS_PER_GROUP + h + 1) * HEAD_DIM]
         for h in range(HEADS_PER_GROUP)], axis=0)
    return (q.astype(jnp.float32) * (SCALE * LOG2E)).astype(jnp.bfloat16)


def _unstack_heads(o_ref, g, out, tq):
    for h in range(HEADS_PER_GROUP):
        c0 = (g * HEADS_PER_GROUP + h) * HEAD_DIM
        o_ref[:, c0:c0 + HEAD_DIM] = out[h * tq:(h + 1) * tq].astype(o_ref.dtype)


def _sel_kernel(q_ref, k_ref, v_ref, tab_ref, neg_ref, o_ref, m0_scr, m1_scr, acc0_scr, acc1_scr, *, tq, tk, nsel):
    i = pl.program_id(1)
    sub = tk // LANE
    pad = jnp.zeros((HEADS_PER_GROUP * tq, HEAD_DIM - nsel), jnp.bfloat16)
    q4 = [jnp.concatenate([_stack_heads(q_ref, g), jnp.concatenate([neg_ref[0, g]] * HEADS_PER_GROUP, axis=0), pad],
                          axis=1) for g in range(KV_HEADS)]
    kpad = jnp.zeros((tk, HEAD_DIM - nsel), jnp.bfloat16)
    m_scr = (m0_scr, m1_scr)
    acc_scr = (acc0_scr, acc1_scr)
    for g in range(KV_HEADS):
        m_scr[g][...] = jnp.full(m_scr[g].shape, NEG_INF, jnp.float32)
        acc_scr[g][...] = jnp.zeros(acc_scr[g].shape, jnp.float32)
    u_min = 1 - tq // tk
    n_far = tab_ref.shape[1] - 1
    ones = jnp.ones((tk, LANE), jnp.bfloat16)

    def key_tile(j):
        tidx = jnp.minimum(i * (tq // tk) - j - u_min, n_far)
        k_i = lax.broadcasted_iota(jnp.int32, (tk, nsel), 0)
        n_i = lax.broadcasted_iota(jnp.int32, (tk, nsel), 1)
        onehot = jnp.where(lax.shift_right_logical(k_i + j * tk, 6) == n_i, 1.0, 0.0).astype(jnp.bfloat16)
        start = pl.multiple_of(j * tk, tk)
        for g in range(KV_HEADS):
            gs = slice(g * HEAD_DIM, (g + 1) * HEAD_DIM)
            kt = jnp.concatenate([k_ref[pl.ds(start, tk), gs], onehot, kpad], axis=1)
            vt = jnp.concatenate([v_ref[pl.ds(start, tk), gs], ones], axis=1)
            s = lax.dot_general(q4[g], kt, (((1,), (1,)), ((), ())), preferred_element_type=jnp.float32)
            s = s + tab_ref[g, tidx]
            m_prev = m_scr[g][...]
            m_new = jnp.maximum(m_prev, jnp.max(s, axis=-1, keepdims=True))
            alpha = jnp.exp2(m_prev - m_new)
            p = jnp.exp2(s - jnp.concatenate([m_new] * sub, axis=1))
            acc_scr[g][...] = (jnp.concatenate([alpha, alpha], axis=1) * acc_scr[g][...]
                               + jnp.dot(p.astype(jnp.bfloat16), vt, preferred_element_type=jnp.float32))
            m_scr[g][...] = m_new

    per_trip = tq // tk

    def body(jj, carry):
        for jo in range(per_trip):
            key_tile(jj * per_trip + jo)
        return carry

    lax.fori_loop(0, i + 1, body, 0)
    for g in range(KV_HEADS):
        _unstack_heads(o_ref, g, acc_scr[g][:, 0:HEAD_DIM] / acc_scr[g][:, HEAD_DIM:2 * HEAD_DIM], tq)


def _win_kernel(q_ref, k_ref, v_ref, tab_ref, o_ref, *, tq):
    i = pl.program_id(1)
    nb = N_PREV + 1
    sb = jnp.maximum(i - N_PREV, 0)
    off = sb - i + N_PREV
    start = pl.multiple_of(sb * WIN_QBLOCK, WIN_QBLOCK)
    ones = jnp.ones((nb * WIN_QBLOCK, LANE), jnp.bfloat16)
    for g in range(KV_HEADS):
        gs = slice(g * HEAD_DIM, (g + 1) * HEAD_DIM)
        kt = k_ref[pl.ds(start, nb * WIN_QBLOCK), gs]
        vt = jnp.concatenate([v_ref[pl.ds(start, nb * WIN_QBLOCK), gs], ones], axis=1)
        s = lax.dot_general(_stack_heads(q_ref, g), kt, (((1,), (1,)), ((), ())),
                            preferred_element_type=jnp.float32)
        s = s + jnp.concatenate([tab_ref[g, jnp.minimum(u + off, nb)] for u in range(nb)], axis=1)
        p = jnp.exp2(s - jnp.max(s, axis=-1, keepdims=True))
        pv = jnp.dot(p.astype(jnp.bfloat16), vt, preferred_element_type=jnp.float32)
        _unstack_heads(o_ref, g, pv[:, 0:HEAD_DIM] / pv[:, HEAD_DIM:2 * HEAD_DIM], tq)


def _flash(z, tab, neg, b, s, *, selected):
    tq = SEL_TQ if selected else TQ
    nq = s // tq
    tk = SEL_TK if selected else WIN_QBLOCK
    k_off, v_off = (OFF_KS, OFF_VS) if selected else (OFF_KW, OFF_VW)
    rows = HEADS_PER_GROUP * tq
    n_tab = tab.shape[1]
    in_specs = [pl.BlockSpec((tq, NSA_WIDTH), lambda bi, i: (bi * nq + i, OFF_Q // NSA_WIDTH)),
                pl.BlockSpec((s, KV_WIDTH), lambda bi, i: (bi, k_off // KV_WIDTH)),
                pl.BlockSpec((s, KV_WIDTH), lambda bi, i: (bi, v_off // KV_WIDTH)),
                pl.BlockSpec((KV_HEADS, n_tab, rows, tk), lambda bi, i: (0, 0, 0, 0),
                             pipeline_mode=pl.Buffered(1))]
    args = [z, z, z, tab]
    scratch = []
    if selected:
        nsel = s // SEL_BLOCK
        in_specs.append(pl.BlockSpec((1, KV_HEADS, tq, nsel), lambda bi, i: (bi, 0, i, 0)))
        args.append(neg)
        kern = functools.partial(_sel_kernel, tq=tq, tk=tk, nsel=nsel)
        scratch = ([pltpu.VMEM((rows, LANE), jnp.float32)] * KV_HEADS
                   + [pltpu.VMEM((rows, 2 * HEAD_DIM), jnp.float32)] * KV_HEADS)
    else:
        kern = functools.partial(_win_kernel, tq=tq)
    return pl.pallas_call(
        kern,
        grid=(b, nq),
        in_specs=in_specs,
        out_specs=pl.BlockSpec((tq, NSA_WIDTH), lambda bi, i: (bi * nq + i, 0)),
        out_shape=jax.ShapeDtypeStruct((b * s, NSA_WIDTH), jnp.bfloat16),
        scratch_shapes=scratch,
        compiler_params=_cparams(("parallel", "parallel")),
        name="flash_sel" if selected else "flash_win",
    )(*args)


HALO = 32
CONV_RC = 64


def _conv_kernel(a_ref, g_ref, ah_ref, gh_ref, bg_ref, cg_ref, xs_ref, cgh_ref, xsh_ref,
                 cw_ref, cb_ref, sw_ref, uo_ref, so_ref, ext_scr, ext2_scr, *, ts):
    first = pl.program_id(1) == 0
    f32 = jnp.float32
    n_ext = HALO + ts
    n_sh = n_ext - SUBLANE
    u = a_ref[...].astype(f32) * jax.nn.sigmoid(g_ref[...].astype(f32))
    uh = ah_ref[...].astype(f32) * jax.nn.sigmoid(gh_ref[...].astype(f32))
    ext_scr[0, 0:HALO, :] = jnp.where(first, 0.0, uh)
    ext_scr[0, HALO:n_ext, :] = u
    v = cg_ref[...].astype(f32) * xs_ref[...].astype(f32)
    vh = cgh_ref[...].astype(f32) * xsh_ref[...].astype(f32)
    ext2_scr[0, 0:HALO, :] = jnp.where(first, 0.0, vh)
    ext2_scr[0, HALO:n_ext, :] = v
    base = HALO - (CONF_CONV_WIDTH - 1)
    base2 = HALO - (SC_CONV_WIDTH - 1)
    for r in range(1, SUBLANE):
        ext_scr[r, 0:n_sh, :] = ext_scr[0, r:r + n_sh, :]
    sc_shifts = sorted({(base2 + k) % SUBLANE for k in range(SC_CONV_WIDTH)} - {0})
    for r in sc_shifts:
        ext2_scr[r, 0:n_sh, :] = ext2_scr[0, r:r + n_sh, :]

    def tap(scr, off, r0):
        r = off % SUBLANE
        return scr[r, r0 + off - r:r0 + off - r + CONV_RC, :]

    for r0 in range(0, ts, CONV_RC):
        acc = jnp.zeros((CONV_RC, a_ref.shape[1]), f32) + cb_ref[...]
        for k in range(CONF_CONV_WIDTH):
            acc = acc + cw_ref[k:k + 1, :] * tap(ext_scr, base + k, r0)
        uo_ref[r0:r0 + CONV_RC, :] = acc.astype(uo_ref.dtype)
        acc2 = jnp.zeros((CONV_RC, a_ref.shape[1]), f32)
        for k in range(SC_CONV_WIDTH):
            acc2 = acc2 + sw_ref[k:k + 1, :] * tap(ext2_scr, base2 + k, r0)
        so_ref[r0:r0 + CONV_RC, :] = (bg_ref[r0:r0 + CONV_RC, :].astype(f32) * acc2).astype(so_ref.dtype)


def _conv(z, cw, cb, sw, b, s, ts, tc):
    ns = s // ts
    t = b * s

    def cur(off):
        return pl.BlockSpec((ts, tc), lambda bi, i, c: (bi * ns + i, off // tc + c))

    def halo(off):
        return pl.BlockSpec((HALO, tc), lambda bi, i, c: (jnp.maximum((bi * s + i * ts) // HALO - 1, 0), off // tc + c))

    return pl.pallas_call(
        functools.partial(_conv_kernel, ts=ts),
        grid=(b, ns, CONF_WIDTH // tc),
        in_specs=[cur(OFF_CONF), cur(OFF_CONF + CONF_WIDTH), halo(OFF_CONF), halo(OFF_CONF + CONF_WIDTH),
                  cur(OFF_SC), cur(OFF_SC + SC_WIDTH), cur(OFF_SC + 2 * SC_WIDTH),
                  halo(OFF_SC + SC_WIDTH), halo(OFF_SC + 2 * SC_WIDTH),
                  pl.BlockSpec((CONF_CONV_WIDTH, tc), lambda bi, i, c: (0, c)),
                  pl.BlockSpec((1, tc), lambda bi, i, c: (0, c)),
                  pl.BlockSpec((SC_CONV_WIDTH, tc), lambda bi, i, c: (0, c))],
        out_specs=[pl.BlockSpec((ts, tc), lambda bi, i, c: (bi * ns + i, c)),
                   pl.BlockSpec((ts, tc), lambda bi, i, c: (bi * ns + i, c))],
        out_shape=[jax.ShapeDtypeStruct((t, CONF_WIDTH), jnp.bfloat16),
                   jax.ShapeDtypeStruct((t, SC_WIDTH), jnp.bfloat16)],
        scratch_shapes=[pltpu.VMEM((SUBLANE, HALO + ts, tc), jnp.float32),
                        pltpu.VMEM((SUBLANE, HALO + ts, tc), jnp.float32)],
        compiler_params=_cparams(("parallel", "parallel", "parallel")),
        name="conv",
    )(z, z, z, z, z, z, z, z, z, cw, cb, sw)


def _merge_kernel(ocmp_ref, oslc_ref, owin_ref, ng_ref, uc_ref, lng_ref, lnb_ref, osc_ref,
                  mg0_ref, mg1_ref, mg2_ref, wn_ref, wc_ref, ws_ref, o_ref, nsa_scr, conf_scr):
    f32 = jnp.float32

    @pl.when(pl.program_id(1) == 0)
    def _():
        gt = jax.nn.sigmoid(ng_ref[...].astype(f32))
        for h in range(N_HEADS):
            sl = slice(h * HEAD_DIM, (h + 1) * HEAD_DIM)
            c = N_NSA_BRANCHES * h
            o = (gt[:, c:c + 1] * ocmp_ref[:, sl].astype(f32) + gt[:, c + 1:c + 2] * oslc_ref[:, sl].astype(f32)
                 + gt[:, c + 2:c + 3] * owin_ref[:, sl].astype(f32))
            nsa_scr[:, sl] = o.astype(nsa_scr.dtype)
        u = uc_ref[...].astype(f32)
        mu = jnp.mean(u, axis=-1, keepdims=True)
        var = jnp.mean(jnp.square(u - mu), axis=-1, keepdims=True)
        y = (u - mu) * lax.rsqrt(var + EPS) * lng_ref[...] + lnb_ref[...]
        conf_scr[...] = (y * jax.nn.sigmoid(y)).astype(conf_scr.dtype)

    a = jnp.dot(nsa_scr[...], wn_ref[...], preferred_element_type=f32)
    b = jnp.dot(conf_scr[...], wc_ref[...], preferred_element_type=f32)
    c = jnp.dot(osc_ref[...], ws_ref[...], preferred_element_type=f32)
    m = (jax.nn.sigmoid(mg0_ref[...].astype(f32)) * a + jax.nn.sigmoid(mg1_ref[...].astype(f32)) * b
         + jax.nn.sigmoid(mg2_ref[...].astype(f32)) * c)
    o_ref[...] = m.astype(o_ref.dtype)


def _merge(z, ocmp, oslc, owin, uconv, lng, lnb, osc, wb, tm, tn):
    t = z.shape[0]
    nj = D_MODEL // tn
    row = lambda w: pl.BlockSpec((tm, w), lambda i, j: (i, 0))
    return pl.pallas_call(
        _merge_kernel,
        grid=(t // tm, nj),
        in_specs=[row(NSA_WIDTH), row(NSA_WIDTH), row(NSA_WIDTH),
                  pl.BlockSpec((tm, LANE), lambda i, j: (i, OFF_NSAG // LANE)),
                  row(CONF_WIDTH),
                  pl.BlockSpec((1, CONF_WIDTH), lambda i, j: (0, 0)),
                  pl.BlockSpec((1, CONF_WIDTH), lambda i, j: (0, 0)),
                  row(SC_WIDTH),
                  pl.BlockSpec((tm, tn), lambda i, j: (i, j)),
                  pl.BlockSpec((tm, tn), lambda i, j: (i, nj + j)),
                  pl.BlockSpec((tm, tn), lambda i, j: (i, 2 * nj + j)),
                  pl.BlockSpec((NSA_WIDTH, tn), lambda i, j: (0, j)),
                  pl.BlockSpec((CONF_WIDTH, tn), lambda i, j: (NSA_WIDTH // CONF_WIDTH, j)),
                  pl.BlockSpec((SC_WIDTH, tn), lambda i, j: ((NSA_WIDTH + CONF_WIDTH) // SC_WIDTH, j))],
        out_specs=pl.BlockSpec((tm, tn), lambda i, j: (i, j)),
        out_shape=jax.ShapeDtypeStruct((t, D_MODEL), jnp.bfloat16),
        scratch_shapes=[pltpu.VMEM((tm, NSA_WIDTH), jnp.bfloat16), pltpu.VMEM((tm, CONF_WIDTH), jnp.bfloat16)],
        compiler_params=_cparams(("parallel", "arbitrary")),
        name="merge",
    )(ocmp, oslc, owin, z, uconv, lng, lnb, osc, z, z, z, wb, wb, wb)


R_E1, R_E2, R_W1, R_W2, R_RANK1, R_RANK2 = range(6)
GROUP_LANE0 = N_EXPERTS


def _out_router_kernel(x_ref, m_ref, wo_ref, fn_ref, wr_ref, br_ref, x1_ref, h2_ref, route_ref, cnt_ref,
                       base_scr, *, tm):
    f32 = jnp.float32

    @pl.when(pl.program_id(0) == 0)
    def _():
        base_scr[...] = jnp.zeros_like(base_scr)

    x1 = x_ref[...] + jnp.dot(m_ref[...], wo_ref[...], preferred_element_type=f32)
    x1_ref[...] = x1
    var = jnp.mean(x1 * x1, axis=-1, keepdims=True)
    h2 = x1 * lax.rsqrt(var + EPS) * fn_ref[...]
    h2_ref[...] = _pack_bf16_pairs(h2)
    h_hi = h2.astype(jnp.bfloat16)
    h_lo = (h2 - h_hi.astype(f32)).astype(jnp.bfloat16)
    l_hi = jnp.dot(h_hi, wr_ref[...], preferred_element_type=f32)
    l_lo = jnp.dot(h_lo, wr_ref[:, 0:ROUTE_LANES], preferred_element_type=f32)
    logits = l_hi[:, 0:ROUTE_LANES] + l_hi[:, ROUTE_LANES:2 * ROUTE_LANES] + l_lo + br_ref[...]
    lane = lax.broadcasted_iota(jnp.int32, (tm, ROUTE_LANES), 1).astype(f32)
    big = float(ROUTE_LANES)
    is_g = jnp.where(lane >= GROUP_LANE0, jnp.where(lane < GROUP_LANE0 + N_GROUPS, 1.0, 0.0), 0.0) > 0.5
    gl = jnp.where(is_g, logits, NEG_INF)
    gmax = jnp.max(gl, axis=-1, keepdims=True)
    glane = jnp.min(jnp.where(gl == gmax, lane, big), axis=-1, keepdims=True)
    gsum = jnp.sum(jnp.where(is_g, jnp.exp(gl - gmax), 0.0), axis=-1, keepdims=True)
    g_w = 1.0 / gsum
    grp = glane - GROUP_LANE0
    in_grp = jnp.floor(lane * (1.0 / EXPERTS_PER_GROUP)) == grp
    el = jnp.where(in_grp, logits, NEG_INF)
    emax = jnp.max(el, axis=-1, keepdims=True)
    ee = jnp.where(in_grp, jnp.exp(el - emax), 0.0)
    ep = ee / jnp.sum(ee, axis=-1, keepdims=True)
    ep = jnp.where(in_grp, ep, -1.0)
    p1 = jnp.max(ep, axis=-1, keepdims=True)
    i1 = jnp.min(jnp.where(ep == p1, lane, big), axis=-1, keepdims=True)
    ep2 = jnp.where(lane == i1, -1.0, ep)
    p2 = jnp.max(ep2, axis=-1, keepdims=True)
    i2 = jnp.min(jnp.where(ep2 == p2, lane, big), axis=-1, keepdims=True)
    psum = p1 + p2
    w1 = g_w * (p1 / psum)
    w2 = g_w * (p2 / psum)
    onehot = jnp.where(lane == i1, 1.0, jnp.where(lane == i2, 1.0, 0.0))
    r_i = lax.broadcasted_iota(jnp.int32, (tm, tm), 0)
    c_i = lax.broadcasted_iota(jnp.int32, (tm, tm), 1)
    tri = jnp.where(c_i < r_i, 1.0, 0.0).astype(jnp.bfloat16)
    cum = jnp.dot(tri, onehot.astype(jnp.bfloat16), preferred_element_type=f32) + base_scr[0:1, :]
    rank1 = jnp.sum(jnp.where(lane == i1, cum, 0.0), axis=-1, keepdims=True)
    rank2 = jnp.sum(jnp.where(lane == i2, cum, 0.0), axis=-1, keepdims=True)
    new_base = base_scr[0:1, :] + jnp.sum(onehot, axis=0, keepdims=True)
    base_scr[...] = jnp.broadcast_to(new_base, base_scr.shape)
    cnt_ref[...] = jnp.broadcast_to(new_base, cnt_ref.shape)
    rec = jnp.zeros((tm, ROUTE_LANES), f32)
    for ln, val in ((R_E1, i1), (R_E2, i2), (R_W1, w1), (R_W2, w2), (R_RANK1, rank1), (R_RANK2, rank2)):
        rec = jnp.where(lane == ln, val, rec)
    route_ref[...] = rec


def _out_router(x, merged, wo, fn, wr, br, tm):
    t = x.shape[0]
    n = t // tm
    return pl.pallas_call(
        functools.partial(_out_router_kernel, tm=tm),
        grid=(n,),
        in_specs=[pl.BlockSpec((tm, D_MODEL), lambda i: (i, 0)),
                  pl.BlockSpec((tm, D_MODEL), lambda i: (i, 0)),
                  pl.BlockSpec((D_MODEL, D_MODEL), lambda i: (0, 0)),
                  pl.BlockSpec((1, D_MODEL), lambda i: (0, 0)),
                  pl.BlockSpec((D_MODEL, 2 * ROUTE_LANES), lambda i: (0, 0)),
                  pl.BlockSpec((1, ROUTE_LANES), lambda i: (0, 0))],
        out_specs=[pl.BlockSpec((tm, D_MODEL), lambda i: (i, 0)),
                   pl.BlockSpec((tm, D_MODEL // 2), lambda i: (i, 0)),
                   pl.BlockSpec((tm, ROUTE_LANES), lambda i: (i, 0)),
                   pl.BlockSpec((8, ROUTE_LANES), lambda i: (i, 0))],
        out_shape=[jax.ShapeDtypeStruct((t, D_MODEL), jnp.float32),
                   jax.ShapeDtypeStruct((t, D_MODEL // 2), jnp.uint32),
                   jax.ShapeDtypeStruct((t, ROUTE_LANES), jnp.float32),
                   jax.ShapeDtypeStruct((n * 8, ROUTE_LANES), jnp.float32)],
        scratch_shapes=[pltpu.VMEM((8, ROUTE_LANES), jnp.float32)],
        compiler_params=_cparams(("arbitrary",)),
        name="out_router",
    )(x, merged, wo, fn, wr, br)


HALF = D_MODEL // 2
DMA_UNROLL = 8


def _pack_bf16_pairs(x):
    lo = pltpu.bitcast(x[:, 0:HALF].astype(jnp.bfloat16).astype(jnp.float32), jnp.uint32)
    hi = pltpu.bitcast(x[:, HALF:D_MODEL].astype(jnp.bfloat16).astype(jnp.float32), jnp.uint32)
    return hi | (lo >> 16)


def _unpack_bf16_pairs(w):
    lo = pltpu.bitcast(w << 16, jnp.float32)
    hi = pltpu.bitcast(w & jnp.uint32(0xFFFF0000), jnp.float32)
    return lo, hi


def _dispatch_kernel(zs_ref, dest_ref, h_ref, xs_ref, zero_scr, sem, zsem, *, tmd):
    @pl.when(pl.program_id(0) == 0)
    def _():
        zero_scr[...] = jnp.zeros_like(zero_scr)
        for e in range(N_EXPERTS):
            @pl.when(zs_ref[e] >= 0)
            def _():
                cp = pltpu.make_async_copy(zero_scr, xs_ref.at[pl.ds(pl.multiple_of(zs_ref[e], TMX), TMX)], zsem)
                cp.start()
                cp.wait()

        def zero_tail(tile, c):
            cp = pltpu.make_async_copy(zero_scr, xs_ref.at[pl.ds(pl.multiple_of(tile * TMX, TMX), TMX)], zsem)
            cp.start()
            cp.wait()
            return c

        lax.fori_loop(zs_ref[N_EXPERTS] // TMX, xs_ref.shape[0] // TMX, zero_tail, 0)

    def row_copy(r, k):
        return pltpu.make_async_copy(h_ref.at[pl.ds(r, 1)], xs_ref.at[pl.ds(dest_ref[0, 0, k * tmd + r], 1)], sem)

    def issue(r, c):
        row_copy(r, 0).start(priority=0)
        row_copy(r, 1).start(priority=1)
        return c

    lax.fori_loop(0, tmd, issue, 0, unroll=DMA_UNROLL)

    def drain(r, c):
        row_copy(r, 0).wait()
        row_copy(r, 1).wait()
        return c

    lax.fori_loop(0, tmd, drain, 0, unroll=DMA_UNROLL)


def _dispatch(zstart, dest3, h2p, p_rows, tmd):
    t = h2p.shape[0]
    grid_spec = pltpu.PrefetchScalarGridSpec(
        num_scalar_prefetch=1,
        grid=(t // tmd,),
        in_specs=[pl.BlockSpec((1, 1, 2 * tmd), lambda i, zs: (i, 0, 0), memory_space=pltpu.SMEM),
                  pl.BlockSpec((tmd, HALF), lambda i, zs: (i, 0))],
        out_specs=pl.BlockSpec(memory_space=pl.ANY),
        scratch_shapes=[pltpu.VMEM((TMX, HALF), jnp.uint32), pltpu.SemaphoreType.DMA(()),
                        pltpu.SemaphoreType.DMA(())],
    )
    return pl.pallas_call(
        functools.partial(_dispatch_kernel, tmd=tmd),
        grid_spec=grid_spec,
        out_shape=jax.ShapeDtypeStruct((p_rows, HALF), jnp.uint32),
        compiler_params=_cparams(("arbitrary",)),
        name="dispatch",
    )(zstart, dest3, h2p)


TM_EXPERT, TM_VALID, TM_BLOCK, TM_FIRST, TM_NEXT, TM_SLOT = range(6)


def _expert_kernel(tm_ref, xs_ref, wg_hbm, wu_hbm, wd_hbm, y_ref, wg_buf, wu_buf, wd_buf, wgu_scr, wd_scr, sems,
                   *, layer):
    i = pl.program_id(0)
    e = tm_ref[TM_EXPERT, i]
    slot = tm_ref[TM_SLOT, i]

    def weight_copies(expert, s):
        return (pltpu.make_async_copy(wg_hbm.at[layer, expert], wg_buf.at[s], sems.at[s, 0]),
                pltpu.make_async_copy(wu_hbm.at[layer, expert], wu_buf.at[s], sems.at[s, 1]),
                pltpu.make_async_copy(wd_hbm.at[layer, expert], wd_buf.at[s], sems.at[s, 2]))

    @pl.when(tm_ref[TM_FIRST, i] == 1)
    def _():
        @pl.when(i == 0)
        def _():
            for cp in weight_copies(e, slot):
                cp.start()

        for cp in weight_copies(e, slot):
            cp.wait()

        @pl.when(tm_ref[TM_NEXT, i] >= 0)
        def _():
            for cp in weight_copies(tm_ref[TM_NEXT, i], 1 - slot):
                cp.start()

        wgu_scr[:, 0:D_EXPERT] = wg_buf[slot].astype(jnp.bfloat16)
        wgu_scr[:, D_EXPERT:2 * D_EXPERT] = wu_buf[slot].astype(jnp.bfloat16)
        wd_scr[...] = wd_buf[slot].astype(jnp.bfloat16)

    @pl.when(tm_ref[TM_VALID, i] == 1)
    def _():
        x_lo, x_hi = _unpack_bf16_pairs(xs_ref[...])
        gu = (jnp.dot(x_lo.astype(jnp.bfloat16), wgu_scr[0:HALF, :], preferred_element_type=jnp.float32)
              + jnp.dot(x_hi.astype(jnp.bfloat16), wgu_scr[HALF:D_MODEL, :], preferred_element_type=jnp.float32))
        gate = gu[:, 0:D_EXPERT]
        he = (gate * jax.nn.sigmoid(gate)) * gu[:, D_EXPERT:2 * D_EXPERT]
        y = jnp.dot(he.astype(jnp.bfloat16), wd_scr[...], preferred_element_type=jnp.float32)
        y_ref[...] = _pack_bf16_pairs(y)

    @pl.when(tm_ref[TM_VALID, i] == 0)
    def _():
        y_ref[...] = jnp.zeros_like(y_ref)


def _experts(tile_meta, xs, wg, wu, wd, layer):
    p_rows = xs.shape[0]
    f32 = jnp.float32
    grid_spec = pltpu.PrefetchScalarGridSpec(
        num_scalar_prefetch=1,
        grid=(p_rows // TMX,),
        in_specs=[pl.BlockSpec((TMX, HALF), lambda i, tm: (tm[TM_BLOCK, i], 0)),
                  pl.BlockSpec(memory_space=pl.ANY),
                  pl.BlockSpec(memory_space=pl.ANY),
                  pl.BlockSpec(memory_space=pl.ANY)],
        out_specs=pl.BlockSpec((TMX, HALF), lambda i, tm: (i, 0)),
        scratch_shapes=[pltpu.VMEM((2, D_MODEL, D_EXPERT), f32), pltpu.VMEM((2, D_MODEL, D_EXPERT), f32),
                        pltpu.VMEM((2, D_EXPERT, D_MODEL), f32),
                        pltpu.VMEM((D_MODEL, 2 * D_EXPERT), jnp.bfloat16),
                        pltpu.VMEM((D_EXPERT, D_MODEL), jnp.bfloat16),
                        pltpu.SemaphoreType.DMA((2, 3))],
    )
    return pl.pallas_call(
        functools.partial(_expert_kernel, layer=layer),
        grid_spec=grid_spec,
        out_shape=jax.ShapeDtypeStruct((p_rows, HALF), jnp.uint32),
        compiler_params=_cparams(("arbitrary",)),
        name="experts",
    )(tile_meta, xs, wg, wu, wd)


def _combine_ple_kernel(dcur_ref, dnext_ref, x1_ref, route_ref, y_ref, p_ref, pn_ref, wpg_ref, wpp_ref, fn_ref,
                        o_ref, buf, sems, *, tmc, final):
    f32 = jnp.float32
    i = pl.program_id(0)
    slot = lax.rem(i, 2)

    def row_copy(d_ref, s, r, k):
        return pltpu.make_async_copy(y_ref.at[pl.ds(d_ref[0, 0, k * tmc + r], 1)], buf.at[s, k, pl.ds(r, 1)],
                                     sems.at[s])

    def issue(d_ref, s):
        def body(r, c):
            row_copy(d_ref, s, r, 0).start(priority=0)
            row_copy(d_ref, s, r, 1).start(priority=1)
            return c
        lax.fori_loop(0, tmc, body, 0, unroll=DMA_UNROLL)

    @pl.when(i == 0)
    def _():
        issue(dcur_ref, 0)

    @pl.when(i + 1 < pl.num_programs(0))
    def _():
        issue(dnext_ref, 1 - slot)

    def drain(r, c):
        row_copy(dcur_ref, slot, r, 0).wait()
        row_copy(dcur_ref, slot, r, 1).wait()
        return c

    lax.fori_loop(0, tmc, drain, 0, unroll=DMA_UNROLL)

    route = route_ref[...]
    w1 = route[:, R_W1:R_W1 + 1]
    w2 = route[:, R_W2:R_W2 + 1]
    y1_lo, y1_hi = _unpack_bf16_pairs(buf[slot, 0])
    y2_lo, y2_hi = _unpack_bf16_pairs(buf[slot, 1])
    x2 = x1_ref[...] + jnp.concatenate([w1 * y1_lo + w2 * y2_lo, w1 * y1_hi + w2 * y2_hi], axis=1)
    var = jnp.mean(x2 * x2, axis=-1, keepdims=True)
    hp = (x2 * lax.rsqrt(var + EPS) * pn_ref[...]).astype(jnp.bfloat16)
    gate = jax.nn.sigmoid(jnp.dot(hp, wpg_ref[...], preferred_element_type=f32))
    pp = jnp.dot(p_ref[...].astype(jnp.bfloat16), wpp_ref[...], preferred_element_type=f32)
    x3 = x2 + gate * pp
    if final:
        var3 = jnp.mean(x3 * x3, axis=-1, keepdims=True)
        x3 = x3 * lax.rsqrt(var3 + EPS) * fn_ref[...]
    o_ref[...] = x3


def _combine_ple(dest3, x1, route, y, p, pn, wpg, wpp, fn, tmc, final):
    t = x1.shape[0]
    n = t // tmc
    return pl.pallas_call(
        functools.partial(_combine_ple_kernel, tmc=tmc, final=final),
        grid=(n,),
        in_specs=[pl.BlockSpec((1, 1, 2 * tmc), lambda i: (i, 0, 0), memory_space=pltpu.SMEM),
                  pl.BlockSpec((1, 1, 2 * tmc), lambda i: (jnp.minimum(i + 1, n - 1), 0, 0), memory_space=pltpu.SMEM),
                  pl.BlockSpec((tmc, D_MODEL), lambda i: (i, 0)),
                  pl.BlockSpec((tmc, ROUTE_LANES), lambda i: (i, 0)),
                  pl.BlockSpec(memory_space=pl.ANY),
                  pl.BlockSpec((tmc, PLE_DIM), lambda i: (i, 0)),
                  pl.BlockSpec((1, D_MODEL), lambda i: (0, 0)),
                  pl.BlockSpec((D_MODEL, D_MODEL), lambda i: (0, 0)),
                  pl.BlockSpec((PLE_DIM, D_MODEL), lambda i: (0, 0)),
                  pl.BlockSpec((1, D_MODEL), lambda i: (0, 0))],
        out_specs=pl.BlockSpec((tmc, D_MODEL), lambda i: (i, 0)),
        out_shape=jax.ShapeDtypeStruct((t, D_MODEL), jnp.float32),
        scratch_shapes=[pltpu.VMEM((2, 2, tmc, HALF), jnp.uint32), pltpu.SemaphoreType.DMA((2,))],
        compiler_params=_cparams(("arbitrary",)),
        name="combine_ple",
    )(dest3, dest3, x1, route, y, p, pn, wpg, wpp, fn)


REGROUP_ROWS = 512


def _regroup_w_in(w):
    n_layers, d, n_in = w.shape
    o_ng = NSA_WIDTH + 6 * KV_WIDTH
    o_conf = o_ng + N_HEADS * N_NSA_BRANCHES
    o_sc = o_conf + 2 * CONF_WIDTH
    o_mg = o_sc + 3 * SC_WIDTH
    src = []
    for z0 in range(0, Z_WIDTH, REGROUP_ROWS):
        for dst, lo in ((OFF_NSAG, o_ng), (OFF_Q, 0), (OFF_CONF, o_conf), (OFF_SC, o_sc), (OFF_MERGE, o_mg)):
            if z0 >= dst:
                src.append(lo + z0 - dst)
                break
    assert all(v % SUBLANE == 0 and v + REGROUP_ROWS <= n_in for v in src)
    grid_spec = pltpu.PrefetchScalarGridSpec(
        num_scalar_prefetch=1,
        grid=(n_layers, Z_WIDTH // REGROUP_ROWS),
        in_specs=[pl.BlockSpec((pl.Element(1), pl.Element(REGROUP_ROWS), pl.Element(d)),
                               lambda l, r, tab: (l, tab[r] * SUBLANE, 0))],
        out_specs=pl.BlockSpec((1, REGROUP_ROWS, d), lambda l, r, tab: (l, r, 0)),
    )
    return pl.pallas_call(
        functools.partial(_regroup_kernel, n_gate=o_conf - o_ng),
        grid_spec=grid_spec,
        out_shape=jax.ShapeDtypeStruct((n_layers, Z_WIDTH, d), jnp.bfloat16),
        compiler_params=_cparams(("parallel", "parallel")),
        name="regroup_w_in",
    )(jnp.asarray(src, jnp.int32) // SUBLANE, jnp.swapaxes(w, 1, 2))


def _regroup_kernel(tab_ref, w_ref, o_ref, *, n_gate):
    del tab_ref
    last = pl.program_id(1) == pl.num_programs(1) - 1
    row = lax.broadcasted_iota(jnp.int32, w_ref.shape[1:], 0)
    keep = jnp.logical_or(jnp.logical_not(last), row < n_gate)
    o_ref[0] = jnp.where(keep, w_ref[0], 0.0).astype(o_ref.dtype)


def _route_plan(route, counts, t, tmd):
    cnt = counts[:N_EXPERTS].astype(jnp.int32)
    padded = ((cnt + TMX - 1) // TMX) * TMX
    ends = jnp.cumsum(padded)
    starts = ends - padded
    e_idx = jnp.arange(N_EXPERTS, dtype=jnp.int32)

    def lookup(table, idx):
        return jnp.sum(jnp.where(idx[..., None] == e_idx, table, 0), axis=-1)

    dest = [lookup(starts, route[:, R_E1 + k].astype(jnp.int32).reshape(t // tmd, tmd))
            + route[:, R_RANK1 + k].astype(jnp.int32).reshape(t // tmd, tmd) for k in range(2)]
    dest3 = jnp.stack(dest, axis=1).reshape(t // tmd, 1, 2 * tmd)
    p_rows = 2 * t + N_EXPERTS * TMX
    n_tiles = p_rows // TMX
    tile_start = jnp.arange(n_tiles, dtype=jnp.int32) * TMX
    tile_e = jnp.minimum(jnp.sum((tile_start[:, None] >= ends[None, :]).astype(jnp.int32), axis=1), N_EXPERTS - 1)
    tile_v = (tile_start < ends[-1]).astype(jnp.int32)
    tile_b = jnp.where(tile_v == 1, jnp.arange(n_tiles, dtype=jnp.int32), 0)
    tile_f = tile_v * jnp.concatenate([jnp.ones((1,), jnp.int32), (tile_e[1:] != tile_e[:-1]).astype(jnp.int32)])
    has = padded > 0
    later = jnp.where(has[None, :] & (e_idx[None, :] > e_idx[:, None]), e_idx[None, :], N_EXPERTS)
    next_e = jnp.min(later, axis=1)
    next_e = jnp.where(next_e < N_EXPERTS, next_e, -1)
    seg_slot = (jnp.cumsum(has.astype(jnp.int32)) - 1) % 2
    tile_meta = jnp.stack([tile_e, tile_v, tile_b, tile_f, lookup(next_e, tile_e), lookup(seg_slot, tile_e)])
    zstart = jnp.concatenate([jnp.where(padded > cnt, ends - TMX, -1), ends[-1:]]).astype(jnp.int32)
    return dest3, tile_meta.astype(jnp.int32), zstart, p_rows


def kernel(x, p, rel_bias, attn_norm, w_in, cmp_pe, cmp_w1, cmp_w2, conf_conv_w, conf_conv_b, conf_ln_g, conf_ln_b, sc_conv_w, w_branch, w_out, ffn_norm, router_group_w, router_group_b, router_expert_w, router_expert_b, expert_w_gate, expert_w_up, expert_w_down, ple_norm, ple_gate_w, ple_proj_w, final_norm):
    b, s, d = x.shape
    t = b * s
    depth = w_in.shape[0]
    bf16 = jnp.bfloat16
    ncp = s // CMP_STRIDE
    rows4 = HEADS_PER_GROUP * TQ

    cmp_tab = _bias_tables(rel_bias, 1, s, min(1024, s), ncp, col_mult=CMP_STRIDE, d0_base=-(CMP_BLOCK - 1), d0_step=0,
                           lo=0, hi=1 << 30, out_scale=1.0)
    toe = _bias_tables(rel_bias, 6, TQ, TQ, LANE, col_mult=1, d0_base=-LANE, d0_step=LANE,
                       lo=0, hi=WINDOW, out_scale=LOG2E)
    toe = toe.reshape(KV_HEADS, HEADS_PER_GROUP, 6, TQ, LANE).transpose(0, 2, 1, 3, 4).reshape(KV_HEADS, 6, rows4, LANE)
    win_tab = jnp.stack([toe[:, N_PREV - jj + 1] for jj in range(N_PREV + 2)], axis=1)
    n_sel_tab = SEL_TQ // SEL_TK + 2
    sel_tab = _bias_tables(rel_bias, n_sel_tab, SEL_TQ, SEL_TQ, SEL_TK, col_mult=1,
                           d0_base=(1 - SEL_TQ // SEL_TK) * SEL_TK, d0_step=SEL_TK, lo=0, hi=1 << 30, out_scale=LOG2E)
    sel_tab = sel_tab.reshape(KV_HEADS, HEADS_PER_GROUP, n_sel_tab, SEL_TQ, SEL_TK).transpose(0, 2, 1, 3, 4)
    sel_tab = sel_tab.reshape(KV_HEADS, n_sel_tab, HEADS_PER_GROUP * SEL_TQ, SEL_TK)

    w_z = _regroup_w_in(w_in)
    x2d = x.reshape(t, d)
    tm_in = min(1024, t)
    tm = min(512, t)
    tmd = min(256, t)
    for i in range(depth):
        n_pad = ROUTE_LANES - N_EXPERTS - N_GROUPS
        wr = jnp.concatenate([router_expert_w[i], router_group_w[i], jnp.zeros((d, n_pad), jnp.float32)], axis=1)
        wr_hi = wr.astype(bf16)
        wr = jnp.concatenate([wr_hi, (wr - wr_hi.astype(jnp.float32)).astype(bf16)], axis=1)
        br = jnp.concatenate([router_expert_b[i], router_group_b[i], jnp.zeros((n_pad,), jnp.float32)]).reshape(1, -1)

        z, kcv = _in_proj(x2d, attn_norm[i].reshape(1, d), w_z, i, tm_in, 2048)
        cmp_kv = _compress(kcv, cmp_pe[i], cmp_w1[i].astype(bf16), cmp_w2[i].astype(bf16), b, s)
        o_cmp, neg_t = _cmp_attn(z, cmp_kv, cmp_tab, b, s)
        neg = jnp.swapaxes(neg_t, 2, 3)
        o_slc = _flash(z, sel_tab, neg, b, s, selected=True)
        o_win = _flash(z, win_tab, None, b, s, selected=False)
        uconv, o_sc = _conv(z, conf_conv_w[i], conf_conv_b[i].reshape(1, -1), sc_conv_w[i], b, s, min(512, s), 256)
        merged = _merge(z, o_cmp, o_slc, o_win, uconv, conf_ln_g[i].reshape(1, -1), conf_ln_b[i].reshape(1, -1),
                        o_sc, w_branch[i].astype(bf16), tm, 1024)

        x1, h2, route, cnts = _out_router(x2d, merged, w_out[i].astype(bf16), ffn_norm[i].reshape(1, d), wr, br, tm)
        dest3, tile_meta, zstart, p_rows = _route_plan(route, cnts[-1], t, tmd)
        xs = _dispatch(zstart, dest3, h2, p_rows, tmd)
        y = _experts(tile_meta, xs, expert_w_gate, expert_w_up, expert_w_down, i)

        x2d = _combine_ple(dest3, x1, route, y, p[i].reshape(t, PLE_DIM), ple_norm[i].reshape(1, d),
                           ple_gate_w[i].astype(bf16), ple_proj_w[i].astype(bf16), final_norm.reshape(1, d),
                           tmd, i == depth - 1)
    return x2d.reshape(b, s, d)
```

```python
import functools
import math

import jax
import jax.numpy as jnp
from jax import lax
from jax.experimental import pallas as pl
from jax.experimental.pallas import tpu as pltpu

D_MODEL = 2048
N_HEADS = 8
HEAD_DIM = 128
KV_HEADS = 2
HEADS_PER_GROUP = N_HEADS // KV_HEADS
NSA_WIDTH = N_HEADS * HEAD_DIM
KV_WIDTH = KV_HEADS * HEAD_DIM
N_NSA_BRANCHES = 3
CMP_BLOCK = 32
CMP_STRIDE = 16
CMP_HIDDEN = 512
SEL_BLOCK = 64
SEL_TOPN = 16
WINDOW = 512
WIN_QBLOCK = 128
CONF_WIDTH = 1024
CONF_CONV_WIDTH = 31
SC_WIDTH = 1024
SC_CONV_WIDTH = 3
REL_BUCKETS = 32
REL_MAX_DIST = 128
N_GROUPS = 4
EXPERTS_PER_GROUP = 8
N_EXPERTS = N_GROUPS * EXPERTS_PER_GROUP
D_EXPERT = 512
PLE_DIM = 256
EPS = 1e-6
NEG_INF = -1e30
FORCE_SCORE = 1e9
SCALE = HEAD_DIM ** -0.5
LOG2E = math.log2(math.e)

LANE = 128
SUBLANE = 8
VMEM_LIMIT = 56 * 1024 * 1024

OFF_MERGE = 0
OFF_SC = OFF_MERGE + 3 * D_MODEL
OFF_CONF = OFF_SC + 3 * SC_WIDTH
OFF_Q = OFF_CONF + 2 * CONF_WIDTH
OFF_KC = OFF_Q + NSA_WIDTH
OFF_VC = OFF_KC + KV_WIDTH
OFF_KS = OFF_VC + KV_WIDTH
OFF_VS = OFF_KS + KV_WIDTH
OFF_KW = OFF_VS + KV_WIDTH
OFF_VW = OFF_KW + KV_WIDTH
OFF_NSAG = OFF_VW + KV_WIDTH
Z_WIDTH = OFF_NSAG + 512

QW = HEADS_PER_GROUP * HEAD_DIM
TQ = 128
CMP_TQ = 256
SEL_TQ = 512
SEL_TK = 256
N_PREV = WINDOW // WIN_QBLOCK
ROUTE_LANES = LANE
TMX = 256


def _cparams(sem, vmem=VMEM_LIMIT):
    return pltpu.CompilerParams(dimension_semantics=sem, vmem_limit_bytes=vmem)


def _t5_bucket(dist):
    n = jnp.maximum(dist, 0)
    max_exact = REL_BUCKETS // 2
    nf = jnp.maximum(n, 1).astype(jnp.float32)
    large = max_exact + (jnp.log(nf / max_exact) / math.log(REL_MAX_DIST / max_exact)
                         * (REL_BUCKETS - max_exact)).astype(jnp.int32)
    large = jnp.minimum(large, REL_BUCKETS - 1)
    return jnp.where(n < max_exact, n, large)


def _bias_table_kernel(rel_ref, o_ref, *, col_mult, d0_base, d0_step, lo, hi, rows, cols, out_scale):
    h = pl.program_id(0)
    k = pl.program_id(1)
    rb = pl.program_id(2)
    cb = pl.program_id(3)
    d0 = d0_base + d0_step * k
    d_min = rb * rows - col_mult * (cb * cols + cols - 1) + d0
    d_max = rb * rows + rows - 1 - col_mult * cb * cols + d0
    masked = jnp.logical_or(d_max < lo, d_min >= hi)
    far = jnp.logical_and(d_min >= REL_MAX_DIST, d_max < hi)

    @pl.when(masked)
    def _():
        o_ref[0, 0] = jnp.full((rows, cols), NEG_INF * out_scale, jnp.float32)

    @pl.when(far)
    def _():
        o_ref[0, 0] = jnp.full((rows, cols), rel_ref[REL_BUCKETS - 1, h] * out_scale, jnp.float32)

    @pl.when(jnp.logical_not(jnp.logical_or(masked, far)))
    def _():
        r = lax.broadcasted_iota(jnp.int32, (rows, cols), 0) + rb * rows
        c = lax.broadcasted_iota(jnp.int32, (rows, cols), 1) + cb * cols
        dist = r - col_mult * c + d0
        bucket = _t5_bucket(dist)
        val = jnp.zeros((rows, cols), jnp.float32)
        for b in range(REL_BUCKETS):
            val = jnp.where(bucket == b, rel_ref[b, h], val)
        ok = jnp.where(dist >= lo, jnp.where(dist < hi, 1.0, 0.0), 0.0)
        o_ref[0, 0] = jnp.where(ok > 0.5, val, NEG_INF) * out_scale


def _bias_tables(rel_bias, n_k, n_rows, rows, n_cols, **kw):
    cols = min(LANE, n_cols)
    kern = functools.partial(_bias_table_kernel, rows=rows, cols=cols, **kw)
    return pl.pallas_call(
        kern,
        grid=(N_HEADS, n_k, n_rows // rows, n_cols // cols),
        in_specs=[pl.BlockSpec(memory_space=pltpu.SMEM)],
        out_specs=pl.BlockSpec((1, 1, rows, cols), lambda h, k, r, c: (h, k, r, c)),
        out_shape=jax.ShapeDtypeStruct((N_HEADS, n_k, n_rows, n_cols), jnp.float32),
        compiler_params=_cparams(("parallel", "parallel", "parallel", "parallel")),
        name="bias_tables",
    )(rel_bias)


def _inproj_kernel(x_ref, g_ref, w_ref, o_ref, kcv_ref, h_scr, *, kcv_tile, kcv_col):
    @pl.when(pl.program_id(1) == 0)
    def _():
        x = x_ref[...]
        var = jnp.mean(x * x, axis=-1, keepdims=True)
        h_scr[...] = (x * lax.rsqrt(var + EPS) * g_ref[...]).astype(jnp.bfloat16)

    acc = lax.dot_general(h_scr[...], w_ref[0], (((1,), (1,)), ((), ())), preferred_element_type=jnp.float32)
    o_ref[...] = acc.astype(o_ref.dtype)

    @pl.when(pl.program_id(1) == kcv_tile)
    def _():
        kcv_ref[...] = acc[:, kcv_col:kcv_col + 2 * KV_WIDTH]


def _in_proj(x, g, w, layer, tm, tn):
    t = x.shape[0]
    assert OFF_KC % tn + 2 * KV_WIDTH <= tn
    return pl.pallas_call(
        functools.partial(_inproj_kernel, kcv_tile=OFF_KC // tn, kcv_col=OFF_KC % tn),
        grid=(t // tm, Z_WIDTH // tn),
        in_specs=[pl.BlockSpec((tm, D_MODEL), lambda i, j: (i, 0)),
                  pl.BlockSpec((1, D_MODEL), lambda i, j: (0, 0)),
                  pl.BlockSpec((1, tn, D_MODEL), lambda i, j: (layer, j, 0))],
        out_specs=[pl.BlockSpec((tm, tn), lambda i, j: (i, j)),
                   pl.BlockSpec((tm, 2 * KV_WIDTH), lambda i, j: (i, 0))],
        out_shape=[jax.ShapeDtypeStruct((t, Z_WIDTH), jnp.bfloat16),
                   jax.ShapeDtypeStruct((t, 2 * KV_WIDTH), jnp.float32)],
        scratch_shapes=[pltpu.VMEM((tm, D_MODEL), jnp.bfloat16)],
        compiler_params=_cparams(("parallel", "arbitrary")),
        name="in_proj",
    )(x, g, w)


def _compress_kernel(a_ref, pe_ref, w1_ref, w2_ref, o_ref, *, ncp):
    lo = jnp.zeros((ncp, CMP_HIDDEN), jnp.float32)
    hi = jnp.zeros((ncp, CMP_HIDDEN), jnp.float32)
    for l in range(CMP_STRIDE):
        a = a_ref[pl.ds(l, ncp, stride=CMP_STRIDE), :]
        a_lo = (a + pe_ref[0, l:l + 1, :]).astype(jnp.bfloat16)
        a_hi = (a + pe_ref[0, CMP_STRIDE + l:CMP_STRIDE + l + 1, :]).astype(jnp.bfloat16)
        lo = lo + jnp.dot(a_lo, w1_ref[0, l * HEAD_DIM:(l + 1) * HEAD_DIM, :],
                          preferred_element_type=jnp.float32)
        hi = hi + jnp.dot(a_hi, w1_ref[0, (CMP_STRIDE + l) * HEAD_DIM:(CMP_STRIDE + l + 1) * HEAD_DIM, :],
                          preferred_element_type=jnp.float32)
    hidden = lo + pltpu.roll(hi, ncp - 1, 0)
    act = jax.nn.gelu(hidden).astype(jnp.bfloat16)
    out = jnp.dot(act, w2_ref[0], preferred_element_type=jnp.float32)
    row = lax.broadcasted_iota(jnp.int32, (ncp, HEAD_DIM), 0)
    o_ref[0, 0] = jnp.where(row < ncp - 1, out, 0.0).astype(o_ref.dtype)


def _compress(a, pe, w1, w2, b, s):
    ncp = s // CMP_STRIDE
    return pl.pallas_call(
        functools.partial(_compress_kernel, ncp=ncp),
        grid=(b, 2 * KV_HEADS),
        in_specs=[pl.BlockSpec((s, HEAD_DIM), lambda i, j: (i, j)),
                  pl.BlockSpec((1, CMP_BLOCK, HEAD_DIM), lambda i, j: (j // KV_HEADS, 0, 0)),
                  pl.BlockSpec((1, CMP_BLOCK * HEAD_DIM, CMP_HIDDEN), lambda i, j: (j // KV_HEADS, 0, 0)),
                  pl.BlockSpec((1, CMP_HIDDEN, HEAD_DIM), lambda i, j: (j // KV_HEADS, 0, 0))],
        out_specs=pl.BlockSpec((1, 1, ncp, HEAD_DIM), lambda i, j: (i, j, 0, 0)),
        out_shape=jax.ShapeDtypeStruct((b, 2 * KV_HEADS, ncp, HEAD_DIM), jnp.bfloat16),
        compiler_params=_cparams(("parallel", "parallel")),
        name="compress",
    )(a, pe, w1, w2)


def _cmp_attn_kernel(q_ref, kc_ref, vc_ref, tab_ref, o_ref, neg_ref, *, tq, ncp, nsel):
    i = pl.program_id(2)
    kc = kc_ref[0, 0]
    vc = vc_ref[0, 0]
    q4 = jnp.concatenate([q_ref[:, h * HEAD_DIM:(h + 1) * HEAD_DIM] for h in range(HEADS_PER_GROUP)], axis=0)
    tab = jnp.concatenate([tab_ref[h, 0] for h in range(HEADS_PER_GROUP)], axis=0)
    s = lax.dot_general(q4, kc, (((1,), (1,)), ((), ())), preferred_element_type=jnp.float32) * SCALE + tab
    valid = tab > 0.5 * NEG_INF
    m = jnp.max(s, axis=-1, keepdims=True)
    e = jnp.where(valid, jnp.exp(s - m), 0.0)
    den = jnp.sum(e, axis=-1, keepdims=True)
    p = jnp.where(valid, e / den, 0.0)
    o = jnp.dot(p.astype(jnp.bfloat16), vc, preferred_element_type=jnp.float32)
    psum = p[0:tq]
    for h in range(HEADS_PER_GROUP):
        o_ref[:, h * HEAD_DIM:(h + 1) * HEAD_DIM] = o[h * tq:(h + 1) * tq].astype(o_ref.dtype)
        if h:
            psum = psum + p[h * tq:(h + 1) * tq]
    blk_n = lax.broadcasted_iota(jnp.int32, (nsel, ncp), 0)
    c_start = lax.broadcasted_iota(jnp.int32, (nsel, ncp), 1) * CMP_STRIDE
    ov = jnp.where(c_start < (blk_n + 1) * SEL_BLOCK,
                   jnp.where(c_start + CMP_BLOCK > blk_n * SEL_BLOCK, 1.0, 0.0), 0.0)
    imp = lax.dot_general(ov, psum, (((1,), (1,)), ((), ())), precision=lax.Precision.HIGHEST,
                          preferred_element_type=jnp.float32)
    blk = lax.broadcasted_iota(jnp.int32, (nsel, tq), 0)
    t = lax.broadcasted_iota(jnp.int32, (nsel, tq), 1) + i * tq
    cur = lax.shift_right_logical(t, 6)
    forced = jnp.where(blk == 0, 1.0, jnp.where(blk == cur, 1.0, jnp.where(blk == cur - 1, 1.0, 0.0)))
    causal = blk * SEL_BLOCK <= t
    score = jnp.where(forced > 0.5, FORCE_SCORE, jnp.where(causal, imp, NEG_INF))
    rank = jnp.zeros((nsel, tq), jnp.float32)
    for k in range(nsel):
        row = score[k:k + 1, :]
        rank = rank + jnp.where(blk > k, jnp.where(row >= score, 1.0, 0.0), jnp.where(row > score, 1.0, 0.0))
    n_top = min(SEL_TOPN, nsel)
    neg = jnp.where(rank < n_top, jnp.where(causal, 0.0, NEG_INF), NEG_INF)
    neg_ref[0, 0] = neg.astype(neg_ref.dtype)


def _cmp_attn(z, cmp_kv, tab, b, s):
    ncp = s // CMP_STRIDE
    nsel = s // SEL_BLOCK
    nq = s // CMP_TQ
    kern = functools.partial(_cmp_attn_kernel, tq=CMP_TQ, ncp=ncp, nsel=nsel)
    return pl.pallas_call(
        kern,
        grid=(b, KV_HEADS, nq),
        in_specs=[pl.BlockSpec((CMP_TQ, QW), lambda bi, g, i: (bi * nq + i, OFF_Q // QW + g)),
                  pl.BlockSpec((1, 1, ncp, HEAD_DIM), lambda bi, g, i: (bi, g, 0, 0)),
                  pl.BlockSpec((1, 1, ncp, HEAD_DIM), lambda bi, g, i: (bi, KV_HEADS + g, 0, 0)),
                  pl.BlockSpec((HEADS_PER_GROUP, 1, CMP_TQ, ncp), lambda bi, g, i: (g, 0, i, 0))],
        out_specs=[pl.BlockSpec((CMP_TQ, QW), lambda bi, g, i: (bi * nq + i, g)),
                   pl.BlockSpec((1, 1, nsel, CMP_TQ), lambda bi, g, i: (bi, g, 0, i))],
        out_shape=[jax.ShapeDtypeStruct((b * s, NSA_WIDTH), jnp.bfloat16),
                   jax.ShapeDtypeStruct((b, KV_HEADS, nsel, s), jnp.bfloat16)],
        compiler_params=_cparams(("parallel", "parallel", "parallel")),
        name="cmp_attn",
    )(z, cmp_kv, cmp_kv, tab)


def _stack_heads(q_ref, g):
    q = jnp.concatenate(
        [q_ref[:, (g * HEADS_PER_GROUP + h) * HEAD_DIM:(g * HEADS_PER_GROUP + h + 1) * HEAD_DIM]
         for h in range(HEADS_PER_GROUP)], axis=0)
    return (q.astype(jnp.float32) * (SCALE * LOG2E)).astype(jnp.bfloat16)


def _unstack_heads(o_ref, g, out, tq):
    for h in range(HEADS_PER_GROUP):
        c0 = (g * HEADS_PER_GROUP + h) * HEAD_DIM
        o_ref[:, c0:c0 + HEAD_DIM] = out[h * tq:(h + 1) * tq].astype(o_ref.dtype)


def _sel_kernel(q_ref, k_ref, v_ref, tab_ref, neg_ref, o_ref, m0_scr, m1_scr, acc0_scr, acc1_scr, *, tq, tk, nsel):
    i = pl.program_id(1)
    sub = tk // LANE
    pad = jnp.zeros((HEADS_PER_GROUP * tq, HEAD_DIM - nsel), jnp.bfloat16)
    q4 = [jnp.concatenate([_stack_heads(q_ref, g), jnp.concatenate([neg_ref[0, g]] * HEADS_PER_GROUP, axis=0), pad],
                          axis=1) for g in range(KV_HEADS)]
    kpad = jnp.zeros((tk, HEAD_DIM - nsel), jnp.bfloat16)
    m_scr = (m0_scr, m1_scr)
    acc_scr = (acc0_scr, acc1_scr)
    for g in range(KV_HEADS):
        m_scr[g][...] = jnp.full(m_scr[g].shape, NEG_INF, jnp.float32)
        acc_scr[g][...] = jnp.zeros(acc_scr[g].shape, jnp.float32)
    u_min = 1 - tq // tk
    n_far = tab_ref.shape[1] - 1
    ones = jnp.ones((tk, LANE), jnp.bfloat16)

    def key_tile(j):
        tidx = jnp.minimum(i * (tq // tk) - j - u_min, n_far)
        k_i = lax.broadcasted_iota(jnp.int32, (tk, nsel), 0)
        n_i = lax.broadcasted_iota(jnp.int32, (tk, nsel), 1)
        onehot = jnp.where(lax.shift_right_logical(k_i + j * tk, 6) == n_i, 1.0, 0.0).astype(jnp.bfloat16)
        start = pl.multiple_of(j * tk, tk)
        for g in range(KV_HEADS):
            gs = slice(g * HEAD_DIM, (g + 1) * HEAD_DIM)
            kt = jnp.concatenate([k_ref[pl.ds(start, tk), gs], onehot, kpad], axis=1)
            vt = jnp.concatenate([v_ref[pl.ds(start, tk), gs], ones], axis=1)
            s = lax.dot_general(q4[g], kt, (((1,), (1,)), ((), ())), preferred_element_type=jnp.float32)
            s = s + tab_ref[g, tidx]
            m_prev = m_scr[g][...]
            m_new = jnp.maximum(m_prev, jnp.max(s, axis=-1, keepdims=True))
            alpha = jnp.exp2(m_prev - m_new)
            p = jnp.exp2(s - jnp.concatenate([m_new] * sub, axis=1))
            acc_scr[g][...] = (jnp.concatenate([alpha, alpha], axis=1) * acc_scr[g][...]
                               + jnp.dot(p.astype(jnp.bfloat16), vt, preferred_element_type=jnp.float32))
            m_scr[g][...] = m_new

    per_trip = tq // tk

    def body(jj, carry):
        for jo in range(per_trip):
            key_tile(jj * per_trip + jo)
        return carry

    lax.fori_loop(0, i + 1, body, 0)
    for g in range(KV_HEADS):
        _unstack_heads(o_ref, g, acc_scr[g][:, 0:HEAD_DIM] / acc_scr[g][:, HEAD_DIM:2 * HEAD_DIM], tq)


def _win_kernel(q_ref, k_ref, v_ref, tab_ref, o_ref, *, tq):
    i = pl.program_id(1)
    nb = N_PREV + 1
    sb = jnp.maximum(i - N_PREV, 0)
    off = sb - i + N_PREV
    start = pl.multiple_of(sb * WIN_QBLOCK, WIN_QBLOCK)
    ones = jnp.ones((nb * WIN_QBLOCK, LANE), jnp.bfloat16)
    for g in range(KV_HEADS):
        gs = slice(g * HEAD_DIM, (g + 1) * HEAD_DIM)
        kt = k_ref[pl.ds(start, nb * WIN_QBLOCK), gs]
        vt = jnp.concatenate([v_ref[pl.ds(start, nb * WIN_QBLOCK), gs], ones], axis=1)
        s = lax.dot_general(_stack_heads(q_ref, g), kt, (((1,), (1,)), ((), ())),
                            preferred_element_type=jnp.float32)
        s = s + jnp.concatenate([tab_ref[g, jnp.minimum(u + off, nb)] for u in range(nb)], axis=1)
        p = jnp.exp2(s - jnp.max(s, axis=-1, keepdims=True))
        pv = jnp.dot(p.astype(jnp.bfloat16), vt, preferred_element_type=jnp.float32)
        _unstack_heads(o_ref, g, pv[:, 0:HEAD_DIM] / pv[:, HEAD_DIM:2 * HEAD_DIM], tq)


def _flash(z, tab, neg, b, s, *, selected):
    tq = SEL_TQ if selected else TQ
    nq = s // tq
    tk = SEL_TK if selected else WIN_QBLOCK
    k_off, v_off = (OFF_KS, OFF_VS) if selected else (OFF_KW, OFF_VW)
    rows = HEADS_PER_GROUP * tq
    n_tab = tab.shape[1]
    in_specs = [pl.BlockSpec((tq, NSA_WIDTH), lambda bi, i: (bi * nq + i, OFF_Q // NSA_WIDTH)),
                pl.BlockSpec((s, KV_WIDTH), lambda bi, i: (bi, k_off // KV_WIDTH)),
                pl.BlockSpec((s, KV_WIDTH), lambda bi, i: (bi, v_off // KV_WIDTH)),
                pl.BlockSpec((KV_HEADS, n_tab, rows, tk), lambda bi, i: (0, 0, 0, 0),
                             pipeline_mode=pl.Buffered(1))]
    args = [z, z, z, tab]
    scratch = []
    if selected:
        nsel = s // SEL_BLOCK
        in_specs.append(pl.BlockSpec((1, KV_HEADS, tq, nsel), lambda bi, i: (bi, 0, i, 0)))
        args.append(neg)
        kern = functools.partial(_sel_kernel, tq=tq, tk=tk, nsel=nsel)
        scratch = ([pltpu.VMEM((rows, LANE), jnp.float32)] * KV_HEADS
                   + [pltpu.VMEM((rows, 2 * HEAD_DIM), jnp.float32)] * KV_HEADS)
    else:
        kern = functools.partial(_win_kernel, tq=tq)
    return pl.pallas_call(
        kern,
        grid=(b, nq),
        in_specs=in_specs,
        out_specs=pl.BlockSpec((tq, NSA_WIDTH), lambda bi, i: (bi * nq + i, 0)),
        out_shape=jax.ShapeDtypeStruct((b * s, NSA_WIDTH), jnp.bfloat16),
        scratch_shapes=scratch,
        compiler_params=_cparams(("parallel", "parallel")),
        name="flash_sel" if selected else "flash_win",
    )(*args)


HALO = 32
CONV_RC = 64


def _conv_kernel(a_ref, g_ref, ah_ref, gh_ref, bg_ref, cg_ref, xs_ref, cgh_ref, xsh_ref,
                 cw_ref, cb_ref, sw_ref, uo_ref, so_ref, ext_scr, ext2_scr, *, ts):
    first = pl.program_id(1) == 0
    f32 = jnp.float32
    n_ext = HALO + ts
    n_sh = n_ext - SUBLANE
    u = a_ref[...].astype(f32) * jax.nn.sigmoid(g_ref[...].astype(f32))
    uh = ah_ref[...].astype(f32) * jax.nn.sigmoid(gh_ref[...].astype(f32))
    ext_scr[0, 0:HALO, :] = jnp.where(first, 0.0, uh)
    ext_scr[0, HALO:n_ext, :] = u
    v = cg_ref[...].astype(f32) * xs_ref[...].astype(f32)
    vh = cgh_ref[...].astype(f32) * xsh_ref[...].astype(f32)
    ext2_scr[0, 0:HALO, :] = jnp.where(first, 0.0, vh)
    ext2_scr[0, HALO:n_ext, :] = v
    base = HALO - (CONF_CONV_WIDTH - 1)
    base2 = HALO - (SC_CONV_WIDTH - 1)
    for r in range(1, SUBLANE):
        ext_scr[r, 0:n_sh, :] = ext_scr[0, r:r + n_sh, :]
    sc_shifts = sorted({(base2 + k) % SUBLANE for k in range(SC_CONV_WIDTH)} - {0})
    for r in sc_shifts:
        ext2_scr[r, 0:n_sh, :] = ext2_scr[0, r:r + n_sh, :]

    def tap(scr, off, r0):
        r = off % SUBLANE
        return scr[r, r0 + off - r:r0 + off - r + CONV_RC, :]

    for r0 in range(0, ts, CONV_RC):
        acc = jnp.zeros((CONV_RC, a_ref.shape[1]), f32) + cb_ref[...]
        for k in range(CONF_CONV_WIDTH):
            acc = acc + cw_ref[k:k + 1, :] * tap(ext_scr, base + k, r0)
        uo_ref[r0:r0 + CONV_RC, :] = acc.astype(uo_ref.dtype)
        acc2 = jnp.zeros((CONV_RC, a_ref.shape[1]), f32)
        for k in range(SC_CONV_WIDTH):
            acc2 = acc2 + sw_ref[k:k + 1, :] * tap(ext2_scr, base2 + k, r0)
        so_ref[r0:r0 + CONV_RC, :] = (bg_ref[r0:r0 + CONV_RC, :].astype(f32) * acc2).astype(so_ref.dtype)


def _conv(z, cw, cb, sw, b, s, ts, tc):
    ns = s // ts
    t = b * s

    def cur(off):
        return pl.BlockSpec((ts, tc), lambda bi, i, c: (bi * ns + i, off // tc + c))

    def halo(off):
        return pl.BlockSpec((HALO, tc), lambda bi, i, c: (jnp.maximum((bi * s + i * ts) // HALO - 1, 0), off // tc + c))

    return pl.pallas_call(
        functools.partial(_conv_kernel, ts=ts),
        grid=(b, ns, CONF_WIDTH // tc),
        in_specs=[cur(OFF_CONF), cur(OFF_CONF + CONF_WIDTH), halo(OFF_CONF), halo(OFF_CONF + CONF_WIDTH),
                  cur(OFF_SC), cur(OFF_SC + SC_WIDTH), cur(OFF_SC + 2 * SC_WIDTH),
                  halo(OFF_SC + SC_WIDTH), halo(OFF_SC + 2 * SC_WIDTH),
                  pl.BlockSpec((CONF_CONV_WIDTH, tc), lambda bi, i, c: (0, c)),
                  pl.BlockSpec((1, tc), lambda bi, i, c: (0, c)),
                  pl.BlockSpec((SC_CONV_WIDTH, tc), lambda bi, i, c: (0, c))],
        out_specs=[pl.BlockSpec((ts, tc), lambda bi, i, c: (bi * ns + i, c)),
                   pl.BlockSpec((ts, tc), lambda bi, i, c: (bi * ns + i, c))],
        out_shape=[jax.ShapeDtypeStruct((t, CONF_WIDTH), jnp.bfloat16),
                   jax.ShapeDtypeStruct((t, SC_WIDTH), jnp.bfloat16)],
        scratch_shapes=[pltpu.VMEM((SUBLANE, HALO + ts, tc), jnp.float32),
                        pltpu.VMEM((SUBLANE, HALO + ts, tc), jnp.float32)],
        compiler_params=_cparams(("parallel", "parallel", "parallel")),
        name="conv",
    )(z, z, z, z, z, z, z, z, z, cw, cb, sw)


def _merge_kernel(ocmp_ref, oslc_ref, owin_ref, ng_ref, uc_ref, lng_ref, lnb_ref, osc_ref,
                  mg0_ref, mg1_ref, mg2_ref, wn_ref, wc_ref, ws_ref, o_ref, nsa_scr, conf_scr):
    f32 = jnp.float32

    @pl.when(pl.program_id(1) == 0)
    def _():
        gt = jax.nn.sigmoid(ng_ref[...].astype(f32))
        for h in range(N_HEADS):
            sl = slice(h * HEAD_DIM, (h + 1) * HEAD_DIM)
            c = N_NSA_BRANCHES * h
            o = (gt[:, c:c + 1] * ocmp_ref[:, sl].astype(f32) + gt[:, c + 1:c + 2] * oslc_ref[:, sl].astype(f32)
                 + gt[:, c + 2:c + 3] * owin_ref[:, sl].astype(f32))
            nsa_scr[:, sl] = o.astype(nsa_scr.dtype)
        u = uc_ref[...].astype(f32)
        mu = jnp.mean(u, axis=-1, keepdims=True)
        var = jnp.mean(jnp.square(u - mu), axis=-1, keepdims=True)
        y = (u - mu) * lax.rsqrt(var + EPS) * lng_ref[...] + lnb_ref[...]
        conf_scr[...] = (y * jax.nn.sigmoid(y)).astype(conf_scr.dtype)

    a = jnp.dot(nsa_scr[...], wn_ref[...], preferred_element_type=f32)
    b = jnp.dot(conf_scr[...], wc_ref[...], preferred_element_type=f32)
    c = jnp.dot(osc_ref[...], ws_ref[...], preferred_element_type=f32)
    m = (jax.nn.sigmoid(mg0_ref[...].astype(f32)) * a + jax.nn.sigmoid(mg1_ref[...].astype(f32)) * b
         + jax.nn.sigmoid(mg2_ref[...].astype(f32)) * c)
    o_ref[...] = m.astype(o_ref.dtype)


def _merge(z, ocmp, oslc, owin, uconv, lng, lnb, osc, wb, tm, tn):
    t = z.shape[0]
    nj = D_MODEL // tn
    row = lambda w: pl.BlockSpec((tm, w), lambda i, j: (i, 0))
    return pl.pallas_call(
        _merge_kernel,
        grid=(t // tm, nj),
        in_specs=[row(NSA_WIDTH), row(NSA_WIDTH), row(NSA_WIDTH),
                  pl.BlockSpec((tm, LANE), lambda i, j: (i, OFF_NSAG // LANE)),
                  row(CONF_WIDTH),
                  pl.BlockSpec((1, CONF_WIDTH), lambda i, j: (0, 0)),
                  pl.BlockSpec((1, CONF_WIDTH), lambda i, j: (0, 0)),
                  row(SC_WIDTH),
                  pl.BlockSpec((tm, tn), lambda i, j: (i, j)),
                  pl.BlockSpec((tm, tn), lambda i, j: (i, nj + j)),
                  pl.BlockSpec((tm, tn), lambda i, j: (i, 2 * nj + j)),
                  pl.BlockSpec((NSA_WIDTH, tn), lambda i, j: (0, j)),
                  pl.BlockSpec((CONF_WIDTH, tn), lambda i, j: (NSA_WIDTH // CONF_WIDTH, j)),
                  pl.BlockSpec((SC_WIDTH, tn), lambda i, j: ((NSA_WIDTH + CONF_WIDTH) // SC_WIDTH, j))],
        out_specs=pl.BlockSpec((tm, tn), lambda i, j: (i, j)),
        out_shape=jax.ShapeDtypeStruct((t, D_MODEL), jnp.bfloat16),
        scratch_shapes=[pltpu.VMEM((tm, NSA_WIDTH), jnp.bfloat16), pltpu.VMEM((tm, CONF_WIDTH), jnp.bfloat16)],
        compiler_params=_cparams(("parallel", "arbitrary")),
        name="merge",
    )(ocmp, oslc, owin, z, uconv, lng, lnb, osc, z, z, z, wb, wb, wb)


R_E1, R_E2, R_W1, R_W2, R_RANK1, R_RANK2 = range(6)
GROUP_LANE0 = N_EXPERTS


def _out_router_kernel(x_ref, m_ref, wo_ref, fn_ref, wr_ref, br_ref, x1_ref, h2_ref, route_ref, cnt_ref,
                       base_scr, *, tm):
    f32 = jnp.float32

    @pl.when(pl.program_id(0) == 0)
    def _():
        base_scr[...] = jnp.zeros_like(base_scr)

    x1 = x_ref[...] + jnp.dot(m_ref[...], wo_ref[...], preferred_element_type=f32)
    x1_ref[...] = x1
    var = jnp.mean(x1 * x1, axis=-1, keepdims=True)
    h2 = x1 * lax.rsqrt(var + EPS) * fn_ref[...]
    h2_ref[...] = _pack_bf16_pairs(h2)
    h_hi = h2.astype(jnp.bfloat16)
    h_lo = (h2 - h_hi.astype(f32)).astype(jnp.bfloat16)
    l_hi = jnp.dot(h_hi, wr_ref[...], preferred_element_type=f32)
    l_lo = jnp.dot(h_lo, wr_ref[:, 0:ROUTE_LANES], preferred_element_type=f32)
    logits = l_hi[:, 0:ROUTE_LANES] + l_hi[:, ROUTE_LANES:2 * ROUTE_LANES] + l_lo + br_ref[...]
    lane = lax.broadcasted_iota(jnp.int32, (tm, ROUTE_LANES), 1).astype(f32)
    big = float(ROUTE_LANES)
    is_g = jnp.where(lane >= GROUP_LANE0, jnp.where(lane < GROUP_LANE0 + N_GROUPS, 1.0, 0.0), 0.0) > 0.5
    gl = jnp.where(is_g, logits, NEG_INF)
    gmax = jnp.max(gl, axis=-1, keepdims=True)
    glane = jnp.min(jnp.where(gl == gmax, lane, big), axis=-1, keepdims=True)
    gsum = jnp.sum(jnp.where(is_g, jnp.exp(gl - gmax), 0.0), axis=-1, keepdims=True)
    g_w = 1.0 / gsum
    grp = glane - GROUP_LANE0
    in_grp = jnp.floor(lane * (1.0 / EXPERTS_PER_GROUP)) == grp
    el = jnp.where(in_grp, logits, NEG_INF)
    emax = jnp.max(el, axis=-1, keepdims=True)
    ee = jnp.where(in_grp, jnp.exp(el - emax), 0.0)
    ep = ee / jnp.sum(ee, axis=-1, keepdims=True)
    ep = jnp.where(in_grp, ep, -1.0)
    p1 = jnp.max(ep, axis=-1, keepdims=True)
    i1 = jnp.min(jnp.where(ep == p1, lane, big), axis=-1, keepdims=True)
    ep2 = jnp.where(lane == i1, -1.0, ep)
    p2 = jnp.max(ep2, axis=-1, keepdims=True)
    i2 = jnp.min(jnp.where(ep2 == p2, lane, big), axis=-1, keepdims=True)
    psum = p1 + p2
    w1 = g_w * (p1 / psum)
    w2 = g_w * (p2 / psum)
    onehot = jnp.where(lane == i1, 1.0, jnp.where(lane == i2, 1.0, 0.0))
    r_i = lax.broadcasted_iota(jnp.int32, (tm, tm), 0)
    c_i = lax.broadcasted_iota(jnp.int32, (tm, tm), 1)
    tri = jnp.where(c_i < r_i, 1.0, 0.0).astype(jnp.bfloat16)
    cum = jnp.dot(tri, onehot.astype(jnp.bfloat16), preferred_element_type=f32) + base_scr[0:1, :]
    rank1 = jnp.sum(jnp.where(lane == i1, cum, 0.0), axis=-1, keepdims=True)
    rank2 = jnp.sum(jnp.where(lane == i2, cum, 0.0), axis=-1, keepdims=True)
    new_base = base_scr[0:1, :] + jnp.sum(onehot, axis=0, keepdims=True)
    base_scr[...] = jnp.broadcast_to(new_base, base_scr.shape)
    cnt_ref[...] = jnp.broadcast_to(new_base, cnt_ref.shape)
    rec = jnp.zeros((tm, ROUTE_LANES), f32)
    for ln, val in ((R_E1, i1), (R_E2, i2), (R_W1, w1), (R_W2, w2), (R_RANK1, rank1), (R_RANK2, rank2)):
        rec = jnp.where(lane == ln, val, rec)
    route_ref[...] = rec


def _out_router(x, merged, wo, fn, wr, br, tm):
    t = x.shape[0]
    n = t // tm
    return pl.pallas_call(
        functools.partial(_out_router_kernel, tm=tm),
        grid=(n,),
        in_specs=[pl.BlockSpec((tm, D_MODEL), lambda i: (i, 0)),
                  pl.BlockSpec((tm, D_MODEL), lambda i: (i, 0)),
                  pl.BlockSpec((D_MODEL, D_MODEL), lambda i: (0, 0)),
                  pl.BlockSpec((1, D_MODEL), lambda i: (0, 0)),
                  pl.BlockSpec((D_MODEL, 2 * ROUTE_LANES), lambda i: (0, 0)),
                  pl.BlockSpec((1, ROUTE_LANES), lambda i: (0, 0))],
        out_specs=[pl.BlockSpec((tm, D_MODEL), lambda i: (i, 0)),
                   pl.BlockSpec((tm, D_MODEL // 2), lambda i: (i, 0)),
                   pl.BlockSpec((tm, ROUTE_LANES), lambda i: (i, 0)),
                   pl.BlockSpec((8, ROUTE_LANES), lambda i: (i, 0))],
        out_shape=[jax.ShapeDtypeStruct((t, D_MODEL), jnp.float32),
                   jax.ShapeDtypeStruct((t, D_MODEL // 2), jnp.uint32),
                   jax.ShapeDtypeStruct((t, ROUTE_LANES), jnp.float32),
                   jax.ShapeDtypeStruct((n * 8, ROUTE_LANES), jnp.float32)],
        scratch_shapes=[pltpu.VMEM((8, ROUTE_LANES), jnp.float32)],
        compiler_params=_cparams(("arbitrary",)),
        name="out_router",
    )(x, merged, wo, fn, wr, br)


HALF = D_MODEL // 2
DMA_UNROLL = 8


def _pack_bf16_pairs(x):
    lo = pltpu.bitcast(x[:, 0:HALF].astype(jnp.bfloat16).astype(jnp.float32), jnp.uint32)
    hi = pltpu.bitcast(x[:, HALF:D_MODEL].astype(jnp.bfloat16).astype(jnp.float32), jnp.uint32)
    return hi | (lo >> 16)


def _unpack_bf16_pairs(w):
    lo = pltpu.bitcast(w << 16, jnp.float32)
    hi = pltpu.bitcast(w & jnp.uint32(0xFFFF0000), jnp.float32)
    return lo, hi


def _dispatch_kernel(zs_ref, dest_ref, h_ref, xs_ref, zero_scr, sem, zsem, *, tmd):
    @pl.when(pl.program_id(0) == 0)
    def _():
        zero_scr[...] = jnp.zeros_like(zero_scr)
        for e in range(N_EXPERTS):
            @pl.when(zs_ref[e] >= 0)
            def _():
                cp = pltpu.make_async_copy(zero_scr, xs_ref.at[pl.ds(pl.multiple_of(zs_ref[e], TMX), TMX)], zsem)
                cp.start()
                cp.wait()

        def zero_tail(tile, c):
            cp = pltpu.make_async_copy(zero_scr, xs_ref.at[pl.ds(pl.multiple_of(tile * TMX, TMX), TMX)], zsem)
            cp.start()
            cp.wait()
            return c

        lax.fori_loop(zs_ref[N_EXPERTS] // TMX, xs_ref.shape[0] // TMX, zero_tail, 0)

    def row_copy(r, k):
        return pltpu.make_async_copy(h_ref.at[pl.ds(r, 1)], xs_ref.at[pl.ds(dest_ref[0, 0, k * tmd + r], 1)], sem)

    def issue(r, c):
        row_copy(r, 0).start(priority=0)
        row_copy(r, 1).start(priority=1)
        return c

    lax.fori_loop(0, tmd, issue, 0, unroll=DMA_UNROLL)

    def drain(r, c):
        row_copy(r, 0).wait()
        row_copy(r, 1).wait()
        return c

    lax.fori_loop(0, tmd, drain, 0, unroll=DMA_UNROLL)


def _dispatch(zstart, dest3, h2p, p_rows, tmd):
    t = h2p.shape[0]
    grid_spec = pltpu.PrefetchScalarGridSpec(
        num_scalar_prefetch=1,
        grid=(t // tmd,),
        in_specs=[pl.BlockSpec((1, 1, 2 * tmd), lambda i, zs: (i, 0, 0), memory_space=pltpu.SMEM),
                  pl.BlockSpec((tmd, HALF), lambda i, zs: (i, 0))],
        out_specs=pl.BlockSpec(memory_space=pl.ANY),
        scratch_shapes=[pltpu.VMEM((TMX, HALF), jnp.uint32), pltpu.SemaphoreType.DMA(()),
                        pltpu.SemaphoreType.DMA(())],
    )
    return pl.pallas_call(
        functools.partial(_dispatch_kernel, tmd=tmd),
        grid_spec=grid_spec,
        out_shape=jax.ShapeDtypeStruct((p_rows, HALF), jnp.uint32),
        compiler_params=_cparams(("arbitrary",)),
        name="dispatch",
    )(zstart, dest3, h2p)


TM_EXPERT, TM_VALID, TM_BLOCK, TM_FIRST, TM_NEXT, TM_SLOT = range(6)


def _expert_kernel(tm_ref, xs_ref, wg_hbm, wu_hbm, wd_hbm, y_ref, wg_buf, wu_buf, wd_buf, wgu_scr, wd_scr, sems,
                   *, layer):
    i = pl.program_id(0)
    e = tm_ref[TM_EXPERT, i]
    slot = tm_ref[TM_SLOT, i]

    def weight_copies(expert, s):
        return (pltpu.make_async_copy(wg_hbm.at[layer, expert], wg_buf.at[s], sems.at[s, 0]),
                pltpu.make_async_copy(wu_hbm.at[layer, expert], wu_buf.at[s], sems.at[s, 1]),
                pltpu.make_async_copy(wd_hbm.at[layer, expert], wd_buf.at[s], sems.at[s, 2]))

    @pl.when(tm_ref[TM_FIRST, i] == 1)
    def _():
        @pl.when(i == 0)
        def _():
            for cp in weight_copies(e, slot):
                cp.start()

        for cp in weight_copies(e, slot):
            cp.wait()

        @pl.when(tm_ref[TM_NEXT, i] >= 0)
        def _():
            for cp in weight_copies(tm_ref[TM_NEXT, i], 1 - slot):
                cp.start()

        wgu_scr[:, 0:D_EXPERT] = wg_buf[slot].astype(jnp.bfloat16)
        wgu_scr[:, D_EXPERT:2 * D_EXPERT] = wu_buf[slot].astype(jnp.bfloat16)
        wd_scr[...] = wd_buf[slot].astype(jnp.bfloat16)

    @pl.when(tm_ref[TM_VALID, i] == 1)
    def _():
        x_lo, x_hi = _unpack_bf16_pairs(xs_ref[...])
        gu = (jnp.dot(x_lo.astype(jnp.bfloat16), wgu_scr[0:HALF, :], preferred_element_type=jnp.float32)
              + jnp.dot(x_hi.astype(jnp.bfloat16), wgu_scr[HALF:D_MODEL, :], preferred_element_type=jnp.float32))
        gate = gu[:, 0:D_EXPERT]
        he = (gate * jax.nn.sigmoid(gate)) * gu[:, D_EXPERT:2 * D_EXPERT]
        y = jnp.dot(he.astype(jnp.bfloat16), wd_scr[...], preferred_element_type=jnp.float32)
        y_ref[...] = _pack_bf16_pairs(y)

    @pl.when(tm_ref[TM_VALID, i] == 0)
    def _():
        y_ref[...] = jnp.zeros_like(y_ref)


def _experts(tile_meta, xs, wg, wu, wd, layer):
    p_rows = xs.shape[0]
    f32 = jnp.float32
    grid_spec = pltpu.PrefetchScalarGridSpec(
        num_scalar_prefetch=1,
        grid=(p_rows // TMX,),
        in_specs=[pl.BlockSpec((TMX, HALF), lambda i, tm: (tm[TM_BLOCK, i], 0)),
                  pl.BlockSpec(memory_space=pl.ANY),
                  pl.BlockSpec(memory_space=pl.ANY),
                  pl.BlockSpec(memory_space=pl.ANY)],
        out_specs=pl.BlockSpec((TMX, HALF), lambda i, tm: (i, 0)),
        scratch_shapes=[pltpu.VMEM((2, D_MODEL, D_EXPERT), f32), pltpu.VMEM((2, D_MODEL, D_EXPERT), f32),
                        pltpu.VMEM((2, D_EXPERT, D_MODEL), f32),
                        pltpu.VMEM((D_MODEL, 2 * D_EXPERT), jnp.bfloat16),
                        pltpu.VMEM((D_EXPERT, D_MODEL), jnp.bfloat16),
                        pltpu.SemaphoreType.DMA((2, 3))],
    )
    return pl.pallas_call(
        functools.partial(_expert_kernel, layer=layer),
        grid_spec=grid_spec,
        out_shape=jax.ShapeDtypeStruct((p_rows, HALF), jnp.uint32),
        compiler_params=_cparams(("arbitrary",)),
        name="experts",
    )(tile_meta, xs, wg, wu, wd)


def _combine_ple_kernel(dcur_ref, dnext_ref, x1_ref, route_ref, y_ref, p_ref, pn_ref, wpg_ref, wpp_ref, fn_ref,
                        o_ref, buf, sems, *, tmc, final):
    f32 = jnp.float32
    i = pl.program_id(0)
    slot = lax.rem(i, 2)

    def row_copy(d_ref, s, r, k):
        return pltpu.make_async_copy(y_ref.at[pl.ds(d_ref[0, 0, k * tmc + r], 1)], buf.at[s, k, pl.ds(r, 1)],
                                     sems.at[s])

    def issue(d_ref, s):
        def body(r, c):
            row_copy(d_ref, s, r, 0).start(priority=0)
            row_copy(d_ref, s, r, 1).start(priority=1)
            return c
        lax.fori_loop(0, tmc, body, 0, unroll=DMA_UNROLL)

    @pl.when(i == 0)
    def _():
        issue(dcur_ref, 0)

    @pl.when(i + 1 < pl.num_programs(0))
    def _():
        issue(dnext_ref, 1 - slot)

    def drain(r, c):
        row_copy(dcur_ref, slot, r, 0).wait()
        row_copy(dcur_ref, slot, r, 1).wait()
        return c

    lax.fori_loop(0, tmc, drain, 0, unroll=DMA_UNROLL)

    route = route_ref[...]
    w1 = route[:, R_W1:R_W1 + 1]
    w2 = route[:, R_W2:R_W2 + 1]
    y1_lo, y1_hi = _unpack_bf16_pairs(buf[slot, 0])
    y2_lo, y2_hi = _unpack_bf16_pairs(buf[slot, 1])
    x2 = x1_ref[...] + jnp.concatenate([w1 * y1_lo + w2 * y2_lo, w1 * y1_hi + w2 * y2_hi], axis=1)
    var = jnp.mean(x2 * x2, axis=-1, keepdims=True)
    hp = (x2 * lax.rsqrt(var + EPS) * pn_ref[...]).astype(jnp.bfloat16)
    gate = jax.nn.sigmoid(jnp.dot(hp, wpg_ref[...], preferred_element_type=f32))
    pp = jnp.dot(p_ref[...].astype(jnp.bfloat16), wpp_ref[...], preferred_element_type=f32)
    x3 = x2 + gate * pp
    if final:
        var3 = jnp.mean(x3 * x3, axis=-1, keepdims=True)
        x3 = x3 * lax.rsqrt(var3 + EPS) * fn_ref[...]
    o_ref[...] = x3


def _combine_ple(dest3, x1, route, y, p, pn, wpg, wpp, fn, tmc, final):
    t = x1.shape[0]
    n = t // tmc
    return pl.pallas_call(
        functools.partial(_combine_ple_kernel, tmc=tmc, final=final),
        grid=(n,),
        in_specs=[pl.BlockSpec((1, 1, 2 * tmc), lambda i: (i, 0, 0), memory_space=pltpu.SMEM),
                  pl.BlockSpec((1, 1, 2 * tmc), lambda i: (jnp.minimum(i + 1, n - 1), 0, 0), memory_space=pltpu.SMEM),
                  pl.BlockSpec((tmc, D_MODEL), lambda i: (i, 0)),
                  pl.BlockSpec((tmc, ROUTE_LANES), lambda i: (i, 0)),
                  pl.BlockSpec(memory_space=pl.ANY),
                  pl.BlockSpec((tmc, PLE_DIM), lambda i: (i, 0)),
                  pl.BlockSpec((1, D_MODEL), lambda i: (0, 0)),
                  pl.BlockSpec((D_MODEL, D_MODEL), lambda i: (0, 0)),
                  pl.BlockSpec((PLE_DIM, D_MODEL), lambda i: (0, 0)),
                  pl.BlockSpec((1, D_MODEL), lambda i: (0, 0))],
        out_specs=pl.BlockSpec((tmc, D_MODEL), lambda i: (i, 0)),
        out_shape=jax.ShapeDtypeStruct((t, D_MODEL), jnp.float32),
        scratch_shapes=[pltpu.VMEM((2, 2, tmc, HALF), jnp.uint32), pltpu.SemaphoreType.DMA((2,))],
        compiler_params=_cparams(("arbitrary",)),
        name="combine_ple",
    )(dest3, dest3, x1, route, y, p, pn, wpg, wpp, fn)


REGROUP_ROWS = 512


def _regroup_w_in(w):
    n_layers, d, n_in = w.shape
    o_ng = NSA_WIDTH + 6 * KV_WIDTH
    o_conf = o_ng + N_HEADS * N_NSA_BRANCHES
    o_sc = o_conf + 2 * CONF_WIDTH
    o_mg = o_sc + 3 * SC_WIDTH
    src = []
    for z0 in range(0, Z_WIDTH, REGROUP_ROWS):
        for dst, lo in ((OFF_NSAG, o_ng), (OFF_Q, 0), (OFF_CONF, o_conf), (OFF_SC, o_sc), (OFF_MERGE, o_mg)):
            if z0 >= dst:
                src.append(lo + z0 - dst)
                break
    assert all(v % SUBLANE == 0 and v + REGROUP_ROWS <= n_in for v in src)
    grid_spec = pltpu.PrefetchScalarGridSpec(
        num_scalar_prefetch=1,
        grid=(n_layers, Z_WIDTH // REGROUP_ROWS),
        in_specs=[pl.BlockSpec((pl.Element(1), pl.Element(REGROUP_ROWS), pl.Element(d)),
                               lambda l, r, tab: (l, tab[r] * SUBLANE, 0))],
        out_specs=pl.BlockSpec((1, REGROUP_ROWS, d), lambda l, r, tab: (l, r, 0)),
    )
    return pl.pallas_call(
        functools.partial(_regroup_kernel, n_gate=o_conf - o_ng),
        grid_spec=grid_spec,
        out_shape=jax.ShapeDtypeStruct((n_layers, Z_WIDTH, d), jnp.bfloat16),
        compiler_params=_cparams(("parallel", "parallel")),
        name="regroup_w_in",
    )(jnp.asarray(src, jnp.int32) // SUBLANE, jnp.swapaxes(w, 1, 2))


def _regroup_kernel(tab_ref, w_ref, o_ref, *, n_gate):
    del tab_ref
    last = pl.program_id(1) == pl.num_programs(1) - 1
    row = lax.broadcasted_iota(jnp.int32, w_ref.shape[1:], 0)
    keep = jnp.logical_or(jnp.logical_not(last), row < n_gate)
    o_ref[0] = jnp.where(keep, w_ref[0], 0.0).astype(o_ref.dtype)


def _route_plan(route, counts, t, tmd):
    cnt = counts[:N_EXPERTS].astype(jnp.int32)
    padded = ((cnt + TMX - 1) // TMX) * TMX
    ends = jnp.cumsum(padded)
    starts = ends - padded
    e_idx = jnp.arange(N_EXPERTS, dtype=jnp.int32)

    def lookup(table, idx):
        return jnp.sum(jnp.where(idx[..., None] == e_idx, table, 0), axis=-1)

    dest = [lookup(starts, route[:, R_E1 + k].astype(jnp.int32).reshape(t // tmd, tmd))
            + route[:, R_RANK1 + k].astype(jnp.int32).reshape(t // tmd, tmd) for k in range(2)]
    dest3 = jnp.stack(dest, axis=1).reshape(t // tmd, 1, 2 * tmd)
    p_rows = 2 * t + N_EXPERTS * TMX
    n_tiles = p_rows // TMX
    tile_start = jnp.arange(n_tiles, dtype=jnp.int32) * TMX
    tile_e = jnp.minimum(jnp.sum((tile_start[:, None] >= ends[None, :]).astype(jnp.int32), axis=1), N_EXPERTS - 1)
    tile_v = (tile_start < ends[-1]).astype(jnp.int32)
    tile_b = jnp.where(tile_v == 1, jnp.arange(n_tiles, dtype=jnp.int32), 0)
    tile_f = tile_v * jnp.concatenate([jnp.ones((1,), jnp.int32), (tile_e[1:] != tile_e[:-1]).astype(jnp.int32)])
    has = padded > 0
    later = jnp.where(has[None, :] & (e_idx[None, :] > e_idx[:, None]), e_idx[None, :], N_EXPERTS)
    next_e = jnp.min(later, axis=1)
    next_e = jnp.where(next_e < N_EXPERTS, next_e, -1)
    seg_slot = (jnp.cumsum(has.astype(jnp.int32)) - 1) % 2
    tile_meta = jnp.stack([tile_e, tile_v, tile_b, tile_f, lookup(next_e, tile_e), lookup(seg_slot, tile_e)])
    zstart = jnp.concatenate([jnp.where(padded > cnt, ends - TMX, -1), ends[-1:]]).astype(jnp.int32)
    return dest3, tile_meta.astype(jnp.int32), zstart, p_rows


def kernel(x, p, rel_bias, attn_norm, w_in, cmp_pe, cmp_w1, cmp_w2, conf_conv_w, conf_conv_b, conf_ln_g, conf_ln_b, sc_conv_w, w_branch, w_out, ffn_norm, router_group_w, router_group_b, router_expert_w, router_expert_b, expert_w_gate, expert_w_up, expert_w_down, ple_norm, ple_gate_w, ple_proj_w, final_norm):
    b, s, d = x.shape
    t = b * s
    depth = w_in.shape[0]
    bf16 = jnp.bfloat16
    ncp = s // CMP_STRIDE
    rows4 = HEADS_PER_GROUP * TQ

    cmp_tab = _bias_tables(rel_bias, 1, s, min(1024, s), ncp, col_mult=CMP_STRIDE, d0_base=-(CMP_BLOCK - 1), d0_step=0,
                           lo=0, hi=1 << 30, out_scale=1.0)
    toe = _bias_tables(rel_bias, 6, TQ, TQ, LANE, col_mult=1, d0_base=-LANE, d0_step=LANE,
                       lo=0, hi=WINDOW, out_scale=LOG2E)
    toe = toe.reshape(KV_HEADS, HEADS_PER_GROUP, 6, TQ, LANE).transpose(0, 2, 1, 3, 4).reshape(KV_HEADS, 6, rows4, LANE)
    win_tab = jnp.stack([toe[:, N_PREV - jj + 1] for jj in range(N_PREV + 2)], axis=1)
    n_sel_tab = SEL_TQ // SEL_TK + 2
    sel_tab = _bias_tables(rel_bias, n_sel_tab, SEL_TQ, SEL_TQ, SEL_TK, col_mult=1,
                           d0_base=(1 - SEL_TQ // SEL_TK) * SEL_TK, d0_step=SEL_TK, lo=0, hi=1 << 30, out_scale=LOG2E)
    sel_tab = sel_tab.reshape(KV_HEADS, HEADS_PER_GROUP, n_sel_tab, SEL_TQ, SEL_TK).transpose(0, 2, 1, 3, 4)
    sel_tab = sel_tab.reshape(KV_HEADS, n_sel_tab, HEADS_PER_GROUP * SEL_TQ, SEL_TK)

    w_z = _regroup_w_in(w_in)
    x2d = x.reshape(t, d)
    tm_in = min(1024, t)
    tm = min(512, t)
    tmd = min(256, t)
    for i in range(depth):
        n_pad = ROUTE_LANES - N_EXPERTS - N_GROUPS
        wr = jnp.concatenate([router_expert_w[i], router_group_w[i], jnp.zeros((d, n_pad), jnp.float32)], axis=1)
        wr_hi = wr.astype(bf16)
        wr = jnp.concatenate([wr_hi, (wr - wr_hi.astype(jnp.float32)).astype(bf16)], axis=1)
        br = jnp.concatenate([router_expert_b[i], router_group_b[i], jnp.zeros((n_pad,), jnp.float32)]).reshape(1, -1)

        z, kcv = _in_proj(x2d, attn_norm[i].reshape(1, d), w_z, i, tm_in, 2048)
        cmp_kv = _compress(kcv, cmp_pe[i], cmp_w1[i].astype(bf16), cmp_w2[i].astype(bf16), b, s)
        o_cmp, neg_t = _cmp_attn(z, cmp_kv, cmp_tab, b, s)
        neg = jnp.swapaxes(neg_t, 2, 3)
        o_slc = _flash(z, sel_tab, neg, b, s, selected=True)
        o_win = _flash(z, win_tab, None, b, s, selected=False)
        uconv, o_sc = _conv(z, conf_conv_w[i], conf_conv_b[i].reshape(1, -1), sc_conv_w[i], b, s, min(512, s), 256)
        merged = _merge(z, o_cmp, o_slc, o_win, uconv, conf_ln_g[i].reshape(1, -1), conf_ln_b[i].reshape(1, -1),
                        o_sc, w_branch[i].astype(bf16), tm, 1024)

        x1, h2, route, cnts = _out_router(x2d, merged, w_out[i].astype(bf16), ffn_norm[i].reshape(1, d), wr, br, tm)
        dest3, tile_meta, zstart, p_rows = _route_plan(route, cnts[-1], t, tmd)
        xs = _dispatch(zstart, dest3, h2, p_rows, tmd)
        y = _experts(tile_meta, xs, expert_w_gate, expert_w_up, expert_w_down, i)

        x2d = _combine_ple(dest3, x1, route, y, p[i].reshape(t, PLE_DIM), ple_norm[i].reshape(1, d),
                           ple_gate_w[i].astype(bf16), ple_proj_w[i].astype(bf16), final_norm.reshape(1, d),
                           tmd, i == depth - 1)
    return x2d.reshape(b, s, d)
```
